```python
import jax, jax.numpy as jnp
from jax import lax
import numpy as np

D_MODEL = 1024
BATCH = 2
SEQ = 8192
DEPTH = 4

GRID_W = 64
CTX_LEN = 256
HEAD_DIM = 64
A_HEADS = 8
A_KV_HEADS = 2
A_GROUP = A_HEADS // A_KV_HEADS
B_HEADS = 4
C_GROUPS = 4
C_GROUP_DIM = 64
C_WIDTH = C_GROUPS * C_GROUP_DIM
POOL_WINDOWS = (2, 4, 8, 16)
NA_WIN_ROWS = 8
NA_WIN_COLS = 16
Q_BLOCK = 128
ROPE_THETA = 10000.0
AXIS_ROPE_DIM = HEAD_DIM // 2
N_BRANCHES = 3
D_FF = 2816
N_EXPERTS = 8
TOP_K = 2
D_FF_EXPERT = 3584
N_DENSE_LAYERS = (DEPTH + 1) // 2
N_MOE_LAYERS = DEPTH // 2
EPS = 1e-6

A_Q = A_HEADS * HEAD_DIM
A_KV = A_KV_HEADS * HEAD_DIM
B_W = B_HEADS * HEAD_DIM
IN_SPLITS = (A_Q, A_KV, A_KV, B_W, B_W, B_W, C_WIDTH, N_BRANCHES * D_MODEL)
IN_COLS = sum(IN_SPLITS)

kernel_name = "hybrid_parallel_gqa_natten_pool_moe_diffusion"


def rms_norm(x, g):
    xf = x.astype(jnp.float32)
    y = xf * lax.rsqrt(jnp.mean(xf * xf, axis=-1, keepdims=True) + EPS)
    return (y * g.astype(jnp.float32)).astype(x.dtype)


def modulate(h, shift, scale):
    return h * (1 + scale) + shift


def axial_rope_tables(n_tok):
    t = jnp.arange(n_tok, dtype=jnp.int32)
    pos = jnp.stack([t // GRID_W, t % GRID_W], axis=-1).astype(jnp.float32)
    inv_freq = ROPE_THETA ** (-jnp.arange(0, AXIS_ROPE_DIM, 2, dtype=jnp.float32) / AXIS_ROPE_DIM)
    ang = pos[:, :, None] * inv_freq[None, None, :]
    return jnp.cos(ang), jnp.sin(ang)


def apply_axial_rope(x, cos, sin):
    B, S, H, _ = x.shape
    xr = x.reshape(B, S, H, 2, 2, AXIS_ROPE_DIM // 2).astype(jnp.float32)
    x1, x2 = xr[..., 0, :], xr[..., 1, :]
    cs = cos[None, :, None]
    sn = sin[None, :, None]
    out = jnp.stack([x1 * cs - x2 * sn, x2 * cs + x1 * sn], axis=-2)
    return out.reshape(x.shape).astype(x.dtype)


def gqa_core(q, k, v):
    s = jnp.einsum('bqkgd,bnkd->bkgqn', q, k).astype(jnp.float32) * (HEAD_DIM ** -0.5)
    p = jax.nn.softmax(s, axis=-1).astype(v.dtype)
    return jnp.einsum('bkgqn,bnkd->bqkgd', p, v)


def gqa_axial_attention(q_x, k_x, v_x, q_c, k_c, v_c, cos, sin):
    B, S = q_x.shape[:2]
    q_x = apply_axial_rope(q_x, cos, sin)
    k_x = apply_axial_rope(k_x, cos, sin)
    k_all = jnp.concatenate([k_c, k_x], axis=1)
    v_all = jnp.concatenate([v_c, v_x], axis=1)
    qb = q_x.reshape(B, S // Q_BLOCK, Q_BLOCK, A_KV_HEADS, A_GROUP, HEAD_DIM).transpose(1, 0, 2, 3, 4, 5)
    o = lax.map(lambda qblk: gqa_core(qblk, k_all, v_all), qb)
    o_x = o.transpose(1, 0, 2, 3, 4, 5).reshape(B, S, A_Q)
    o_c = None
    if q_c is not None:
        L = q_c.shape[1]
        o_c = gqa_core(q_c.reshape(B, L, A_KV_HEADS, A_GROUP, HEAD_DIM), k_c, v_c).reshape(B, L, A_Q)
    return o_x, o_c


def neighbourhood_attention(q_x, k_x, v_x, q_c, k_c, v_c, rel_bias):
    B, S = q_x.shape[:2]
    rows = S // GRID_W
    wr = min(NA_WIN_ROWS, rows)
    n_win = wr * NA_WIN_COLS
    qg = q_x.reshape(B, rows, GRID_W, B_HEADS, HEAD_DIM)
    kg = k_x.reshape(B, rows, GRID_W, B_HEADS, HEAD_DIM)
    vg = v_x.reshape(B, rows, GRID_W, B_HEADS, HEAD_DIM)
    col = np.arange(GRID_W)
    col_start = np.clip(col - NA_WIN_COLS // 2, 0, GRID_W - NA_WIN_COLS)
    col_idx = col_start[:, None] + np.arange(NA_WIN_COLS)[None, :]
    dc_idx = col_idx - col[:, None] + (NA_WIN_COLS - 1)
    bias_c = rel_bias[:, :, dc_idx]
    scale = HEAD_DIM ** -0.5

    def one_row(r):
        rs = jnp.clip(r - wr // 2, 0, rows - wr)
        k_rows = lax.dynamic_slice_in_dim(kg, rs, wr, axis=1)
        v_rows = lax.dynamic_slice_in_dim(vg, rs, wr, axis=1)
        k_win = k_rows[:, :, col_idx]
        v_win = v_rows[:, :, col_idx]
        q_row = lax.dynamic_index_in_dim(qg, r, axis=1, keepdims=False)
        dr_idx = rs + jnp.arange(wr) - r + (NA_WIN_ROWS - 1)
        bias = jnp.take(bias_c, dr_idx, axis=1).transpose(0, 2, 1, 3)
        s_win = jnp.einsum('bjhd,brjchd->bhjrc', q_row, k_win).astype(jnp.float32) * scale
        s_win = (s_win + bias[None].astype(jnp.float32)).reshape(B, B_HEADS, GRID_W, n_win)
        s_ctx = jnp.einsum('bjhd,blhd->bhjl', q_row, k_c).astype(jnp.float32) * scale
        p = jax.nn.softmax(jnp.concatenate([s_win, s_ctx], axis=-1), axis=-1).astype(v_x.dtype)
        p_win = p[..., :n_win].reshape(B, B_HEADS, GRID_W, wr, NA_WIN_COLS)
        p_ctx = p[..., n_win:]
        o = jnp.einsum('bhjrc,brjchd->bjhd', p_win, v_win) + jnp.einsum('bhjl,blhd->bjhd', p_ctx, v_c)
        return o.reshape(B, GRID_W, B_W)

    o = lax.map(one_row, jnp.arange(rows))
    o_x = o.transpose(1, 0, 2, 3).reshape(B, S, B_W)
    o_c = None
    if q_c is not None:
        L = q_c.shape[1]
        o_c = gqa_core(q_c[:, :, :, None, :], k_c, v_c).reshape(B, L, B_W)
    return o_x, o_c


def pool_mixer(u, w, scale):
    B, N, _ = u.shape
    ug = u.reshape(B, N, C_GROUPS, C_GROUP_DIM)
    csum = jnp.cumsum(ug.astype(jnp.float32), axis=1)
    csum = jnp.concatenate([jnp.zeros((B, 1, C_GROUPS, C_GROUP_DIM), jnp.float32), csum], axis=1)
    t = np.arange(N)[:, None]
    win = np.array(POOL_WINDOWS)[None, :]
    lo = np.clip(t - win // 2, 0, N)
    hi = np.clip(t - win // 2 + win, 0, N)
    g = np.arange(C_GROUPS)[None, :]
    cnt = (hi - lo).astype(np.float32)[None, :, :, None]
    mean = (csum[:, hi, g] - csum[:, lo, g]) / cnt
    d = (mean - ug.astype(jnp.float32)).astype(u.dtype)
    y = jnp.einsum('bngc,gcd->bngd', d, w)
    return y.reshape(B, N, C_WIDTH) * scale


def branch_merge(y_a, y_b, y_c, gates, w_pa, w_pb, w_pc, w_out):
    g_a, g_b, g_c = jnp.split(gates, N_BRANCHES, axis=-1)
    m = (jax.nn.sigmoid(g_a) * (y_a @ w_pa) + jax.nn.sigmoid(g_b) * (y_b @ w_pb)
         + jax.nn.sigmoid(g_c) * (y_c @ w_pc))
    return m @ w_out


def token_mixers(px, pc, q_g, k_g, rel_bias, pool_w, pool_scale, w_pa, w_pb, w_pc, w_out, cos, sin, want_ctx):
    offs = np.cumsum(IN_SPLITS)[:-1].tolist()
    aq, ak, av, bq, bk, bv, cu, gates = jnp.split(px, offs, axis=-1)
    caq, cak, cav, cbq, cbk, cbv, ccu, cgates = jnp.split(pc, offs, axis=-1)

    def heads(t, h):
        return t.reshape(t.shape[0], t.shape[1], h, HEAD_DIM)

    qa = rms_norm(heads(aq, A_HEADS), q_g)
    ka = rms_norm(heads(ak, A_KV_HEADS), k_g)
    cka = rms_norm(heads(cak, A_KV_HEADS), k_g)
    cqa = rms_norm(heads(caq, A_HEADS), q_g) if want_ctx else None
    y_a, cy_a = gqa_axial_attention(qa, ka, heads(av, A_KV_HEADS), cqa, cka, heads(cav, A_KV_HEADS), cos, sin)
    cqb = heads(cbq, B_HEADS) if want_ctx else None
    y_b, cy_b = neighbourhood_attention(heads(bq, B_HEADS), heads(bk, B_HEADS), heads(bv, B_HEADS),
                                        cqb, heads(cbk, B_HEADS), heads(cbv, B_HEADS), rel_bias)
    y_c = pool_mixer(cu, pool_w, pool_scale)
    out_x = branch_merge(y_a, y_b, y_c, gates, w_pa, w_pb, w_pc, w_out)
    out_c = None
    if want_ctx:
        cy_c = pool_mixer(ccu, pool_w, pool_scale)
        out_c = branch_merge(cy_a, cy_b, cy_c, cgates, w_pa, w_pb, w_pc, w_out)
    return out_x, out_c


def swiglu(h, w1, w3, w2):
    return (jax.nn.silu(h @ w1) * (h @ w3)) @ w2


def moe_swiglu(h, router_w, router_b, w1, w3, w2):
    logits = (h @ router_w).astype(jnp.float32) + router_b.astype(jnp.float32)
    top_val, top_idx = lax.top_k(logits, TOP_K)
    top_p = jax.nn.softmax(top_val, axis=-1)
    combine = jnp.sum(jax.nn.one_hot(top_idx, N_EXPERTS, dtype=jnp.float32) * top_p[..., None], axis=-2)
    out = jnp.zeros_like(h)
    for e in range(N_EXPERTS):
        out = out + combine[..., e:e + 1].astype(h.dtype) * swiglu(h, w1[e], w3[e], w2[e])
    return out


def setup_inputs(seed: int = 0) -> dict:
    key = jax.random.key(seed)
    ks = jax.random.split(key, 32)
    f32 = jnp.float32
    D = D_MODEL

    def nrm(k, shape, scale):
        return jax.random.normal(k, shape, f32) * scale

    return {
        "x": nrm(ks[0], (BATCH, SEQ, D), 1.0),
        "c": nrm(ks[1], (BATCH, D), 1.0),
        "ctx": nrm(ks[2], (BATCH, CTX_LEN, D), 1.0),
        "c_ctx": nrm(ks[3], (D,), 1.0),
        "w_mod": nrm(ks[4], (DEPTH, D, 6 * D), 0.5 * D ** -0.5),
        "b_mod": nrm(ks[5], (DEPTH, 6 * D), 0.02),
        "norm1_g": 1.0 + nrm(ks[6], (DEPTH, D), 0.02),
        "norm2_g": 1.0 + nrm(ks[7], (DEPTH, D), 0.02),
        "w_in": nrm(ks[8], (DEPTH, D, IN_COLS), D ** -0.5),
        "q_norm_g": 1.0 + nrm(ks[9], (DEPTH, HEAD_DIM), 0.02),
        "k_norm_g": 1.0 + nrm(ks[10], (DEPTH, HEAD_DIM), 0.02),
        "na_rel_bias": nrm(ks[11], (DEPTH, B_HEADS, 2 * NA_WIN_ROWS - 1, 2 * NA_WIN_COLS - 1), 0.5),
        "pool_w": nrm(ks[12], (DEPTH, C_GROUPS, C_GROUP_DIM, C_GROUP_DIM), C_GROUP_DIM ** -0.5),
        "pool_scale": 1.0 + nrm(ks[13], (DEPTH, C_WIDTH), 0.1),
        "w_branch_a": nrm(ks[14], (DEPTH, A_Q, D), A_Q ** -0.5),
        "w_branch_b": nrm(ks[15], (DEPTH, B_W, D), B_W ** -0.5),
        "w_branch_c": nrm(ks[16], (DEPTH, C_WIDTH, D), C_WIDTH ** -0.5),
        "w_out": nrm(ks[17], (DEPTH, D, D), D ** -0.5),
        "ffn_w1": nrm(ks[18], (N_DENSE_LAYERS, D, D_FF), D ** -0.5),
        "ffn_w3": nrm(ks[19], (N_DENSE_LAYERS, D, D_FF), D ** -0.5),
        "ffn_w2": nrm(ks[20], (N_DENSE_LAYERS, D_FF, D), D_FF ** -0.5),
        "router_w": nrm(ks[21], (N_MOE_LAYERS, D, N_EXPERTS), D ** -0.5),
        "router_b": nrm(ks[22], (N_MOE_LAYERS, N_EXPERTS), 0.01),
        "moe_w1": nrm(ks[23], (N_MOE_LAYERS, N_EXPERTS, D, D_FF_EXPERT), D ** -0.5),
        "moe_w3": nrm(ks[24], (N_MOE_LAYERS, N_EXPERTS, D, D_FF_EXPERT), D ** -0.5),
        "moe_w2": nrm(ks[25], (N_MOE_LAYERS, N_EXPERTS, D_FF_EXPERT, D), D_FF_EXPERT ** -0.5),
        "final_g": 1.0 + nrm(ks[26], (D,), 0.02),
    }


def reference(x, c, ctx, c_ctx, w_mod, b_mod, norm1_g, norm2_g, w_in, q_norm_g, k_norm_g, na_rel_bias,
              pool_w, pool_scale, w_branch_a, w_branch_b, w_branch_c, w_out, ffn_w1, ffn_w3, ffn_w2,
              router_w, router_b, moe_w1, moe_w3, moe_w2, final_g):
    S = x.shape[1]
    L = ctx.shape[1]
    cos, sin = axial_rope_tables(S)
    xc = ctx
    silu_c = jax.nn.silu(c)
    silu_cc = jax.nn.silu(c_ctx)
    for layer in range(DEPTH):
        want_ctx = layer < DEPTH - 1
        mod = silu_c @ w_mod[layer] + b_mod[layer]
        cmod = silu_cc @ w_mod[layer] + b_mod[layer]
        sh1, sc1, g1, sh2, sc2, g2 = jnp.split(mod[:, None, :], 6, axis=-1)
        csh1, csc1, cg1, csh2, csc2, cg2 = jnp.split(cmod, 6, axis=-1)

        hx = modulate(rms_norm(x, norm1_g[layer]), sh1, sc1)
        hc = modulate(rms_norm(xc, norm1_g[layer]), csh1, csc1)
        mx, mc = token_mixers(hx @ w_in[layer], hc @ w_in[layer], q_norm_g[layer], k_norm_g[layer],
                              na_rel_bias[layer], pool_w[layer], pool_scale[layer], w_branch_a[layer],
                              w_branch_b[layer], w_branch_c[layer], w_out[layer], cos, sin, want_ctx)
        x = x + g1 * mx
        hx = modulate(rms_norm(x, norm2_g[layer]), sh2, sc2)
        if want_ctx:
            xc = xc + cg1 * mc
            hc = modulate(rms_norm(xc, norm2_g[layer]), csh2, csc2)
            h_all = jnp.concatenate([hc, hx], axis=1)
        else:
            h_all = hx
        if layer % 2 == 0:
            i = layer // 2
            f = swiglu(h_all, ffn_w1[i], ffn_w3[i], ffn_w2[i])
        else:
            i = layer // 2
            f = moe_swiglu(h_all, router_w[i], router_b[i], moe_w1[i], moe_w3[i], moe_w2[i])
        if want_ctx:
            xc = xc + cg2 * f[:, :L]
            fx = f[:, L:]
        else:
            fx = f
        x = x + g2 * fx
    return rms_norm(x, final_g)
```

```python
import functools

import numpy as np
import jax
import jax.numpy as jnp
from jax import lax
from jax.experimental import pallas as pl
from jax.experimental.pallas import tpu as pltpu

F32 = jnp.float32
BF16 = jnp.bfloat16

D_MODEL = 1024
DEPTH = 4
GRID_W = 64
HEAD_DIM = 64
A_HEADS = 8
A_KV_HEADS = 2
A_GROUP = A_HEADS // A_KV_HEADS
B_HEADS = 4
C_GROUPS = 4
C_GROUP_DIM = 64
C_WIDTH = C_GROUPS * C_GROUP_DIM
POOL_WINDOWS = (2, 4, 8, 16)
NA_WIN_ROWS = 8
NA_WIN_COLS = 16
ROPE_THETA = 10000.0
AXIS_ROPE_DIM = HEAD_DIM // 2
N_EXPERTS = 8
EPS = 1e-6

A_Q = A_HEADS * HEAD_DIM
A_KV = A_KV_HEADS * HEAD_DIM
B_W = B_HEADS * HEAD_DIM
QK_W = A_Q + A_KV
OFF_AV = QK_W
OFF_BQ = OFF_AV + A_KV
OFF_BK = OFF_BQ + B_W
OFF_BV = OFF_BK + B_W
OFF_CU = OFF_BV + B_W
OFF_G = OFF_CU + C_WIDTH
IN_COLS = OFF_G + 3 * D_MODEL

LANES = 128
SUBLANES = 8
TM = 256
TF = 512
TG = 256
FC = 512
POOL_HALO = 8
MASK_VALUE = -1e30
VMEM_LIMIT = 56 * 1024 * 1024


def _cparams(sem):
    return pltpu.CompilerParams(dimension_semantics=sem, vmem_limit_bytes=VMEM_LIMIT)


def _dot(a, b):
    return jnp.dot(a, b, preferred_element_type=F32)


def _dot_nt(a, b):
    return lax.dot_general(a, b, (((1,), (1,)), ((), ())), preferred_element_type=F32)


def _split(a):
    hi = a.astype(BF16)
    lo = (a - hi.astype(F32)).astype(BF16)
    return hi, lo


def _dot3(a, b_hi, b_lo):
    a_hi, a_lo = _split(a)
    return _dot(a_hi, b_hi) + (_dot(a_lo, b_hi) + _dot(a_hi, b_lo))


def _const_spec(shape):
    n = len(shape)
    return pl.BlockSpec(shape, lambda *_: (0,) * n, pipeline_mode=pl.Buffered(1))


def _mod_kernel(c_ref, w_ref, b_ref, o_ref):
    c = c_ref[...]
    s = c * jax.nn.sigmoid(c)
    w_hi, w_lo = _split(w_ref[0])
    o_ref[0] = _dot3(s, w_hi, w_lo) + b_ref[0]


def _mod_vectors(cvec, w_mod, b_mod):
    depth, d, n = w_mod.shape
    tn = 1536
    return pl.pallas_call(
        _mod_kernel,
        grid=(depth, n // tn),
        in_specs=[pl.BlockSpec((SUBLANES, d), lambda l, j: (0, 0)),
                  pl.BlockSpec((1, d, tn), lambda l, j: (l, 0, j)),
                  pl.BlockSpec((1, 1, tn), lambda l, j: (l, 0, j))],
        out_specs=pl.BlockSpec((1, SUBLANES, tn), lambda l, j: (l, 0, j)),
        out_shape=jax.ShapeDtypeStruct((depth, SUBLANES, n), F32),
        compiler_params=_cparams(("arbitrary", "arbitrary")),
        name="mod_vectors",
    )(cvec, w_mod, b_mod.reshape(depth, 1, n))


def _swap16(x):
    lane = lax.broadcasted_iota(jnp.int32, (1, LANES), 1)
    first = (lane % 32) < 16
    return jnp.where(first, pltpu.roll(x, LANES - 16, 1), pltpu.roll(x, 16, 1))


def _inproj_kernel(x_ref, mod_ref, g_ref, w_ref, bd_ref, gqk_ref, cos_ref, sin_ref,
                   q_ref, kt_ref, v_ref, bq_ref, bk_ref, bv_ref, cu_ref, sg_ref):
    x = x_ref[0]
    m = mod_ref[0]
    y = x * lax.rsqrt(jnp.mean(x * x, axis=-1, keepdims=True) + EPS) * g_ref[...]
    h = y * (1.0 + m[1:2]) + m[0:1]
    px = _dot(h.astype(BF16), w_ref[...])

    qk = px[:, :QK_W]
    sq_hi, sq_lo = _split(qk * qk)
    ss = _dot(sq_hi, bd_ref[...]) + _dot(sq_lo, bd_ref[...])
    qn = qk * lax.rsqrt(ss * (1.0 / HEAD_DIM) + EPS) * gqk_ref[...]
    cos = cos_ref[...]
    sin = sin_ref[...]
    chunks = []
    for j in range(QK_W // LANES):
        c = qn[:, j * LANES:(j + 1) * LANES]
        chunks.append(c * cos + _swap16(c) * sin)
    scale = HEAD_DIM ** -0.5
    for j in range(A_Q // LANES):
        q_ref[0, :, j * LANES:(j + 1) * LANES] = (chunks[j] * scale).astype(BF16)
    kt_ref[0, 0] = chunks[A_Q // LANES].T.astype(BF16)

    v_ref[0] = px[:, OFF_AV:OFF_BQ].astype(BF16)
    bq_ref[0] = (px[:, OFF_BQ:OFF_BK] * scale).astype(BF16)
    bk_ref[0] = px[:, OFF_BK:OFF_BV].astype(BF16)
    bv_ref[0] = px[:, OFF_BV:OFF_CU].astype(BF16)
    cu_ref[0] = px[:, OFF_CU:OFF_G]
    sg_ref[0] = jax.nn.sigmoid(px[:, OFF_G:])


def _inproj(xs, modr, g1, w_in, bd, gqk, cos_t, sin_t, nct):
    B, T, D = xs.shape
    nt = T // TM
    tok = lambda w: pl.BlockSpec((1, TM, w), lambda b, i: (b, i, 0))
    out_shape = (
        jax.ShapeDtypeStruct((B, T, A_Q), BF16),
        jax.ShapeDtypeStruct((B, nt, A_KV, TM), BF16),
        jax.ShapeDtypeStruct((B, T, A_KV), BF16),
        jax.ShapeDtypeStruct((B, T, B_W), BF16),
        jax.ShapeDtypeStruct((B, T, B_W), BF16),
        jax.ShapeDtypeStruct((B, T, B_W), BF16),
        jax.ShapeDtypeStruct((B, T, C_WIDTH), F32),
        jax.ShapeDtypeStruct((B, T, 3 * D), F32),
    )
    return pl.pallas_call(
        _inproj_kernel,
        grid=(B, nt),
        in_specs=[tok(D),
                  pl.BlockSpec((1, 6, D), lambda b, i: (jnp.where(i < nct, B, b), 0, 0)),
                  _const_spec((1, D)),
                  _const_spec((D, IN_COLS)),
                  _const_spec((QK_W, QK_W)),
                  _const_spec((1, QK_W)),
                  pl.BlockSpec((TM, LANES), lambda b, i: (i, 0)),
                  pl.BlockSpec((TM, LANES), lambda b, i: (i, 0))],
        out_specs=(tok(A_Q),
                   pl.BlockSpec((1, 1, A_KV, TM), lambda b, i: (b, i, 0, 0)),
                   tok(A_KV), tok(B_W), tok(B_W), tok(B_W), tok(C_WIDTH), tok(3 * D)),
        out_shape=out_shape,
        compiler_params=_cparams(("arbitrary", "arbitrary")),
        name="inproj",
    )(xs, modr, g1, w_in, bd, gqk, cos_t, sin_t)


def _attn_a_kernel(q_ref, kt_ref, v_ref, o_ref, *, nct, nk):
    i = pl.program_id(1)
    n_steps = jnp.where(i < nct, nct, nk)
    rows = A_GROUP * TM
    for j in range(A_KV_HEADS):
        base = j * A_GROUP * HEAD_DIM
        q4 = jnp.concatenate(
            [q_ref[0, :, base + g * HEAD_DIM: base + (g + 1) * HEAD_DIM] for g in range(A_GROUP)], axis=0)

        def body(kk, carry, j=j, q4=q4):
            m, l, acc = carry
            kt = kt_ref[0, kk, j * HEAD_DIM:(j + 1) * HEAD_DIM, :]
            s = _dot(q4, kt)
            m_new = jnp.maximum(m, jnp.max(s, axis=-1, keepdims=True))
            p = jnp.exp(s - m_new)
            alpha = jnp.exp(m - m_new)
            l = alpha * l + jnp.sum(p, axis=-1, keepdims=True)
            vv = v_ref[0, pl.ds(pl.multiple_of(kk * TM, TM), TM), :]
            acc = alpha * acc + _dot(p.astype(BF16), vv)
            return m_new, l, acc

        init = (jnp.full((rows, 1), MASK_VALUE, F32), jnp.zeros((rows, 1), F32),
                jnp.zeros((rows, A_KV), F32))
        _, l, acc = lax.fori_loop(0, n_steps, body, init)
        o = acc[:, j * HEAD_DIM:(j + 1) * HEAD_DIM] / l
        for g in range(A_GROUP):
            o_ref[0, :, base + g * HEAD_DIM: base + (g + 1) * HEAD_DIM] = o[g * TM:(g + 1) * TM].astype(BF16)


def _attn_a(q, kt, v, nct):
    B, T, _ = q.shape
    nt = T // TM
    return pl.pallas_call(
        functools.partial(_attn_a_kernel, nct=nct, nk=nt),
        grid=(B, nt),
        in_specs=[pl.BlockSpec((1, TM, A_Q), lambda b, i: (b, i, 0)),
                  pl.BlockSpec((1, nt, A_KV, TM), lambda b, i: (b, 0, 0, 0)),
                  pl.BlockSpec((1, T, A_KV), lambda b, i: (b, 0, 0))],
        out_specs=pl.BlockSpec((1, TM, A_Q), lambda b, i: (b, i, 0)),
        out_shape=jax.ShapeDtypeStruct((B, T, A_Q), BF16),
        compiler_params=_cparams(("arbitrary", "arbitrary")),
        name="attn_a",
    )(q, kt, v)


def _attn_b_kernel(q_ref, kp_ref, kc_ref, kn_ref, vp_ref, vc_ref, vn_ref, kx_ref, vx_ref, bias_ref,
                   o_ref, kbuf, vbuf, *, nct, grid_rows):
    i = pl.program_id(1)
    rpt = TM // GRID_W
    win = NA_WIN_ROWS * GRID_W

    @pl.when(i < nct)
    def _():
        for h in range(B_HEADS):
            sl = slice(h * HEAD_DIM, (h + 1) * HEAD_DIM)
            s = _dot_nt(q_ref[0, :, sl], kx_ref[0, :, sl])
            m = jnp.max(s, axis=-1, keepdims=True)
            p = jnp.exp(s - m)
            l = jnp.sum(p, axis=-1, keepdims=True)
            o_ref[0, :, sl] = (_dot(p.astype(BF16), vx_ref[0, :, sl]) / l).astype(BF16)

    @pl.when(i >= nct)
    def _():
        kbuf[0:TM] = kp_ref[0]
        kbuf[TM:2 * TM] = kc_ref[0]
        kbuf[2 * TM:3 * TM] = kn_ref[0]
        vbuf[0:TM] = vp_ref[0]
        vbuf[TM:2 * TM] = vc_ref[0]
        vbuf[2 * TM:3 * TM] = vn_ref[0]
        r0 = (i - nct) * rpt

        def row_body(a, carry):
            qr = r0 + a
            rs = jnp.clip(qr - NA_WIN_ROWS // 2, 0, grid_rows - NA_WIN_ROWS)
            case = qr - rs
            start = pl.multiple_of((rs - r0 + rpt) * GRID_W, GRID_W)
            qs = pl.multiple_of(a * GRID_W, GRID_W)
            kw = kbuf[pl.ds(start, win), :]
            vw = vbuf[pl.ds(start, win), :]
            qrow = q_ref[0, pl.ds(qs, GRID_W), :]
            for h in range(B_HEADS):
                sl = slice(h * HEAD_DIM, (h + 1) * HEAD_DIM)
                s_w = _dot_nt(qrow[:, sl], kw[:, sl]) + bias_ref[case, h]
                s_c = _dot_nt(qrow[:, sl], kx_ref[0, :, sl])
                m = jnp.maximum(jnp.max(s_w, axis=-1, keepdims=True), jnp.max(s_c, axis=-1, keepdims=True))
                p_w = jnp.exp(s_w - m)
                p_c = jnp.exp(s_c - m)
                l = jnp.sum(p_w, axis=-1, keepdims=True) + jnp.sum(p_c, axis=-1, keepdims=True)
                o = _dot(p_w.astype(BF16), vw[:, sl]) + _dot(p_c.astype(BF16), vx_ref[0, :, sl])
                o_ref[0, pl.ds(qs, GRID_W), sl] = (o / l).astype(BF16)
            return carry

        lax.fori_loop(0, rpt, row_body, 0)


def _attn_b(bq, bk, bv, bias_t, nct, L):
    B, T, _ = bq.shape
    nt = T // TM
    grid_rows = (T - L) // GRID_W
    cur = lambda b, i: (b, i, 0)
    prev = lambda b, i: (b, jnp.maximum(i - 1, nct), 0)
    nxt = lambda b, i: (b, jnp.minimum(i + 1, nt - 1), 0)
    blk = lambda f: pl.BlockSpec((1, TM, B_W), f)
    ctx = pl.BlockSpec((1, L, B_W), lambda b, i: (b, 0, 0))
    return pl.pallas_call(
        functools.partial(_attn_b_kernel, nct=nct, grid_rows=grid_rows),
        grid=(B, nt),
        in_specs=[blk(cur), blk(prev), blk(cur), blk(nxt), blk(prev), blk(cur), blk(nxt), ctx, ctx,
                  _const_spec(bias_t.shape)],
        out_specs=blk(cur),
        out_shape=jax.ShapeDtypeStruct((B, T, B_W), BF16),
        scratch_shapes=[pltpu.VMEM((3 * TM, B_W), BF16), pltpu.VMEM((3 * TM, B_W), BF16)],
        compiler_params=_cparams(("arbitrary", "arbitrary")),
        name="attn_b",
    )(bq, bk, bk, bk, bv, bv, bv, bk, bv, bias_t)


def _merge_kernel(*refs, nct, nt, L, S, with_router):
    if with_router:
        (ya_ref, yb_ref, cup_ref, cu_ref, cun_ref, sg_ref, x_ref, mod_ref, wpa_ref, wpb_ref, wpc_ref,
         pbd_ref, psc_ref, wo_ref, g2_ref, rwh_ref, rwl_ref, rb_ref,
         xo_ref, h_ref, cmb_ref, sel_ref, e_scr) = refs
    else:
        (ya_ref, yb_ref, cup_ref, cu_ref, cun_ref, sg_ref, x_ref, mod_ref, wpa_ref, wpb_ref, wpc_ref,
         pbd_ref, psc_ref, wo_ref, g2_ref, xo_ref, h_ref, e_scr) = refs
    i = pl.program_id(1)
    D = D_MODEL

    at_start = (i == 0) | (i == nct)
    at_end = (i == nct - 1) | (i == nt - 1)
    u = cu_ref[0]
    e_scr[0:POOL_HALO] = jnp.where(at_start, 0.0, cup_ref[0])
    e_scr[POOL_HALO:POOL_HALO + TM] = u
    e_scr[POOL_HALO + TM:] = jnp.where(at_end, 0.0, cun_ref[0])
    sh = lambda k: e_scr[POOL_HALO + k:POOL_HALO + k + TM]
    w2 = sh(-1) + sh(0)
    w4 = w2 + (sh(-2) + sh(1))
    w8 = w4 + ((sh(-4) + sh(-3)) + (sh(2) + sh(3)))
    w16 = w8 + (((sh(-8) + sh(-7)) + (sh(-6) + sh(-5))) + ((sh(4) + sh(5)) + (sh(6) + sh(7))))
    grp = lax.broadcasted_iota(jnp.int32, (1, C_WIDTH), 1) // C_GROUP_DIM
    half = jnp.left_shift(1, grp)
    t_loc = lax.broadcasted_iota(jnp.int32, (TM, 1), 0)
    t_seq = jnp.where(i < nct, i * TM, (i - nct) * TM) + t_loc
    n_seq = jnp.where(i < nct, L, S)
    cnt = jnp.minimum(t_seq + half, n_seq) - jnp.maximum(t_seq - half, 0)
    wsum = jnp.where(grp == 0, w2, jnp.where(grp == 1, w4, jnp.where(grp == 2, w8, w16)))
    dlt = wsum / cnt.astype(F32) - u
    yc = _dot(dlt.astype(BF16), pbd_ref[...]) * psc_ref[...]

    mrg = (sg_ref[0, :, 0:D] * _dot(ya_ref[0], wpa_ref[...])
           + sg_ref[0, :, D:2 * D] * _dot(yb_ref[0], wpb_ref[...])
           + sg_ref[0, :, 2 * D:3 * D] * _dot(yc.astype(BF16), wpc_ref[...]))
    m = mod_ref[0]
    x = x_ref[0] + m[2:3] * _dot(mrg.astype(BF16), wo_ref[...])
    xo_ref[0] = x
    y = x * lax.rsqrt(jnp.mean(x * x, axis=-1, keepdims=True) + EPS) * g2_ref[...]
    h = y * (1.0 + m[4:5]) + m[3:4]
    h_ref[0] = h.astype(BF16)

    if with_router:
        lane = lax.broadcasted_iota(jnp.int32, (1, LANES), 1).astype(F32)
        lg = _dot3(h, rwh_ref[...], rwl_ref[...]) + rb_ref[...]
        lg = jnp.where(lane < N_EXPERTS, lg, -jnp.inf)
        m1 = jnp.max(lg, axis=-1, keepdims=True)
        i1 = jnp.min(jnp.where(lg == m1, lane, float(LANES)), axis=-1, keepdims=True)
        mask1 = lane == i1
        lg2 = jnp.where(mask1, -jnp.inf, lg)
        m2 = jnp.max(lg2, axis=-1, keepdims=True)
        i2 = jnp.min(jnp.where(lg2 == m2, lane, float(LANES)), axis=-1, keepdims=True)
        mask2 = lane == i2
        e2 = jnp.exp(m2 - m1)
        den = 1.0 + e2
        cmb_ref[0] = jnp.where(mask1, 1.0 / den, 0.0) + jnp.where(mask2, e2 / den, 0.0)
        sel_ref[0] = jnp.where(mask1 | mask2, 1.0, 0.0)


def _merge(ya, yb, cu, sg, xs, modr, wpa, wpb, wpc, pbd, psc, wo, g2, router, nct, L):
    B, T, D = xs.shape
    nt = T // TM
    S = T - L
    hb = TM // POOL_HALO
    tok = lambda w: pl.BlockSpec((1, TM, w), lambda b, i: (b, i, 0))
    in_specs = [tok(A_Q), tok(B_W),
                pl.BlockSpec((1, POOL_HALO, C_WIDTH), lambda b, i: (b, jnp.maximum(i * hb - 1, 0), 0)),
                tok(C_WIDTH),
                pl.BlockSpec((1, POOL_HALO, C_WIDTH), lambda b, i: (b, jnp.minimum((i + 1) * hb, nt * hb - 1), 0)),
                tok(3 * D), tok(D),
                pl.BlockSpec((1, 6, D), lambda b, i: (jnp.where(i < nct, B, b), 0, 0)),
                _const_spec((A_Q, D)), _const_spec((B_W, D)), _const_spec((C_WIDTH, D)),
                _const_spec((C_WIDTH, C_WIDTH)), _const_spec((1, C_WIDTH)), _const_spec((D, D)),
                _const_spec((1, D))]
    args = [ya, yb, cu, cu, cu, sg, xs, modr, wpa, wpb, wpc, pbd, psc, wo, g2]
    out_specs = [tok(D), tok(D)]
    out_shape = [jax.ShapeDtypeStruct((B, T, D), F32), jax.ShapeDtypeStruct((B, T, D), BF16)]
    if router is not None:
        in_specs += [_const_spec((D, LANES)), _const_spec((D, LANES)), _const_spec((1, LANES))]
        args += list(router)
        out_specs += [tok(LANES), tok(LANES)]
        out_shape += [jax.ShapeDtypeStruct((B, T, LANES), F32)] * 2
    return pl.pallas_call(
        functools.partial(_merge_kernel, nct=nct, nt=nt, L=L, S=S, with_router=router is not None),
        grid=(B, nt),
        in_specs=in_specs,
        out_specs=tuple(out_specs),
        out_shape=tuple(out_shape),
        scratch_shapes=[pltpu.VMEM((TM + 2 * POOL_HALO, C_WIDTH), F32)],
        compiler_params=_cparams(("arbitrary", "arbitrary")),
        name="merge_router" if router is not None else "merge",
    )(*args)


def _swiglu_acc(h, w1_ref, w3_ref, w2_ref, lead):
    d_ff = w1_ref.shape[-1]
    acc = None
    for f0 in range(0, d_ff, FC):
        f1 = min(f0 + FC, d_ff)
        a = _dot(h, w1_ref[lead + (slice(None), slice(f0, f1))])
        b = _dot(h, w3_ref[lead + (slice(None), slice(f0, f1))])
        t = (a * jax.nn.sigmoid(a) * b).astype(BF16)
        part = _dot(t, w2_ref[lead + (slice(f0, f1), slice(None))])
        acc = part if acc is None else acc + part
    return acc


def _ffn_kernel(h_ref, x_ref, g_ref, w1_ref, w3_ref, w2_ref, o_ref):
    f = _swiglu_acc(h_ref[0], w1_ref, w3_ref, w2_ref, ())
    o_ref[0] = x_ref[0] + g_ref[0] * f


def _ffn_dense(h2, xs, gate2, w1, w3, w2, nct):
    B, T, D = xs.shape
    F = w1.shape[1]
    nt = T // TM
    tok = lambda: pl.BlockSpec((1, TM, D), lambda b, i: (b, i, 0))
    return pl.pallas_call(
        _ffn_kernel,
        grid=(B, nt),
        in_specs=[tok(), tok(),
                  pl.BlockSpec((1, 1, D), lambda b, i: (jnp.where(i < nct, B, b), 0, 0)),
                  _const_spec((D, F)), _const_spec((D, F)), _const_spec((F, D))],
        out_specs=tok(),
        out_shape=jax.ShapeDtypeStruct((B, T, D), F32),
        compiler_params=_cparams(("arbitrary", "arbitrary")),
        name="ffn_dense",
    )(h2, xs, gate2, w1, w3, w2)


def _gather_kernel(wj_ref, ws_ref, wf_ref, wl_ref, wv_ref, src_ref, h_ref, o_ref, acc_ref):
    w = pl.program_id(0)

    @pl.when(wf_ref[w] == 1)
    def _():
        acc_ref[...] = jnp.zeros_like(acc_ref)

    @pl.when(wv_ref[w] == 1)
    def _():
        col = lax.broadcasted_iota(jnp.int32, (1, TM), 1) + ws_ref[w] * TM
        onehot = jnp.where(src_ref[...] == col, 1.0, 0.0).astype(BF16)
        acc_ref[...] += _dot(onehot, h_ref[...])

    @pl.when(wl_ref[w] == 1)
    def _():
        o_ref[...] = acc_ref[...].astype(BF16)


def _moe_gather(work, src, h2f, n_tiles):
    wj, ws, wf, wl, wv = work
    n, D = h2f.shape
    return pl.pallas_call(
        _gather_kernel,
        grid_spec=pltpu.PrefetchScalarGridSpec(
            num_scalar_prefetch=5,
            grid=(wj.shape[0],),
            in_specs=[pl.BlockSpec((TG, 1), lambda w, wj, ws, wf, wl, wv: (wj[w], 0)),
                      pl.BlockSpec((TM, D), lambda w, wj, ws, wf, wl, wv: (ws[w], 0))],
            out_specs=pl.BlockSpec((TG, D), lambda w, wj, ws, wf, wl, wv: (wj[w], 0)),
            scratch_shapes=[pltpu.VMEM((TG, D), F32)]),
        out_shape=jax.ShapeDtypeStruct((n_tiles * TG, D), BF16),
        compiler_params=_cparams(("arbitrary",)),
        name="moe_gather",
    )(wj, ws, wf, wl, wv, src, h2f)


def _gffn_kernel(te_ref, tv_ref, x_ref, pw_ref, w1_ref, w3_ref, w2_ref, o_ref):
    j = pl.program_id(0)

    @pl.when(tv_ref[j] == 1)
    def _():
        f = _swiglu_acc(x_ref[...], w1_ref, w3_ref, w2_ref, (0,))
        o_ref[...] = (f * pw_ref[...]).astype(BF16)

    @pl.when(tv_ref[j] == 0)
    def _():
        o_ref[...] = jnp.zeros_like(o_ref)


def _moe_gffn(tile_expert, tile_valid, xg, pw, w1, w3, w2):
    P, D = xg.shape
    F = w1.shape[-1]
    wspec = lambda shape: pl.BlockSpec(shape, lambda j, te, tv: (te[j], 0, 0), pipeline_mode=pl.Buffered(1))
    return pl.pallas_call(
        _gffn_kernel,
        grid_spec=pltpu.PrefetchScalarGridSpec(
            num_scalar_prefetch=2,
            grid=(P // TG,),
            in_specs=[pl.BlockSpec((TG, D), lambda j, te, tv: (j, 0)),
                      pl.BlockSpec((TG, 1), lambda j, te, tv: (j, 0)),
                      wspec((1, D, F)), wspec((1, D, F)), wspec((1, F, D))],
            out_specs=pl.BlockSpec((TG, D), lambda j, te, tv: (j, 0))),
        out_shape=jax.ShapeDtypeStruct((P, D), BF16),
        compiler_params=_cparams(("arbitrary",)),
        name="moe_gffn",
    )(tile_expert, tile_valid, xg, pw, w1, w3, w2)


def _combine_kernel(wt_ref, ws_ref, wf_ref, wl_ref, wv_ref, pos_ref, y_ref, x_ref, g_ref, o_ref, acc_ref):
    w = pl.program_id(0)

    @pl.when(wf_ref[w] == 1)
    def _():
        acc_ref[...] = jnp.zeros_like(acc_ref)

    @pl.when(wv_ref[w] == 1)
    def _():
        col = lax.broadcasted_iota(jnp.int32, (1, TG), 1) + ws_ref[w] * TG
        pos = pos_ref[...]
        hit = (pos[:, 0:1] == col) | (pos[:, 1:2] == col)
        acc_ref[...] += _dot(jnp.where(hit, 1.0, 0.0).astype(BF16), y_ref[...])

    @pl.when(wl_ref[w] == 1)
    def _():
        o_ref[...] = x_ref[...] + g_ref[0] * acc_ref[...]


def _moe_combine(work, pos2, yw, xf, gate2, tiles_per_batch, nct, B):
    wt, ws, wf, wl, wv = work
    n, D = xf.shape

    def gidx(w, wt, ws, wf, wl, wv):
        b = wt[w] // tiles_per_batch
        return (jnp.where(wt[w] % tiles_per_batch < nct, B, b), 0, 0)

    return pl.pallas_call(
        _combine_kernel,
        grid_spec=pltpu.PrefetchScalarGridSpec(
            num_scalar_prefetch=5,
            grid=(wt.shape[0],),
            in_specs=[pl.BlockSpec((TM, 2), lambda w, wt, ws, wf, wl, wv: (wt[w], 0)),
                      pl.BlockSpec((TG, D), lambda w, wt, ws, wf, wl, wv: (ws[w], 0)),
                      pl.BlockSpec((TM, D), lambda w, wt, ws, wf, wl, wv: (wt[w], 0)),
                      pl.BlockSpec((1, 1, D), gidx)],
            out_specs=pl.BlockSpec((TM, D), lambda w, wt, ws, wf, wl, wv: (wt[w], 0)),
            scratch_shapes=[pltpu.VMEM((TM, D), F32)]),
        out_shape=jax.ShapeDtypeStruct((n, D), F32),
        compiler_params=_cparams(("arbitrary",)),
        name="moe_combine",
    )(wt, ws, wf, wl, wv, pos2, yw, xf, gate2)


def _work_list(group, item, valid, n_work):
    valid = valid.astype(jnp.int32)
    cs = jnp.cumsum(valid)
    total = cs[-1]
    w = jnp.minimum(jnp.arange(n_work, dtype=jnp.int32), total - 1)
    idx = jnp.searchsorted(cs, w + 1, side="left").astype(jnp.int32)
    g = group[idx]
    it = item[idx]
    live = jnp.arange(n_work, dtype=jnp.int32) < total
    g_prev = jnp.concatenate([jnp.full((1,), -1, jnp.int32), g[:-1]])
    g_next = jnp.concatenate([g[1:], jnp.full((1,), -1, jnp.int32)])
    live_next = jnp.concatenate([live[1:], jnp.zeros((1,), bool)])
    first = live & (g != g_prev)
    last = live & ((g != g_next) | ~live_next)
    i32 = lambda a: a.astype(jnp.int32)
    return g, it, i32(first), i32(last), i32(live)


def _moe_plan(sel, n_tok):
    E = N_EXPERTS
    sel = sel.astype(jnp.int32)
    cnt = jnp.sum(sel, axis=0)
    rank = jnp.cumsum(sel, axis=0) - 1
    gsz = ((cnt + TG - 1) // TG) * TG
    gend = jnp.cumsum(gsz)
    goff = gend - gsz
    n_tiles = (2 * n_tok + E * (TG - 1) + TG - 1) // TG
    P = n_tiles * TG
    pos = jnp.where(sel == 1, goff[None, :] + rank, -1)
    tok = jnp.broadcast_to(jnp.arange(n_tok, dtype=jnp.int32)[:, None], (n_tok, E))
    src = jnp.full((P,), -1, jnp.int32).at[jnp.where(sel == 1, pos, P).reshape(-1)].set(
        tok.reshape(-1), mode="drop")
    tile_start = jnp.arange(n_tiles, dtype=jnp.int32) * TG
    tile_valid = (tile_start < gend[-1]).astype(jnp.int32)
    tile_expert = jnp.minimum(jnp.searchsorted(gend, tile_start, side="right"), E - 1).astype(jnp.int32)
    pos2 = jnp.sort(pos, axis=1)[:, E - 2:]

    rows_in_tile = jnp.clip(goff[tile_expert] + cnt[tile_expert] - tile_start, 0, TG)
    first_src = src[tile_start]
    last_src = src[jnp.maximum(tile_start + rows_in_tile - 1, 0)]
    lo = first_src // TM
    span = jnp.where(tile_valid == 1, last_src // TM - lo + 1, 0)
    n_src_tiles = n_tok // TM
    n_gw = E * n_src_tiles + n_tiles
    cs = jnp.cumsum(span)
    total = cs[-1]
    w = jnp.minimum(jnp.arange(n_gw, dtype=jnp.int32), total - 1)
    jw = jnp.searchsorted(cs, w + 1, side="left").astype(jnp.int32)
    sw = lo[jw] + (w - (cs[jw] - span[jw]))
    live = jnp.arange(n_gw, dtype=jnp.int32) < total
    j_prev = jnp.concatenate([jnp.full((1,), -1, jnp.int32), jw[:-1]])
    j_next = jnp.concatenate([jw[1:], jnp.full((1,), -1, jnp.int32)])
    live_next = jnp.concatenate([live[1:], jnp.zeros((1,), bool)])
    i32 = lambda a: a.astype(jnp.int32)
    gwork = (jw, i32(sw), i32(live & (jw != j_prev)), i32(live & ((jw != j_next) | ~live_next)), i32(live))

    pt = pos.reshape(n_src_tiles, TM, E)
    pmax = jnp.max(pt, axis=1)
    pmin = jnp.min(jnp.where(pt >= 0, pt, P), axis=1)
    ta = pmin // TG
    tb = pmax // TG
    cand_item = jnp.stack([ta, tb], axis=-1).reshape(-1)
    cand_valid = jnp.stack([pmax >= 0, (pmax >= 0) & (tb != ta)], axis=-1).reshape(-1)
    cand_group = jnp.repeat(jnp.arange(n_src_tiles, dtype=jnp.int32), 2 * E)
    n_cw = E * n_src_tiles + n_tiles
    cwork = _work_list(cand_group, i32(cand_item), cand_valid, n_cw)
    return src, pos, pos2, tile_expert, tile_valid, gwork, cwork, n_tiles


def _moe(h2, xs1, cmb, sel, gate2, w1, w3, w2, nct):
    B, T, D = xs1.shape
    n_tok = B * T
    selm = sel.reshape(n_tok, LANES)[:, :N_EXPERTS] > 0.5
    cmbm = cmb.reshape(n_tok, LANES)[:, :N_EXPERTS]
    src, pos, pos2, tile_expert, tile_valid, gwork, cwork, n_tiles = _moe_plan(selm, n_tok)
    P = n_tiles * TG
    pw = jnp.zeros((P,), F32).at[jnp.where(selm, pos, P).reshape(-1)].set(cmbm.reshape(-1), mode="drop")

    xg = _moe_gather(gwork, src.reshape(P, 1), h2.reshape(n_tok, D), n_tiles)
    yw = _moe_gffn(tile_expert, tile_valid, xg, pw.reshape(P, 1), w1, w3, w2)
    out = _moe_combine(cwork, pos2, yw, xs1.reshape(n_tok, D), gate2, T // TM, nct, B)
    return out.reshape(B, T, D)


def _final_kernel(x_ref, g_ref, o_ref):
    x = x_ref[0]
    o_ref[0] = x * lax.rsqrt(jnp.mean(x * x, axis=-1, keepdims=True) + EPS) * g_ref[...]


def _final_norm(xs, g, nct, S):
    B, T, D = xs.shape
    return pl.pallas_call(
        _final_kernel,
        grid=(B, S // TM),
        in_specs=[pl.BlockSpec((1, TM, D), lambda b, i: (b, i + nct, 0)), _const_spec((1, D))],
        out_specs=pl.BlockSpec((1, TM, D), lambda b, i: (b, i, 0)),
        out_shape=jax.ShapeDtypeStruct((B, S, D), F32),
        compiler_params=_cparams(("arbitrary", "arbitrary")),
        name="final_norm",
    )(xs, g)


def _rope_tables(L, S):
    t = np.arange(S)
    pos = np.stack([t // GRID_W, t % GRID_W], axis=-1).astype(np.float32)
    inv_freq = (ROPE_THETA ** (-np.arange(0, AXIS_ROPE_DIM, 2, dtype=np.float32) / AXIS_ROPE_DIM)).astype(np.float32)
    ang = pos[:, :, None] * inv_freq[None, None, :]
    cos, sin = np.cos(ang), np.sin(ang)
    cos64 = np.concatenate([cos[:, 0], cos[:, 0], cos[:, 1], cos[:, 1]], axis=-1)
    sin64 = np.concatenate([-sin[:, 0], sin[:, 0], -sin[:, 1], sin[:, 1]], axis=-1)
    cos_t = np.concatenate([np.ones((L, HEAD_DIM), np.float32), cos64], axis=0)
    sin_t = np.concatenate([np.zeros((L, HEAD_DIM), np.float32), sin64], axis=0)
    rep = LANES // HEAD_DIM
    return jnp.asarray(np.tile(cos_t, (1, rep)), F32), jnp.asarray(np.tile(sin_t, (1, rep)), F32)


def _na_bias_table(rel_bias):
    col = np.arange(GRID_W)
    col_start = np.clip(col - NA_WIN_COLS // 2, 0, GRID_W - NA_WIN_COLS)
    kc = np.arange(GRID_W)
    inside = (kc[None, :] >= col_start[:, None]) & (kc[None, :] < col_start[:, None] + NA_WIN_COLS)
    dc = np.clip(kc[None, :] - col[:, None] + (NA_WIN_COLS - 1), 0, 2 * NA_WIN_COLS - 2)
    case = np.arange(NA_WIN_ROWS)
    dr = np.arange(NA_WIN_ROWS)[None, :] - case[:, None] + (NA_WIN_ROWS - 1)
    t = rel_bias[:, dr]
    t = t[:, :, :, dc]
    t = jnp.where(jnp.asarray(inside)[None, None, None], t, MASK_VALUE)
    t = t.transpose(1, 0, 3, 2, 4)
    return t.reshape(NA_WIN_ROWS, B_HEADS, GRID_W, NA_WIN_ROWS * GRID_W).astype(F32)


def _block_diag_ones(n, blk):
    idx = np.arange(n) // blk
    return jnp.asarray((idx[:, None] == idx[None, :]).astype(np.float32), BF16)


def _pool_block_diag(pool_w):
    g, c, d = pool_w.shape
    out = jnp.zeros((g * c, g * d), pool_w.dtype)
    for k in range(g):
        out = out.at[k * c:(k + 1) * c, k * d:(k + 1) * d].set(pool_w[k])
    return out.astype(BF16)


def kernel(x, c, ctx, c_ctx, w_mod, b_mod, norm1_g, norm2_g, w_in, q_norm_g, k_norm_g, na_rel_bias, pool_w,
           pool_scale, w_branch_a, w_branch_b, w_branch_c, w_out, ffn_w1, ffn_w3, ffn_w2, router_w, router_b,
           moe_w1, moe_w3, moe_w2, final_g):
    B, S, D = x.shape
    L = ctx.shape[1]
    T = L + S
    depth = w_mod.shape[0]
    assert D == D_MODEL and L % TM == 0 and S % TM == 0 and TM % GRID_W == 0 and B + 1 <= SUBLANES
    assert S // GRID_W >= NA_WIN_ROWS and TM // GRID_W <= NA_WIN_ROWS // 2 + 1 and TM <= TG
    nct = L // TM

    cvec = jnp.zeros((SUBLANES, D), F32).at[:B].set(c).at[B].set(c_ctx)
    mod = _mod_vectors(cvec, w_mod, b_mod)
    cos_t, sin_t = _rope_tables(L, S)
    bd = _block_diag_ones(QK_W, HEAD_DIM)

    xs = jnp.concatenate([ctx, x], axis=1)
    for layer in range(depth):
        modr = mod[layer].reshape(SUBLANES, 6, D)
        gqk = jnp.concatenate([jnp.tile(q_norm_g[layer], A_HEADS), jnp.tile(k_norm_g[layer], A_KV_HEADS)])
        q, kt, v, bq, bk, bv, cu, sg = _inproj(
            xs, modr, norm1_g[layer].reshape(1, D), w_in[layer].astype(BF16), bd, gqk.reshape(1, QK_W),
            cos_t, sin_t, nct)
        ya = _attn_a(q, kt, v, nct)
        yb = _attn_b(bq, bk, bv, _na_bias_table(na_rel_bias[layer]), nct, L)
        is_moe = layer % 2 == 1
        i = layer // 2
        router = None
        if is_moe:
            rw = jnp.zeros((D, LANES), F32).at[:, :N_EXPERTS].set(router_w[i])
            rw_hi = rw.astype(BF16)
            rw_lo = (rw - rw_hi.astype(F32)).astype(BF16)
            rb = jnp.zeros((1, LANES), F32).at[0, :N_EXPERTS].set(router_b[i])
            router = (rw_hi, rw_lo, rb)
        outs = _merge(ya, yb, cu, sg, xs, modr, w_branch_a[layer].astype(BF16), w_branch_b[layer].astype(BF16),
                      w_branch_c[layer].astype(BF16), _pool_block_diag(pool_w[layer]),
                      pool_scale[layer].reshape(1, C_WIDTH), w_out[layer].astype(BF16),
                      norm2_g[layer].reshape(1, D), router, nct, L)
        gate2 = modr[:, 5:6, :]
        if is_moe:
            xs1, h2, cmb, sel = outs
            xs = _moe(h2, xs1, cmb, sel, gate2, moe_w1[i].astype(BF16), moe_w3[i].astype(BF16),
                      moe_w2[i].astype(BF16), nct)
        else:
            xs1, h2 = outs
            xs = _ffn_dense(h2, xs1, gate2, ffn_w1[i].astype(BF16), ffn_w3[i].astype(BF16),
                            ffn_w2[i].astype(BF16), nct)
    return _final_norm(xs, final_g.reshape(1, D), nct, S)
```

```python
import functools

import numpy as np
import jax
import jax.numpy as jnp
from jax import lax
from jax.experimental import pallas as pl
from jax.experimental.pallas import tpu as pltpu

F32 = jnp.float32
BF16 = jnp.bfloat16

D_MODEL = 1024
DEPTH = 4
GRID_W = 64
HEAD_DIM = 64
A_HEADS = 8
A_KV_HEADS = 2
A_GROUP = A_HEADS // A_KV_HEADS
B_HEADS = 4
C_GROUPS = 4
C_GROUP_DIM = 64
C_WIDTH = C_GROUPS * C_GROUP_DIM
POOL_WINDOWS = (2, 4, 8, 16)
NA_WIN_ROWS = 8
NA_WIN_COLS = 16
ROPE_THETA = 10000.0
AXIS_ROPE_DIM = HEAD_DIM // 2
N_EXPERTS = 8
EPS = 1e-6

A_Q = A_HEADS * HEAD_DIM
A_KV = A_KV_HEADS * HEAD_DIM
B_W = B_HEADS * HEAD_DIM
QK_W = A_Q + A_KV
OFF_AV = QK_W
OFF_BQ = OFF_AV + A_KV
OFF_BK = OFF_BQ + B_W
OFF_BV = OFF_BK + B_W
OFF_CU = OFF_BV + B_W
OFF_G = OFF_CU + C_WIDTH
IN_COLS = OFF_G + 3 * D_MODEL

LANES = 128
SUBLANES = 8
TM = 256
TG = 256
FC = 512
A_KBLK = 4
LOG2E = 1.4426950408889634
POOL_HALO = 8
MASK_VALUE = -1e30
VMEM_LIMIT = 56 * 1024 * 1024


def _cparams(sem):
    return pltpu.CompilerParams(dimension_semantics=sem, vmem_limit_bytes=VMEM_LIMIT)


def _dot(a, b):
    return jnp.dot(a, b, preferred_element_type=F32)


def _dot_nt(a, b):
    return lax.dot_general(a, b, (((1,), (1,)), ((), ())), preferred_element_type=F32)


def _split(a):
    hi = a.astype(BF16)
    lo = (a - hi.astype(F32)).astype(BF16)
    return hi, lo


def _dot3(a, b_hi, b_lo):
    a_hi, a_lo = _split(a)
    return _dot(a_hi, b_hi) + (_dot(a_lo, b_hi) + _dot(a_hi, b_lo))


def _const_spec(shape):
    n = len(shape)
    return pl.BlockSpec(shape, lambda *_: (0,) * n, pipeline_mode=pl.Buffered(1))


def _mod_kernel(c_ref, w_ref, b_ref, o_ref):
    c = c_ref[...]
    s = c * jax.nn.sigmoid(c)
    w_hi, w_lo = _split(w_ref[0])
    o_ref[0] = _dot3(s, w_hi, w_lo) + b_ref[0]


def _mod_vectors(cvec, w_mod, b_mod):
    depth, d, n = w_mod.shape
    tn = 1536
    return pl.pallas_call(
        _mod_kernel,
        grid=(depth, n // tn),
        in_specs=[pl.BlockSpec((SUBLANES, d), lambda l, j: (0, 0)),
                  pl.BlockSpec((1, d, tn), lambda l, j: (l, 0, j)),
                  pl.BlockSpec((1, 1, tn), lambda l, j: (l, 0, j))],
        out_specs=pl.BlockSpec((1, SUBLANES, tn), lambda l, j: (l, 0, j)),
        out_shape=jax.ShapeDtypeStruct((depth, SUBLANES, n), F32),
        compiler_params=_cparams(("arbitrary", "arbitrary")),
        name="mod_vectors",
    )(cvec, w_mod, b_mod.reshape(depth, 1, n))


def _swap16(x):
    lane = lax.broadcasted_iota(jnp.int32, (1, LANES), 1)
    first = (lane % 32) < 16
    return jnp.where(first, pltpu.roll(x, LANES - 16, 1), pltpu.roll(x, 16, 1))


def _inproj_kernel(x_ref, mod_ref, g_ref, w_ref, bd_ref, gqk_ref, cos_ref, sin_ref,
                   q_ref, kt_ref, v_ref, bq_ref, bk_ref, bv_ref, cu_ref, sg_ref):
    x = x_ref[0]
    m = mod_ref[0]
    y = x * lax.rsqrt(jnp.mean(x * x, axis=-1, keepdims=True) + EPS) * g_ref[...]
    h = y * (1.0 + m[1:2]) + m[0:1]
    px = _dot(h.astype(BF16), w_ref[...])

    qk = px[:, :QK_W]
    sq_hi, sq_lo = _split(qk * qk)
    ss = _dot(sq_hi, bd_ref[...]) + _dot(sq_lo, bd_ref[...])
    qn = qk * lax.rsqrt(ss * (1.0 / HEAD_DIM) + EPS) * gqk_ref[...]
    cos = cos_ref[...]
    sin = sin_ref[...]
    chunks = []
    for j in range(QK_W // LANES):
        c = qn[:, j * LANES:(j + 1) * LANES]
        chunks.append(c * cos + _swap16(c) * sin)
    scale = HEAD_DIM ** -0.5
    for j in range(A_Q // LANES):
        q_ref[0, :, j * LANES:(j + 1) * LANES] = (chunks[j] * (scale * LOG2E)).astype(BF16)
    kt_ref[0, 0] = chunks[A_Q // LANES].T.astype(BF16)

    ones = jnp.ones((TM, LANES - HEAD_DIM), BF16)
    for j in range(A_KV_HEADS):
        vj = px[:, OFF_AV + j * HEAD_DIM:OFF_AV + (j + 1) * HEAD_DIM].astype(BF16)
        v_ref[0, :, j * LANES:(j + 1) * LANES] = jnp.concatenate([vj, ones], axis=1)
    bq_ref[0] = (px[:, OFF_BQ:OFF_BK] * scale).astype(BF16)
    bk_ref[0] = px[:, OFF_BK:OFF_BV].astype(BF16)
    bv_ref[0] = px[:, OFF_BV:OFF_CU].astype(BF16)
    cu_ref[0] = px[:, OFF_CU:OFF_G]
    sg_ref[0] = jax.nn.sigmoid(px[:, OFF_G:])


def _inproj(xs, modr, g1, w_in, bd, gqk, cos_t, sin_t, nct):
    B, T, D = xs.shape
    nt = T // TM
    tok = lambda w: pl.BlockSpec((1, TM, w), lambda b, i: (b, i, 0))
    out_shape = (
        jax.ShapeDtypeStruct((B, T, A_Q), BF16),
        jax.ShapeDtypeStruct((B, nt, A_KV, TM), BF16),
        jax.ShapeDtypeStruct((B, T, A_KV_HEADS * LANES), BF16),
        jax.ShapeDtypeStruct((B, T, B_W), BF16),
        jax.ShapeDtypeStruct((B, T, B_W), BF16),
        jax.ShapeDtypeStruct((B, T, B_W), BF16),
        jax.ShapeDtypeStruct((B, T, C_WIDTH), F32),
        jax.ShapeDtypeStruct((B, T, 3 * D), F32),
    )
    return pl.pallas_call(
        _inproj_kernel,
        grid=(B, nt),
        in_specs=[tok(D),
                  pl.BlockSpec((1, 6, D), lambda b, i: (jnp.where(i < nct, B, b), 0, 0)),
                  _const_spec((1, D)),
                  _const_spec((D, IN_COLS)),
                  _const_spec((QK_W, QK_W)),
                  _const_spec((1, QK_W)),
                  pl.BlockSpec((TM, LANES), lambda b, i: (i, 0)),
                  pl.BlockSpec((TM, LANES), lambda b, i: (i, 0))],
        out_specs=(tok(A_Q),
                   pl.BlockSpec((1, 1, A_KV, TM), lambda b, i: (b, i, 0, 0)),
                   tok(A_KV_HEADS * LANES), tok(B_W), tok(B_W), tok(B_W), tok(C_WIDTH), tok(3 * D)),
        out_shape=out_shape,
        compiler_params=_cparams(("arbitrary", "arbitrary")),
        name="inproj",
    )(xs, modr, g1, w_in, bd, gqk, cos_t, sin_t)


def _attn_a_kernel(q_ref, kt_ref, v_ref, o_ref, *, nct, n_steps):
    i = pl.program_id(1)
    rows = A_GROUP * TM

    def step(carry, q4, j, blk0, nblk):
        m, acc = carry
        s_list = [_dot(q4, kt_ref[0, blk0 + c, j * HEAD_DIM:(j + 1) * HEAD_DIM, :]) for c in range(nblk)]
        smax = s_list[0]
        for s in s_list[1:]:
            smax = jnp.maximum(smax, s)
        m_new = jnp.maximum(m, jnp.max(smax, axis=-1, keepdims=True))
        alpha = jnp.exp2(m - m_new)
        p = jnp.concatenate([jnp.exp2((s - m_new).astype(BF16)) for s in s_list], axis=1)
        vv = v_ref[0, pl.ds(pl.multiple_of(blk0 * TM, TM), nblk * TM), j * LANES:(j + 1) * LANES]
        return m_new, alpha * acc + _dot(p, vv)

    def run(n_main):
        for j in range(A_KV_HEADS):
            base = j * A_GROUP * HEAD_DIM
            q4 = jnp.concatenate(
                [q_ref[0, :, base + g * HEAD_DIM: base + (g + 1) * HEAD_DIM] for g in range(A_GROUP)], axis=0)
            carry = (jnp.full((rows, 1), MASK_VALUE, F32), jnp.zeros((rows, LANES), F32))
            carry = step(carry, q4, j, 0, nct)
            if n_main:
                carry = lax.fori_loop(
                    0, n_main, lambda k, cr: step(cr, q4, j, nct + k * A_KBLK, A_KBLK), carry)
            acc = carry[1]
            o = acc[:, 0:HEAD_DIM] / acc[:, HEAD_DIM:HEAD_DIM + 1]
            for g in range(A_GROUP):
                o_ref[0, :, base + g * HEAD_DIM: base + (g + 1) * HEAD_DIM] = o[g * TM:(g + 1) * TM].astype(BF16)

    @pl.when(i < nct)
    def _():
        run(0)

    @pl.when(i >= nct)
    def _():
        run(n_steps)


def _attn_a(q, kt, v, nct):
    B, T, _ = q.shape
    nt = T // TM
    assert (nt - nct) % A_KBLK == 0
    return pl.pallas_call(
        functools.partial(_attn_a_kernel, nct=nct, n_steps=(nt - nct) // A_KBLK),
        grid=(B, nt),
        in_specs=[pl.BlockSpec((1, TM, A_Q), lambda b, i: (b, i, 0)),
                  pl.BlockSpec((1, nt, A_KV, TM), lambda b, i: (b, 0, 0, 0)),
                  pl.BlockSpec((1, T, A_KV_HEADS * LANES), lambda b, i: (b, 0, 0))],
        out_specs=pl.BlockSpec((1, TM, A_Q), lambda b, i: (b, i, 0)),
        out_shape=jax.ShapeDtypeStruct((B, T, A_Q), BF16),
        compiler_params=_cparams(("arbitrary", "arbitrary")),
        name="attn_a",
    )(q, kt, v)


def _attn_b_kernel(q_ref, kp_ref, kc_ref, kn_ref, vp_ref, vc_ref, vn_ref, kx_ref, vx_ref, bias_ref,
                   o_ref, kbuf, vbuf, *, nct, grid_rows):
    i = pl.program_id(1)
    rpt = TM // GRID_W
    win = NA_WIN_ROWS * GRID_W

    @pl.when(i < nct)
    def _():
        for h in range(B_HEADS):
            sl = slice(h * HEAD_DIM, (h + 1) * HEAD_DIM)
            s = _dot_nt(q_ref[0, :, sl], kx_ref[0, :, sl])
            m = jnp.max(s, axis=-1, keepdims=True)
            p = jnp.exp(s - m)
            l = jnp.sum(p, axis=-1, keepdims=True)
            o_ref[0, :, sl] = (_dot(p.astype(BF16), vx_ref[0, :, sl]) / l).astype(BF16)

    @pl.when(i >= nct)
    def _():
        kbuf[0:TM] = kp_ref[0]
        kbuf[TM:2 * TM] = kc_ref[0]
        kbuf[2 * TM:3 * TM] = kn_ref[0]
        vbuf[0:TM] = vp_ref[0]
        vbuf[TM:2 * TM] = vc_ref[0]
        vbuf[2 * TM:3 * TM] = vn_ref[0]
        r0 = (i - nct) * rpt

        def row_body(a, carry):
            qr = r0 + a
            rs = jnp.clip(qr - NA_WIN_ROWS // 2, 0, grid_rows - NA_WIN_ROWS)
            case = qr - rs
            start = pl.multiple_of((rs - r0 + rpt) * GRID_W, GRID_W)
            qs = pl.multiple_of(a * GRID_W, GRID_W)
            kw = kbuf[pl.ds(start, win), :]
            vw = vbuf[pl.ds(start, win), :]
            qrow = q_ref[0, pl.ds(qs, GRID_W), :]
            for h in range(B_HEADS):
                sl = slice(h * HEAD_DIM, (h + 1) * HEAD_DIM)
                s_w = _dot_nt(qrow[:, sl], kw[:, sl]) + bias_ref[case, h]
                s_c = _dot_nt(qrow[:, sl], kx_ref[0, :, sl])
                m = jnp.maximum(jnp.max(s_w, axis=-1, keepdims=True), jnp.max(s_c, axis=-1, keepdims=True))
                p_w = jnp.exp(s_w - m)
                p_c = jnp.exp(s_c - m)
                l = jnp.sum(p_w, axis=-1, keepdims=True) + jnp.sum(p_c, axis=-1, keepdims=True)
                o = _dot(p_w.astype(BF16), vw[:, sl]) + _dot(p_c.astype(BF16), vx_ref[0, :, sl])
                o_ref[0, pl.ds(qs, GRID_W), sl] = (o / l).astype(BF16)
            return carry

        lax.fori_loop(0, rpt, row_body, 0)


def _attn_b(bq, bk, bv, bias_t, nct, L):
    B, T, _ = bq.shape
    nt = T // TM
    grid_rows = (T - L) // GRID_W
    cur = lambda b, i: (b, i, 0)
    prev = lambda b, i: (b, jnp.maximum(i - 1, nct), 0)
    nxt = lambda b, i: (b, jnp.minimum(i + 1, nt - 1), 0)
    blk = lambda f: pl.BlockSpec((1, TM, B_W), f)
    ctx = pl.BlockSpec((1, L, B_W), lambda b, i: (b, 0, 0))
    return pl.pallas_call(
        functools.partial(_attn_b_kernel, nct=nct, grid_rows=grid_rows),
        grid=(B, nt),
        in_specs=[blk(cur), blk(prev), blk(cur), blk(nxt), blk(prev), blk(cur), blk(nxt), ctx, ctx,
                  _const_spec(bias_t.shape)],
        out_specs=blk(cur),
        out_shape=jax.ShapeDtypeStruct((B, T, B_W), BF16),
        scratch_shapes=[pltpu.VMEM((3 * TM, B_W), BF16), pltpu.VMEM((3 * TM, B_W), BF16)],
        compiler_params=_cparams(("arbitrary", "arbitrary")),
        name="attn_b",
    )(bq, bk, bk, bk, bv, bv, bv, bk, bv, bias_t)


def _merge_kernel(*refs, nct, nt, L, S, with_router):
    if with_router:
        (ya_ref, yb_ref, cup_ref, cu_ref, cun_ref, sg_ref, x_ref, mod_ref, wpa_ref, wpb_ref, wpc_ref,
         pbd_ref, psc_ref, wo_ref, g2_ref, rwh_ref, rwl_ref, rb_ref,
         xo_ref, h_ref, cmb_ref, sel_ref, e_scr) = refs
    else:
        (ya_ref, yb_ref, cup_ref, cu_ref, cun_ref, sg_ref, x_ref, mod_ref, wpa_ref, wpb_ref, wpc_ref,
         pbd_ref, psc_ref, wo_ref, g2_ref, xo_ref, h_ref, e_scr) = refs
    i = pl.program_id(1)
    D = D_MODEL

    at_start = (i == 0) | (i == nct)
    at_end = (i == nct - 1) | (i == nt - 1)
    u = cu_ref[0]
    e_scr[0:POOL_HALO] = jnp.where(at_start, 0.0, cup_ref[0])
    e_scr[POOL_HALO:POOL_HALO + TM] = u
    e_scr[POOL_HALO + TM:] = jnp.where(at_end, 0.0, cun_ref[0])
    sh = lambda k: e_scr[POOL_HALO + k:POOL_HALO + k + TM]
    w2 = sh(-1) + sh(0)
    w4 = w2 + (sh(-2) + sh(1))
    w8 = w4 + ((sh(-4) + sh(-3)) + (sh(2) + sh(3)))
    w16 = w8 + (((sh(-8) + sh(-7)) + (sh(-6) + sh(-5))) + ((sh(4) + sh(5)) + (sh(6) + sh(7))))
    grp = lax.broadcasted_iota(jnp.int32, (1, C_WIDTH), 1) // C_GROUP_DIM
    half = jnp.left_shift(1, grp)
    t_loc = lax.broadcasted_iota(jnp.int32, (TM, 1), 0)
    t_seq = jnp.where(i < nct, i * TM, (i - nct) * TM) + t_loc
    n_seq = jnp.where(i < nct, L, S)
    cnt = jnp.minimum(t_seq + half, n_seq) - jnp.maximum(t_seq - half, 0)
    wsum = jnp.where(grp == 0, w2, jnp.where(grp == 1, w4, jnp.where(grp == 2, w8, w16)))
    dlt = wsum / cnt.astype(F32) - u
    yc = _dot(dlt.astype(BF16), pbd_ref[...]) * psc_ref[...]

    mrg = (sg_ref[0, :, 0:D] * _dot(ya_ref[0], wpa_ref[...])
           + sg_ref[0, :, D:2 * D] * _dot(yb_ref[0], wpb_ref[...])
           + sg_ref[0, :, 2 * D:3 * D] * _dot(yc.astype(BF16), wpc_ref[...]))
    m = mod_ref[0]
    x = x_ref[0] + m[2:3] * _dot(mrg.astype(BF16), wo_ref[...])
    xo_ref[0] = x
    y = x * lax.rsqrt(jnp.mean(x * x, axis=-1, keepdims=True) + EPS) * g2_ref[...]
    h = y * (1.0 + m[4:5]) + m[3:4]
    h_ref[0] = h.astype(BF16)

    if with_router:
        lane = lax.broadcasted_iota(jnp.int32, (1, LANES), 1).astype(F32)
        lg = _dot3(h, rwh_ref[...], rwl_ref[...]) + rb_ref[...]
        lg = jnp.where(lane < N_EXPERTS, lg, -jnp.inf)
        m1 = jnp.max(lg, axis=-1, keepdims=True)
        i1 = jnp.min(jnp.where(lg == m1, lane, float(LANES)), axis=-1, keepdims=True)
        mask1 = lane == i1
        lg2 = jnp.where(mask1, -jnp.inf, lg)
        m2 = jnp.max(lg2, axis=-1, keepdims=True)
        i2 = jnp.min(jnp.where(lg2 == m2, lane, float(LANES)), axis=-1, keepdims=True)
        mask2 = lane == i2
        e2 = jnp.exp(m2 - m1)
        den = 1.0 + e2
        cmb_ref[0] = jnp.where(mask1, 1.0 / den, 0.0) + jnp.where(mask2, e2 / den, 0.0)
        sel_ref[0] = jnp.where(mask1 | mask2, 1.0, 0.0)


def _merge(ya, yb, cu, sg, xs, modr, wpa, wpb, wpc, pbd, psc, wo, g2, router, nct, L):
    B, T, D = xs.shape
    nt = T // TM
    S = T - L
    hb = TM // POOL_HALO
    tok = lambda w: pl.BlockSpec((1, TM, w), lambda b, i: (b, i, 0))
    in_specs = [tok(A_Q), tok(B_W),
                pl.BlockSpec((1, POOL_HALO, C_WIDTH), lambda b, i: (b, jnp.maximum(i * hb - 1, 0), 0)),
                tok(C_WIDTH),
                pl.BlockSpec((1, POOL_HALO, C_WIDTH), lambda b, i: (b, jnp.minimum((i + 1) * hb, nt * hb - 1), 0)),
                tok(3 * D), tok(D),
                pl.BlockSpec((1, 6, D), lambda b, i: (jnp.where(i < nct, B, b), 0, 0)),
                _const_spec((A_Q, D)), _const_spec((B_W, D)), _const_spec((C_WIDTH, D)),
                _const_spec((C_WIDTH, C_WIDTH)), _const_spec((1, C_WIDTH)), _const_spec((D, D)),
                _const_spec((1, D))]
    args = [ya, yb, cu, cu, cu, sg, xs, modr, wpa, wpb, wpc, pbd, psc, wo, g2]
    out_specs = [tok(D), tok(D)]
    out_shape = [jax.ShapeDtypeStruct((B, T, D), F32), jax.ShapeDtypeStruct((B, T, D), BF16)]
    if router is not None:
        in_specs += [_const_spec((D, LANES)), _const_spec((D, LANES)), _const_spec((1, LANES))]
        args += list(router)
        out_specs += [tok(LANES), tok(LANES)]
        out_shape += [jax.ShapeDtypeStruct((B, T, LANES), F32)] * 2
    return pl.pallas_call(
        functools.partial(_merge_kernel, nct=nct, nt=nt, L=L, S=S, with_router=router is not None),
        grid=(B, nt),
        in_specs=in_specs,
        out_specs=tuple(out_specs),
        out_shape=tuple(out_shape),
        scratch_shapes=[pltpu.VMEM((TM + 2 * POOL_HALO, C_WIDTH), F32)],
        compiler_params=_cparams(("arbitrary", "arbitrary")),
        name="merge_router" if router is not None else "merge",
    )(*args)


def _swiglu_acc(h, w1_ref, w3_ref, w2_ref, lead):
    d_ff = w1_ref.shape[-1]
    acc = None
    for f0 in range(0, d_ff, FC):
        f1 = min(f0 + FC, d_ff)
        a = _dot(h, w1_ref[lead + (slice(None), slice(f0, f1))])
        b = _dot(h, w3_ref[lead + (slice(None), slice(f0, f1))])
        t = (a * jax.nn.sigmoid(a) * b).astype(BF16)
        part = _dot(t, w2_ref[lead + (slice(f0, f1), slice(None))])
        acc = part if acc is None else acc + part
    return acc


def _ffn_kernel(h_ref, x_ref, g_ref, w1_ref, w3_ref, w2_ref, o_ref):
    f = _swiglu_acc(h_ref[0], w1_ref, w3_ref, w2_ref, ())
    o_ref[0] = x_ref[0] + g_ref[0] * f


def _ffn_dense(h2, xs, gate2, w1, w3, w2, nct):
    B, T, D = xs.shape
    F = w1.shape[1]
    nt = T // TM
    tok = lambda: pl.BlockSpec((1, TM, D), lambda b, i: (b, i, 0))
    return pl.pallas_call(
        _ffn_kernel,
        grid=(B, nt),
        in_specs=[tok(), tok(),
                  pl.BlockSpec((1, 1, D), lambda b, i: (jnp.where(i < nct, B, b), 0, 0)),
                  _const_spec((D, F)), _const_spec((D, F)), _const_spec((F, D))],
        out_specs=tok(),
        out_shape=jax.ShapeDtypeStruct((B, T, D), F32),
        compiler_params=_cparams(("arbitrary", "arbitrary")),
        name="ffn_dense",
    )(h2, xs, gate2, w1, w3, w2)


def _gather_kernel(wj_ref, ws_ref, wf_ref, wl_ref, wv_ref, te_ref, pos_ref, cmb_ref, h_ref,
                   o_ref, pw_ref, acc_ref, accw_ref):
    w = pl.program_id(0)

    @pl.when(wf_ref[w] == 1)
    def _():
        acc_ref[...] = jnp.zeros_like(acc_ref)
        accw_ref[...] = jnp.zeros_like(accw_ref)

    @pl.when(wv_ref[w] == 1)
    def _():
        row = lax.broadcasted_iota(jnp.int32, (TG, 1), 0) + wj_ref[w] * TG
        hit = pos_ref[0, 0] == row
        acc_ref[...] += _dot(jnp.where(hit, 1.0, 0.0).astype(BF16), h_ref[...])
        accw_ref[...] += jnp.sum(jnp.where(hit, cmb_ref[0, 0], 0.0), axis=-1, keepdims=True)

    @pl.when(wl_ref[w] == 1)
    def _():
        o_ref[...] = acc_ref[...].astype(BF16)
        pw_ref[...] = accw_ref[...]


def _moe_gather(work, tile_expert, pos_t, cmb_t, h2f, n_tiles):
    wj, ws, wf, wl, wv = work
    n, D = h2f.shape
    row_spec = pl.BlockSpec((1, 1, 1, TM), lambda w, wj, ws, wf, wl, wv, te: (te[wj[w]], ws[w], 0, 0))
    return pl.pallas_call(
        _gather_kernel,
        grid_spec=pltpu.PrefetchScalarGridSpec(
            num_scalar_prefetch=6,
            grid=(wj.shape[0],),
            in_specs=[row_spec, row_spec,
                      pl.BlockSpec((TM, D), lambda w, wj, ws, wf, wl, wv, te: (ws[w], 0))],
            out_specs=(pl.BlockSpec((TG, D), lambda w, wj, ws, wf, wl, wv, te: (wj[w], 0)),
                       pl.BlockSpec((TG, 1), lambda w, wj, ws, wf, wl, wv, te: (wj[w], 0))),
            scratch_shapes=[pltpu.VMEM((TG, D), F32), pltpu.VMEM((TG, 1), F32)]),
        out_shape=(jax.ShapeDtypeStruct((n_tiles * TG, D), BF16),
                   jax.ShapeDtypeStruct((n_tiles * TG, 1), F32)),
        compiler_params=_cparams(("arbitrary",)),
        name="moe_gather",
    )(wj, ws, wf, wl, wv, tile_expert, pos_t, cmb_t, h2f)


def _gffn_kernel(te_ref, tv_ref, x_ref, pw_ref, w1_ref, w3_ref, w2_ref, o_ref):
    j = pl.program_id(0)

    @pl.when(tv_ref[j] == 1)
    def _():
        f = _swiglu_acc(x_ref[...], w1_ref, w3_ref, w2_ref, (0,))
        o_ref[...] = (f * pw_ref[...]).astype(BF16)

    @pl.when(tv_ref[j] == 0)
    def _():
        o_ref[...] = jnp.zeros_like(o_ref)


def _moe_gffn(tile_expert, tile_valid, xg, pw, w1, w3, w2):
    P, D = xg.shape
    F = w1.shape[-1]
    wspec = lambda shape: pl.BlockSpec(shape, lambda j, te, tv: (te[j], 0, 0), pipeline_mode=pl.Buffered(1))
    return pl.pallas_call(
        _gffn_kernel,
        grid_spec=pltpu.PrefetchScalarGridSpec(
            num_scalar_prefetch=2,
            grid=(P // TG,),
            in_specs=[pl.BlockSpec((TG, D), lambda j, te, tv: (j, 0)),
                      pl.BlockSpec((TG, 1), lambda j, te, tv: (j, 0)),
                      wspec((1, D, F)), wspec((1, D, F)), wspec((1, F, D))],
            out_specs=pl.BlockSpec((TG, D), lambda j, te, tv: (j, 0))),
        out_shape=jax.ShapeDtypeStruct((P, D), BF16),
        compiler_params=_cparams(("arbitrary",)),
        name="moe_gffn",
    )(tile_expert, tile_valid, xg, pw, w1, w3, w2)


def _combine_kernel(wt_ref, ws_ref, wf_ref, wl_ref, wv_ref, pos_ref, y_ref, x_ref, g_ref, o_ref, acc_ref):
    w = pl.program_id(0)

    @pl.when(wf_ref[w] == 1)
    def _():
        acc_ref[...] = jnp.zeros_like(acc_ref)

    @pl.when(wv_ref[w] == 1)
    def _():
        col = lax.broadcasted_iota(jnp.int32, (1, TG), 1) + ws_ref[w] * TG
        pos = pos_ref[...]
        hit = (pos[:, 0:1] == col) | (pos[:, 1:2] == col)
        acc_ref[...] += _dot(jnp.where(hit, 1.0, 0.0).astype(BF16), y_ref[...])

    @pl.when(wl_ref[w] == 1)
    def _():
        o_ref[...] = x_ref[...] + g_ref[0] * acc_ref[...]


def _moe_combine(work, pos2, yw, xf, gate2, tiles_per_batch, nct, B):
    wt, ws, wf, wl, wv = work
    n, D = xf.shape

    def gidx(w, wt, ws, wf, wl, wv):
        b = wt[w] // tiles_per_batch
        return (jnp.where(wt[w] % tiles_per_batch < nct, B, b), 0, 0)

    return pl.pallas_call(
        _combine_kernel,
        grid_spec=pltpu.PrefetchScalarGridSpec(
            num_scalar_prefetch=5,
            grid=(wt.shape[0],),
            in_specs=[pl.BlockSpec((TM, 2), lambda w, wt, ws, wf, wl, wv: (wt[w], 0)),
                      pl.BlockSpec((TG, D), lambda w, wt, ws, wf, wl, wv: (ws[w], 0)),
                      pl.BlockSpec((TM, D), lambda w, wt, ws, wf, wl, wv: (wt[w], 0)),
                      pl.BlockSpec((1, 1, D), gidx)],
            out_specs=pl.BlockSpec((TM, D), lambda w, wt, ws, wf, wl, wv: (wt[w], 0)),
            scratch_shapes=[pltpu.VMEM((TM, D), F32)]),
        out_shape=jax.ShapeDtypeStruct((n, D), F32),
        compiler_params=_cparams(("arbitrary",)),
        name="moe_combine",
    )(wt, ws, wf, wl, wv, pos2, yw, xf, gate2)


def _work_list(group, item, valid, n_work):
    valid = valid.astype(jnp.int32)
    cs = jnp.cumsum(valid)
    total = cs[-1]
    w = jnp.minimum(jnp.arange(n_work, dtype=jnp.int32), total - 1)
    idx = _count_below(cs, w + 1)
    return _with_flags(group[idx], item[idx], n_work, total)


def _count_below(sorted_vals, x):
    return jnp.sum((sorted_vals[None, :] < x[:, None]).astype(jnp.int32), axis=1)


def _with_flags(g, it, n_work, total):
    live = jnp.arange(n_work, dtype=jnp.int32) < total
    g_prev = jnp.concatenate([jnp.full((1,), -1, jnp.int32), g[:-1]])
    g_next = jnp.concatenate([g[1:], jnp.full((1,), -1, jnp.int32)])
    live_next = jnp.concatenate([live[1:], jnp.zeros((1,), bool)])
    first = live & (g != g_prev)
    last = live & ((g != g_next) | ~live_next)
    i32 = lambda a: a.astype(jnp.int32)
    return i32(g), i32(it), i32(first), i32(last), i32(live)


def _moe_plan(sel, n_tok):
    E = N_EXPERTS
    sel = sel.astype(jnp.int32)
    cnt = jnp.sum(sel, axis=0)
    rank = jnp.cumsum(sel, axis=0) - 1
    gsz = ((cnt + TG - 1) // TG) * TG
    gend = jnp.cumsum(gsz)
    goff = gend - gsz
    n_tiles = (2 * n_tok + E * (TG - 1) + TG - 1) // TG
    P = n_tiles * TG
    pos = jnp.where(sel == 1, goff[None, :] + rank, -1)
    tile_start = jnp.arange(n_tiles, dtype=jnp.int32) * TG
    tile_valid = (tile_start < gend[-1]).astype(jnp.int32)
    tile_expert = jnp.minimum(_count_below(gend, tile_start + 1), E - 1)
    pmax_tok = jnp.max(pos, axis=1)
    psec_tok = jnp.max(jnp.where(pos == pmax_tok[:, None], -1, pos), axis=1)
    pos2 = jnp.stack([psec_tok, pmax_tok], axis=1)

    n_src_tiles = n_tok // TM
    cin = rank[TM - 1::TM] + 1
    ra = tile_start - goff[tile_expert]
    rb = ra + jnp.clip(cnt[tile_expert] - ra, 0, TG) - 1
    cin_t = cin.T[tile_expert]
    lo = jnp.sum((cin_t <= ra[:, None]).astype(jnp.int32), axis=1)
    hi = jnp.sum((cin_t <= rb[:, None]).astype(jnp.int32), axis=1)
    span = jnp.where(tile_valid == 1, hi - lo + 1, 0)
    n_gw = E * n_src_tiles + n_tiles
    cs = jnp.cumsum(span)
    total = cs[-1]
    w = jnp.minimum(jnp.arange(n_gw, dtype=jnp.int32), total - 1)
    jw = _count_below(cs, w + 1)
    sw = lo[jw] + (w - (cs[jw] - span[jw]))
    gwork = _with_flags(jw, sw, n_gw, total)
    i32 = lambda a: a.astype(jnp.int32)

    pt = pos.reshape(n_src_tiles, TM, E)
    pmax = jnp.max(pt, axis=1)
    pmin = jnp.min(jnp.where(pt >= 0, pt, P), axis=1)
    ta = pmin // TG
    tb = pmax // TG
    cand_item = jnp.stack([ta, tb], axis=-1).reshape(-1)
    cand_valid = jnp.stack([pmax >= 0, (pmax >= 0) & (tb != ta)], axis=-1).reshape(-1)
    cand_group = jnp.repeat(jnp.arange(n_src_tiles, dtype=jnp.int32), 2 * E)
    n_cw = E * n_src_tiles + n_tiles
    cwork = _work_list(cand_group, i32(cand_item), cand_valid, n_cw)
    return pos, pos2, tile_expert, tile_valid, gwork, cwork, n_tiles


def _moe(h2, xs1, cmb, sel, gate2, w1, w3, w2, nct):
    B, T, D = xs1.shape
    n_tok = B * T
    E = N_EXPERTS
    selm = sel.reshape(n_tok, LANES)[:, :E] > 0.5
    cmbm = cmb.reshape(n_tok, LANES)[:, :E]
    pos, pos2, tile_expert, tile_valid, gwork, cwork, n_tiles = _moe_plan(selm, n_tok)
    pos_t = pos.T.reshape(E, n_tok // TM, 1, TM)
    cmb_t = cmbm.T.reshape(E, n_tok // TM, 1, TM)

    xg, pw = _moe_gather(gwork, tile_expert, pos_t, cmb_t, h2.reshape(n_tok, D), n_tiles)
    yw = _moe_gffn(tile_expert, tile_valid, xg, pw, w1, w3, w2)
    out = _moe_combine(cwork, pos2, yw, xs1.reshape(n_tok, D), gate2, T // TM, nct, B)
    return out.reshape(B, T, D)


def _final_kernel(x_ref, g_ref, o_ref):
    x = x_ref[0]
    o_ref[0] = x * lax.rsqrt(jnp.mean(x * x, axis=-1, keepdims=True) + EPS) * g_ref[...]


def _final_norm(xs, g, nct, S):
    B, T, D = xs.shape
    return pl.pallas_call(
        _final_kernel,
        grid=(B, S // TM),
        in_specs=[pl.BlockSpec((1, TM, D), lambda b, i: (b, i + nct, 0)), _const_spec((1, D))],
        out_specs=pl.BlockSpec((1, TM, D), lambda b, i: (b, i, 0)),
        out_shape=jax.ShapeDtypeStruct((B, S, D), F32),
        compiler_params=_cparams(("arbitrary", "arbitrary")),
        name="final_norm",
    )(xs, g)


def _rope_tables(L, S):
    t = np.arange(S)
    pos = np.stack([t // GRID_W, t % GRID_W], axis=-1).astype(np.float32)
    inv_freq = (ROPE_THETA ** (-np.arange(0, AXIS_ROPE_DIM, 2, dtype=np.float32) / AXIS_ROPE_DIM)).astype(np.float32)
    ang = pos[:, :, None] * inv_freq[None, None, :]
    cos, sin = np.cos(ang), np.sin(ang)
    cos64 = np.concatenate([cos[:, 0], cos[:, 0], cos[:, 1], cos[:, 1]], axis=-1)
    sin64 = np.concatenate([-sin[:, 0], sin[:, 0], -sin[:, 1], sin[:, 1]], axis=-1)
    cos_t = np.concatenate([np.ones((L, HEAD_DIM), np.float32), cos64], axis=0)
    sin_t = np.concatenate([np.zeros((L, HEAD_DIM), np.float32), sin64], axis=0)
    rep = LANES // HEAD_DIM
    return jnp.asarray(np.tile(cos_t, (1, rep)), F32), jnp.asarray(np.tile(sin_t, (1, rep)), F32)


def _na_bias_table(rel_bias):
    col = np.arange(GRID_W)
    col_start = np.clip(col - NA_WIN_COLS // 2, 0, GRID_W - NA_WIN_COLS)
    kc = np.arange(GRID_W)
    inside = (kc[None, :] >= col_start[:, None]) & (kc[None, :] < col_start[:, None] + NA_WIN_COLS)
    dc = np.clip(kc[None, :] - col[:, None] + (NA_WIN_COLS - 1), 0, 2 * NA_WIN_COLS - 2)
    case = np.arange(NA_WIN_ROWS)
    dr = np.arange(NA_WIN_ROWS)[None, :] - case[:, None] + (NA_WIN_ROWS - 1)
    t = rel_bias[:, dr]
    t = t[:, :, :, dc]
    t = jnp.where(jnp.asarray(inside)[None, None, None], t, MASK_VALUE)
    t = t.transpose(1, 0, 3, 2, 4)
    return t.reshape(NA_WIN_ROWS, B_HEADS, GRID_W, NA_WIN_ROWS * GRID_W).astype(F32)


def _block_diag_ones(n, blk):
    idx = np.arange(n) // blk
    return jnp.asarray((idx[:, None] == idx[None, :]).astype(np.float32), BF16)


def _pool_block_diag(pool_w):
    g, c, d = pool_w.shape
    out = jnp.zeros((g * c, g * d), pool_w.dtype)
    for k in range(g):
        out = out.at[k * c:(k + 1) * c, k * d:(k + 1) * d].set(pool_w[k])
    return out.astype(BF16)


def kernel(x, c, ctx, c_ctx, w_mod, b_mod, norm1_g, norm2_g, w_in, q_norm_g, k_norm_g, na_rel_bias, pool_w,
           pool_scale, w_branch_a, w_branch_b, w_branch_c, w_out, ffn_w1, ffn_w3, ffn_w2, router_w, router_b,
           moe_w1, moe_w3, moe_w2, final_g):
    B, S, D = x.shape
    L = ctx.shape[1]
    T = L + S
    depth = w_mod.shape[0]
    assert D == D_MODEL and L % TM == 0 and S % TM == 0 and TM % GRID_W == 0 and B + 1 <= SUBLANES
    assert S // GRID_W >= NA_WIN_ROWS and TM // GRID_W <= NA_WIN_ROWS // 2 + 1 and TM <= TG
    nct = L // TM

    cvec = jnp.zeros((SUBLANES, D), F32).at[:B].set(c).at[B].set(c_ctx)
    mod = _mod_vectors(cvec, w_mod, b_mod)
    cos_t, sin_t = _rope_tables(L, S)
    bd = _block_diag_ones(QK_W, HEAD_DIM)

    xs = jnp.concatenate([ctx, x], axis=1)
    for layer in range(depth):
        modr = mod[layer].reshape(SUBLANES, 6, D)
        gqk = jnp.concatenate([jnp.tile(q_norm_g[layer], A_HEADS), jnp.tile(k_norm_g[layer], A_KV_HEADS)])
        q, kt, v, bq, bk, bv, cu, sg = _inproj(
            xs, modr, norm1_g[layer].reshape(1, D), w_in[layer].astype(BF16), bd, gqk.reshape(1, QK_W),
            cos_t, sin_t, nct)
        ya = _attn_a(q, kt, v, nct)
        yb = _attn_b(bq, bk, bv, _na_bias_table(na_rel_bias[layer]), nct, L)
        is_moe = layer % 2 == 1
        i = layer // 2
        router = None
        if is_moe:
            rw = jnp.zeros((D, LANES), F32).at[:, :N_EXPERTS].set(router_w[i])
            rw_hi = rw.astype(BF16)
            rw_lo = (rw - rw_hi.astype(F32)).astype(BF16)
            rb = jnp.zeros((1, LANES), F32).at[0, :N_EXPERTS].set(router_b[i])
            router = (rw_hi, rw_lo, rb)
        outs = _merge(ya, yb, cu, sg, xs, modr, w_branch_a[layer].astype(BF16), w_branch_b[layer].astype(BF16),
                      w_branch_c[layer].astype(BF16), _pool_block_diag(pool_w[layer]),
                      pool_scale[layer].reshape(1, C_WIDTH), w_out[layer].astype(BF16),
                      norm2_g[layer].reshape(1, D), router, nct, L)
        gate2 = modr[:, 5:6, :]
        if is_moe:
            xs1, h2, cmb, sel = outs
            xs = _moe(h2, xs1, cmb, sel, gate2, moe_w1[i].astype(BF16), moe_w3[i].astype(BF16),
                      moe_w2[i].astype(BF16), nct)
        else:
            xs1, h2 = outs
            xs = _ffn_dense(h2, xs1, gate2, ffn_w1[i].astype(BF16), ffn_w3[i].astype(BF16),
                            ffn_w2[i].astype(BF16), nct)
    return _final_norm(xs, final_g.reshape(1, D), nct, S)
```

```python
import functools

import numpy as np
import jax
import jax.numpy as jnp
from jax import lax
from jax.experimental import pallas as pl
from jax.experimental.pallas import tpu as pltpu

F32 = jnp.float32
BF16 = jnp.bfloat16

D_MODEL = 1024
DEPTH = 4
GRID_W = 64
HEAD_DIM = 64
A_HEADS = 8
A_KV_HEADS = 2
A_GROUP = A_HEADS // A_KV_HEADS
B_HEADS = 4
C_GROUPS = 4
C_GROUP_DIM = 64
C_WIDTH = C_GROUPS * C_GROUP_DIM
POOL_WINDOWS = (2, 4, 8, 16)
NA_WIN_ROWS = 8
NA_WIN_COLS = 16
ROPE_THETA = 10000.0
AXIS_ROPE_DIM = HEAD_DIM // 2
N_EXPERTS = 8
EPS = 1e-6

A_Q = A_HEADS * HEAD_DIM
A_KV = A_KV_HEADS * HEAD_DIM
B_W = B_HEADS * HEAD_DIM
QK_W = A_Q + A_KV
OFF_AV = QK_W
OFF_BQ = OFF_AV + A_KV
OFF_BK = OFF_BQ + B_W
OFF_BV = OFF_BK + B_W
OFF_CU = OFF_BV + B_W
OFF_G = OFF_CU + C_WIDTH

LANES = 128
SUBLANES = 8
TM = 256
TX = 512
FC = 512
A_KBLK = 4
LOG2E = 1.4426950408889634
POOL_HALO = 8
MASK_VALUE = -1e30
VMEM_LIMIT = 56 * 1024 * 1024


def _cparams(sem):
    return pltpu.CompilerParams(dimension_semantics=sem, vmem_limit_bytes=VMEM_LIMIT)


def _dot(a, b):
    return jnp.dot(a, b, preferred_element_type=F32)


def _dot_nt(a, b):
    return lax.dot_general(a, b, (((1,), (1,)), ((), ())), preferred_element_type=F32)


def _split(a):
    hi = a.astype(BF16)
    lo = (a - hi.astype(F32)).astype(BF16)
    return hi, lo


def _dot3(a, b_hi, b_lo):
    a_hi, a_lo = _split(a)
    return _dot(a_hi, b_hi) + (_dot(a_lo, b_hi) + _dot(a_hi, b_lo))


def _const_spec(shape):
    n = len(shape)
    return pl.BlockSpec(shape, lambda *_: (0,) * n, pipeline_mode=pl.Buffered(1))


def _layer_spec(shape, layer):
    n = len(shape)
    return pl.BlockSpec((1,) + tuple(shape), lambda *_: (layer,) + (0,) * n, pipeline_mode=pl.Buffered(1))


def _rms_mod(x, g, shift, scale):
    y = x * lax.rsqrt(jnp.mean(x * x, axis=-1, keepdims=True) + EPS) * g
    return y * (1.0 + scale) + shift


def _mod_kernel(c_ref, w_ref, b_ref, o_ref):
    c = c_ref[...]
    s = c * jax.nn.sigmoid(c)
    w_hi, w_lo = _split(w_ref[0])
    o_ref[0] = _dot3(s, w_hi, w_lo) + b_ref[0]


def _mod_vectors(cvec, w_mod, b_mod):
    depth, d, n = w_mod.shape
    tn = 1536
    return pl.pallas_call(
        _mod_kernel,
        grid=(depth, n // tn),
        in_specs=[pl.BlockSpec((SUBLANES, d), lambda l, j: (0, 0)),
                  pl.BlockSpec((1, d, tn), lambda l, j: (l, 0, j)),
                  pl.BlockSpec((1, 1, tn), lambda l, j: (l, 0, j))],
        out_specs=pl.BlockSpec((1, SUBLANES, tn), lambda l, j: (l, 0, j)),
        out_shape=jax.ShapeDtypeStruct((depth, SUBLANES, n), F32),
        compiler_params=_cparams(("arbitrary", "arbitrary")),
        name="mod_vectors",
    )(cvec, w_mod, b_mod.reshape(depth, 1, n))


def _swap16(x):
    lane = lax.broadcasted_iota(jnp.int32, (1, LANES), 1)
    first = (lane % 32) < 16
    return jnp.where(first, pltpu.roll(x, LANES - 16, 1), pltpu.roll(x, 16, 1))


def _inproj_kernel(x_ref, mod_ref, g_ref, w_ref, bd_ref, gqk_ref, cos_ref, sin_ref,
                   q_ref, kt_ref, v_ref, bq_ref, bk_ref, bv_ref, cu_ref):
    m = mod_ref[0]
    h = _rms_mod(x_ref[0], g_ref[...], m[0:1], m[1:2])
    px = _dot(h.astype(BF16), w_ref[0])

    qk = px[:, :QK_W]
    sq_hi, sq_lo = _split(qk * qk)
    ss = _dot(sq_hi, bd_ref[...]) + _dot(sq_lo, bd_ref[...])
    qn = qk * lax.rsqrt(ss * (1.0 / HEAD_DIM) + EPS) * gqk_ref[...]
    cos = cos_ref[...]
    sin = sin_ref[...]
    chunks = []
    for j in range(QK_W // LANES):
        c = qn[:, j * LANES:(j + 1) * LANES]
        chunks.append(c * cos + _swap16(c) * sin)
    scale = HEAD_DIM ** -0.5
    for j in range(A_Q // LANES):
        q_ref[0, :, j * LANES:(j + 1) * LANES] = (chunks[j] * (scale * LOG2E)).astype(BF16)
    kt_ref[0, 0] = chunks[A_Q // LANES].T.astype(BF16)

    ones = jnp.ones((TM, LANES - HEAD_DIM), BF16)
    for j in range(A_KV_HEADS):
        vj = px[:, OFF_AV + j * HEAD_DIM:OFF_AV + (j + 1) * HEAD_DIM].astype(BF16)
        v_ref[0, :, j * LANES:(j + 1) * LANES] = jnp.concatenate([vj, ones], axis=1)
    bq_ref[0] = (px[:, OFF_BQ:OFF_BK] * scale).astype(BF16)
    bk_ref[0] = px[:, OFF_BK:OFF_BV].astype(BF16)
    bv_ref[0] = px[:, OFF_BV:OFF_CU].astype(BF16)
    cu_ref[0] = px[:, OFF_CU:OFF_G]


def _inproj(xs, modr, g1, w_qkv, layer, bd, gqk, cos_t, sin_t, nct):
    B, T, D = xs.shape
    nt = T // TM
    tok = lambda w: pl.BlockSpec((1, TM, w), lambda b, i: (b, i, 0))
    out_shape = (
        jax.ShapeDtypeStruct((B, T, A_Q), BF16),
        jax.ShapeDtypeStruct((B, nt, A_KV, TM), BF16),
        jax.ShapeDtypeStruct((B, T, A_KV_HEADS * LANES), BF16),
        jax.ShapeDtypeStruct((B, T, B_W), BF16),
        jax.ShapeDtypeStruct((B, T, B_W), BF16),
        jax.ShapeDtypeStruct((B, T, B_W), BF16),
        jax.ShapeDtypeStruct((B, T, C_WIDTH), F32),
    )
    return pl.pallas_call(
        _inproj_kernel,
        grid=(B, nt),
        in_specs=[tok(D),
                  pl.BlockSpec((1, 6, D), lambda b, i: (jnp.where(i < nct, B, b), 0, 0)),
                  _const_spec((1, D)),
                  _layer_spec((D, OFF_G), layer),
                  _const_spec((QK_W, QK_W)),
                  _const_spec((1, QK_W)),
                  pl.BlockSpec((TM, LANES), lambda b, i: (i, 0)),
                  pl.BlockSpec((TM, LANES), lambda b, i: (i, 0))],
        out_specs=(tok(A_Q),
                   pl.BlockSpec((1, 1, A_KV, TM), lambda b, i: (b, i, 0, 0)),
                   tok(A_KV_HEADS * LANES), tok(B_W), tok(B_W), tok(B_W), tok(C_WIDTH)),
        out_shape=out_shape,
        compiler_params=_cparams(("arbitrary", "arbitrary")),
        name="inproj",
    )(xs, modr, g1, w_qkv, bd, gqk, cos_t, sin_t)


def _attn_a_kernel(q_ref, kt_ref, v_ref, o_ref, *, nct, n_steps):
    i = pl.program_id(1)
    rows = A_GROUP * TM

    def step(carry, q4, j, blk0, nblk):
        m, acc = carry
        s_list = [_dot(q4, kt_ref[0, blk0 + c, j * HEAD_DIM:(j + 1) * HEAD_DIM, :]) for c in range(nblk)]
        smax = s_list[0]
        for s in s_list[1:]:
            smax = jnp.maximum(smax, s)
        m_new = jnp.maximum(m, jnp.max(smax, axis=-1, keepdims=True))
        alpha = jnp.exp2(m - m_new)
        p = jnp.concatenate([jnp.exp2((s - m_new).astype(BF16)) for s in s_list], axis=1)
        vv = v_ref[0, pl.ds(pl.multiple_of(blk0 * TM, TM), nblk * TM), j * LANES:(j + 1) * LANES]
        return m_new, alpha * acc + _dot(p, vv)

    def run(n_main):
        for j in range(A_KV_HEADS):
            base = j * A_GROUP * HEAD_DIM
            q4 = jnp.concatenate(
                [q_ref[0, :, base + g * HEAD_DIM: base + (g + 1) * HEAD_DIM] for g in range(A_GROUP)], axis=0)
            carry = (jnp.full((rows, 1), MASK_VALUE, F32), jnp.zeros((rows, LANES), F32))
            carry = step(carry, q4, j, 0, nct)
            if n_main:
                carry = lax.fori_loop(
                    0, n_main, lambda k, cr: step(cr, q4, j, nct + k * A_KBLK, A_KBLK), carry)
            acc = carry[1]
            o = acc[:, 0:HEAD_DIM] / acc[:, HEAD_DIM:HEAD_DIM + 1]
            for g in range(A_GROUP):
                o_ref[0, :, base + g * HEAD_DIM: base + (g + 1) * HEAD_DIM] = o[g * TM:(g + 1) * TM].astype(BF16)

    @pl.when(i < nct)
    def _():
        run(0)

    @pl.when(i >= nct)
    def _():
        run(n_steps)


def _attn_a(q, kt, v, nct):
    B, T, _ = q.shape
    nt = T // TM
    assert (nt - nct) % A_KBLK == 0
    return pl.pallas_call(
        functools.partial(_attn_a_kernel, nct=nct, n_steps=(nt - nct) // A_KBLK),
        grid=(B, nt),
        in_specs=[pl.BlockSpec((1, TM, A_Q), lambda b, i: (b, i, 0)),
                  pl.BlockSpec((1, nt, A_KV, TM), lambda b, i: (b, 0, 0, 0)),
                  pl.BlockSpec((1, T, A_KV_HEADS * LANES), lambda b, i: (b, 0, 0))],
        out_specs=pl.BlockSpec((1, TM, A_Q), lambda b, i: (b, i, 0)),
        out_shape=jax.ShapeDtypeStruct((B, T, A_Q), BF16),
        compiler_params=_cparams(("arbitrary", "arbitrary")),
        name="attn_a",
    )(q, kt, v)


def _attn_b_kernel(q_ref, kp_ref, kc_ref, kn_ref, vp_ref, vc_ref, vn_ref, kx_ref, vx_ref, bias_ref,
                   o_ref, kbuf, vbuf, *, nct, grid_rows):
    i = pl.program_id(1)
    rpt = TM // GRID_W
    win = NA_WIN_ROWS * GRID_W

    @pl.when(i < nct)
    def _():
        for h in range(B_HEADS):
            sl = slice(h * HEAD_DIM, (h + 1) * HEAD_DIM)
            s = _dot_nt(q_ref[0, :, sl], kx_ref[0, :, sl])
            m = jnp.max(s, axis=-1, keepdims=True)
            p = jnp.exp(s - m)
            l = jnp.sum(p, axis=-1, keepdims=True)
            o_ref[0, :, sl] = (_dot(p.astype(BF16), vx_ref[0, :, sl]) / l).astype(BF16)

    @pl.when(i >= nct)
    def _():
        kbuf[0:TM] = kp_ref[0]
        kbuf[TM:2 * TM] = kc_ref[0]
        kbuf[2 * TM:3 * TM] = kn_ref[0]
        vbuf[0:TM] = vp_ref[0]
        vbuf[TM:2 * TM] = vc_ref[0]
        vbuf[2 * TM:3 * TM] = vn_ref[0]
        r0 = (i - nct) * rpt

        for a in range(rpt):
            qr = r0 + a
            rs = jnp.clip(qr - NA_WIN_ROWS // 2, 0, grid_rows - NA_WIN_ROWS)
            case = qr - rs
            start = pl.multiple_of((rs - r0 + rpt) * GRID_W, GRID_W)
            kw = kbuf[pl.ds(start, win), :]
            vw = vbuf[pl.ds(start, win), :]
            qrow = q_ref[0, a * GRID_W:(a + 1) * GRID_W, :]
            for h in range(B_HEADS):
                sl = slice(h * HEAD_DIM, (h + 1) * HEAD_DIM)
                s_w = _dot_nt(qrow[:, sl], kw[:, sl]) + bias_ref[case, h]
                s_c = _dot_nt(qrow[:, sl], kx_ref[0, :, sl])
                m = jnp.maximum(jnp.max(s_w, axis=-1, keepdims=True), jnp.max(s_c, axis=-1, keepdims=True))
                p_w = jnp.exp(s_w - m)
                p_c = jnp.exp(s_c - m)
                l = jnp.sum(p_w, axis=-1, keepdims=True) + jnp.sum(p_c, axis=-1, keepdims=True)
                o = _dot(p_w.astype(BF16), vw[:, sl]) + _dot(p_c.astype(BF16), vx_ref[0, :, sl])
                o_ref[0, a * GRID_W:(a + 1) * GRID_W, sl] = (o / l).astype(BF16)


def _attn_b(bq, bk, bv, bias_t, nct, L):
    B, T, _ = bq.shape
    nt = T // TM
    grid_rows = (T - L) // GRID_W
    cur = lambda b, i: (b, i, 0)
    prev = lambda b, i: (b, jnp.maximum(i - 1, nct), 0)
    nxt = lambda b, i: (b, jnp.minimum(i + 1, nt - 1), 0)
    blk = lambda f: pl.BlockSpec((1, TM, B_W), f)
    ctx = pl.BlockSpec((1, L, B_W), lambda b, i: (b, 0, 0))
    return pl.pallas_call(
        functools.partial(_attn_b_kernel, nct=nct, grid_rows=grid_rows),
        grid=(B, nt),
        in_specs=[blk(cur), blk(prev), blk(cur), blk(nxt), blk(prev), blk(cur), blk(nxt), ctx, ctx,
                  _const_spec(bias_t.shape)],
        out_specs=blk(cur),
        out_shape=jax.ShapeDtypeStruct((B, T, B_W), BF16),
        scratch_shapes=[pltpu.VMEM((3 * TM, B_W), BF16), pltpu.VMEM((3 * TM, B_W), BF16)],
        compiler_params=_cparams(("arbitrary", "arbitrary")),
        name="attn_b",
    )(bq, bk, bk, bk, bv, bv, bv, bk, bv, bias_t)


def _merge_kernel(*refs, nct, nt, L, S, with_router):
    if with_router:
        (ya_ref, yb_ref, cup_ref, cu_ref, cun_ref, x_ref, mod_ref, g1_ref, wg_ref, wpa_ref, wpb_ref, wpc_ref,
         pbd_ref, psc_ref, wo_ref, g2_ref, rwh_ref, rwl_ref, rb_ref,
         xo_ref, h_ref, cmb_ref, sel_ref, e_scr) = refs
    else:
        (ya_ref, yb_ref, cup_ref, cu_ref, cun_ref, x_ref, mod_ref, g1_ref, wg_ref, wpa_ref, wpb_ref, wpc_ref,
         pbd_ref, psc_ref, wo_ref, g2_ref, xo_ref, h_ref, e_scr) = refs
    i = pl.program_id(1)
    D = D_MODEL
    m = mod_ref[0]
    x_in = x_ref[0]

    at_start = (i == 0) | (i == nct)
    at_end = (i == nct - 1) | (i == nt - 1)
    u = cu_ref[0]
    e_scr[0:POOL_HALO] = jnp.where(at_start, 0.0, cup_ref[0])
    e_scr[POOL_HALO:POOL_HALO + TM] = u
    e_scr[POOL_HALO + TM:] = jnp.where(at_end, 0.0, cun_ref[0])
    sh = lambda k: e_scr[POOL_HALO + k:POOL_HALO + k + TM]
    w2 = sh(-1) + sh(0)
    w4 = w2 + (sh(-2) + sh(1))
    w8 = w4 + ((sh(-4) + sh(-3)) + (sh(2) + sh(3)))
    w16 = w8 + (((sh(-8) + sh(-7)) + (sh(-6) + sh(-5))) + ((sh(4) + sh(5)) + (sh(6) + sh(7))))
    grp = lax.broadcasted_iota(jnp.int32, (1, C_WIDTH), 1) // C_GROUP_DIM
    half = jnp.left_shift(1, grp)
    t_loc = lax.broadcasted_iota(jnp.int32, (TM, 1), 0)
    t_seq = jnp.where(i < nct, i * TM, (i - nct) * TM) + t_loc
    n_seq = jnp.where(i < nct, L, S)
    cnt = jnp.minimum(t_seq + half, n_seq) - jnp.maximum(t_seq - half, 0)
    wsum = jnp.where(grp == 0, w2, jnp.where(grp == 1, w4, jnp.where(grp == 2, w8, w16)))
    dlt = wsum / cnt.astype(F32) - u
    yc = _dot(dlt.astype(BF16), pbd_ref[0]) * psc_ref[...]

    h1 = _rms_mod(x_in, g1_ref[...], m[0:1], m[1:2]).astype(BF16)
    branches = (ya_ref[0], yb_ref[0], yc.astype(BF16))
    weights = (wpa_ref, wpb_ref, wpc_ref)
    mrg = None
    for k in range(3):
        gate = jax.nn.sigmoid(_dot(h1, wg_ref[0, :, k * D:(k + 1) * D]))
        term = gate * _dot(branches[k], weights[k][0])
        mrg = term if mrg is None else mrg + term
    x = x_in + m[2:3] * _dot(mrg.astype(BF16), wo_ref[0])
    xo_ref[0] = x
    h = _rms_mod(x, g2_ref[...], m[3:4], m[4:5])
    h_ref[0] = h.astype(BF16)

    if with_router:
        lane = lax.broadcasted_iota(jnp.int32, (1, LANES), 1).astype(F32)
        lg = _dot3(h, rwh_ref[...], rwl_ref[...]) + rb_ref[...]
        lg = jnp.where(lane < N_EXPERTS, lg, -jnp.inf)
        m1 = jnp.max(lg, axis=-1, keepdims=True)
        i1 = jnp.min(jnp.where(lg == m1, lane, float(LANES)), axis=-1, keepdims=True)
        mask1 = lane == i1
        lg2 = jnp.where(mask1, -jnp.inf, lg)
        m2 = jnp.max(lg2, axis=-1, keepdims=True)
        i2 = jnp.min(jnp.where(lg2 == m2, lane, float(LANES)), axis=-1, keepdims=True)
        mask2 = lane == i2
        e2 = jnp.exp(m2 - m1)
        den = 1.0 + e2
        cmb_ref[0] = jnp.where(mask1, 1.0 / den, 0.0) + jnp.where(mask2, e2 / den, 0.0)
        sel_ref[0] = jnp.where(mask1 | mask2, 1.0, 0.0)


def _merge(ya, yb, cu, xs, modr, g1, w_gate, wpa, wpb, wpc, pbd, psc, wo, g2, layer, router, nct, L):
    B, T, D = xs.shape
    nt = T // TM
    S = T - L
    hb = TM // POOL_HALO
    tok = lambda w: pl.BlockSpec((1, TM, w), lambda b, i: (b, i, 0))
    in_specs = [tok(A_Q), tok(B_W),
                pl.BlockSpec((1, POOL_HALO, C_WIDTH), lambda b, i: (b, jnp.maximum(i * hb - 1, 0), 0)),
                tok(C_WIDTH),
                pl.BlockSpec((1, POOL_HALO, C_WIDTH), lambda b, i: (b, jnp.minimum((i + 1) * hb, nt * hb - 1), 0)),
                tok(D),
                pl.BlockSpec((1, 6, D), lambda b, i: (jnp.where(i < nct, B, b), 0, 0)),
                _const_spec((1, D)),
                _layer_spec((D, 3 * D), layer),
                _layer_spec((A_Q, D), layer), _layer_spec((B_W, D), layer), _layer_spec((C_WIDTH, D), layer),
                _layer_spec((C_WIDTH, C_WIDTH), layer), _const_spec((1, C_WIDTH)), _layer_spec((D, D), layer),
                _const_spec((1, D))]
    args = [ya, yb, cu, cu, cu, xs, modr, g1, w_gate, wpa, wpb, wpc, pbd, psc, wo, g2]
    out_specs = [tok(D), tok(D)]
    out_shape = [jax.ShapeDtypeStruct((B, T, D), F32), jax.ShapeDtypeStruct((B, T, D), BF16)]
    if router is not None:
        in_specs += [_const_spec((D, LANES)), _const_spec((D, LANES)), _const_spec((1, LANES))]
        args += list(router)
        out_specs += [tok(LANES), tok(LANES)]
        out_shape += [jax.ShapeDtypeStruct((B, T, LANES), F32)] * 2
    return pl.pallas_call(
        functools.partial(_merge_kernel, nct=nct, nt=nt, L=L, S=S, with_router=router is not None),
        grid=(B, nt),
        in_specs=in_specs,
        out_specs=tuple(out_specs),
        out_shape=tuple(out_shape),
        scratch_shapes=[pltpu.VMEM((TM + 2 * POOL_HALO, C_WIDTH), F32)],
        compiler_params=_cparams(("arbitrary", "arbitrary")),
        name="merge_router" if router is not None else "merge",
    )(*args)


def _swiglu_acc(h, w1_ref, w3_ref, w2_ref, lead):
    d_ff = w1_ref.shape[-1]
    acc = None
    for f0 in range(0, d_ff, FC):
        f1 = min(f0 + FC, d_ff)
        a = _dot(h, w1_ref[lead + (slice(None), slice(f0, f1))])
        b = _dot(h, w3_ref[lead + (slice(None), slice(f0, f1))])
        t = (a * jax.nn.sigmoid(a) * b).astype(BF16)
        part = _dot(t, w2_ref[lead + (slice(f0, f1), slice(None))])
        acc = part if acc is None else acc + part
    return acc


def _ffn_kernel(h_ref, x_ref, g_ref, w1_ref, w3_ref, w2_ref, o_ref):
    f = _swiglu_acc(h_ref[0], w1_ref, w3_ref, w2_ref, (0,))
    o_ref[0] = x_ref[0] + g_ref[0] * f


def _ffn_dense(h2, xs, gate2, w1, w3, w2, idx, nct):
    B, T, D = xs.shape
    F = w1.shape[-1]
    nt = T // TM
    tok = lambda: pl.BlockSpec((1, TM, D), lambda b, i: (b, i, 0))
    return pl.pallas_call(
        _ffn_kernel,
        grid=(B, nt),
        in_specs=[tok(), tok(),
                  pl.BlockSpec((1, 1, D), lambda b, i: (jnp.where(i < nct, B, b), 0, 0)),
                  _layer_spec((D, F), idx), _layer_spec((D, F), idx), _layer_spec((F, D), idx)],
        out_specs=tok(),
        out_shape=jax.ShapeDtypeStruct((B, T, D), F32),
        compiler_params=_cparams(("arbitrary", "arbitrary")),
        name="ffn_dense",
    )(h2, xs, gate2, w1, w3, w2)


def _gather_kernel(wj_ref, ws_ref, wf_ref, wl_ref, wv_ref, te_ref, pos_ref, cmb_ref, h_ref,
                   o_ref, pw_ref, acc_ref, accw_ref):
    w = pl.program_id(0)

    @pl.when(wf_ref[w] == 1)
    def _():
        acc_ref[...] = jnp.zeros_like(acc_ref)
        accw_ref[...] = jnp.zeros_like(accw_ref)

    @pl.when(wv_ref[w] == 1)
    def _():
        row = lax.broadcasted_iota(jnp.int32, (TX, 1), 0) + wj_ref[w] * TX
        hit = pos_ref[0, 0] == row
        acc_ref[...] += _dot(jnp.where(hit, 1.0, 0.0).astype(BF16), h_ref[...])
        accw_ref[...] += jnp.sum(jnp.where(hit, cmb_ref[0, 0], 0.0), axis=-1, keepdims=True)

    @pl.when(wl_ref[w] == 1)
    def _():
        o_ref[...] = acc_ref[...].astype(BF16)
        pw_ref[...] = accw_ref[...]


def _moe_gather(work, tile_expert, pos_t, cmb_t, h2f, n_tiles):
    wj, ws, wf, wl, wv = work
    n, D = h2f.shape
    row_spec = pl.BlockSpec((1, 1, 1, TX), lambda w, wj, ws, wf, wl, wv, te: (te[wj[w]], ws[w], 0, 0))
    return pl.pallas_call(
        _gather_kernel,
        grid_spec=pltpu.PrefetchScalarGridSpec(
            num_scalar_prefetch=6,
            grid=(wj.shape[0],),
            in_specs=[row_spec, row_spec,
                      pl.BlockSpec((TX, D), lambda w, wj, ws, wf, wl, wv, te: (ws[w], 0))],
            out_specs=(pl.BlockSpec((TX, D), lambda w, wj, ws, wf, wl, wv, te: (wj[w], 0)),
                       pl.BlockSpec((TX, 1), lambda w, wj, ws, wf, wl, wv, te: (wj[w], 0))),
            scratch_shapes=[pltpu.VMEM((TX, D), F32), pltpu.VMEM((TX, 1), F32)]),
        out_shape=(jax.ShapeDtypeStruct((n_tiles * TX, D), BF16),
                   jax.ShapeDtypeStruct((n_tiles * TX, 1), F32)),
        compiler_params=_cparams(("arbitrary",)),
        name="moe_gather",
    )(wj, ws, wf, wl, wv, tile_expert, pos_t, cmb_t, h2f)


def _gffn_kernel(te_ref, tv_ref, x_ref, pw_ref, w1_ref, w3_ref, w2_ref, o_ref):
    j = pl.program_id(0)

    @pl.when(tv_ref[j] == 1)
    def _():
        f = _swiglu_acc(x_ref[...], w1_ref, w3_ref, w2_ref, (0, 0))
        o_ref[...] = (f * pw_ref[...]).astype(BF16)

    @pl.when(tv_ref[j] == 0)
    def _():
        o_ref[...] = jnp.zeros_like(o_ref)


def _moe_gffn(tile_expert, tile_valid, xg, pw, w1, w3, w2, idx):
    P, D = xg.shape
    F = w1.shape[-1]
    wspec = lambda shape: pl.BlockSpec((1, 1) + shape, lambda j, te, tv: (idx, te[j], 0, 0),
                                       pipeline_mode=pl.Buffered(1))
    return pl.pallas_call(
        _gffn_kernel,
        grid_spec=pltpu.PrefetchScalarGridSpec(
            num_scalar_prefetch=2,
            grid=(P // TX,),
            in_specs=[pl.BlockSpec((TX, D), lambda j, te, tv: (j, 0)),
                      pl.BlockSpec((TX, 1), lambda j, te, tv: (j, 0)),
                      wspec((D, F)), wspec((D, F)), wspec((F, D))],
            out_specs=pl.BlockSpec((TX, D), lambda j, te, tv: (j, 0))),
        out_shape=jax.ShapeDtypeStruct((P, D), BF16),
        compiler_params=_cparams(("arbitrary",)),
        name="moe_gffn",
    )(tile_expert, tile_valid, xg, pw, w1, w3, w2)


def _combine_kernel(wt_ref, ws_ref, wf_ref, wl_ref, wv_ref, pos_ref, y_ref, x_ref, g_ref, o_ref, acc_ref,
                    *, n_batch, T, L):
    w = pl.program_id(0)

    @pl.when(wf_ref[w] == 1)
    def _():
        acc_ref[...] = jnp.zeros_like(acc_ref)

    @pl.when(wv_ref[w] == 1)
    def _():
        col = lax.broadcasted_iota(jnp.int32, (1, TX), 1) + ws_ref[w] * TX
        pos = pos_ref[...]
        hit = (pos[:, 0:1] == col) | (pos[:, 1:2] == col)
        acc_ref[...] += _dot(jnp.where(hit, 1.0, 0.0).astype(BF16), y_ref[...])

    @pl.when(wl_ref[w] == 1)
    def _():
        n = lax.broadcasted_iota(jnp.int32, (TX, 1), 0) + wt_ref[w] * TX
        b = jnp.zeros((TX, 1), jnp.int32)
        for k in range(1, n_batch):
            b = b + (n >= k * T).astype(jnp.int32)
        rowid = jnp.where(n - b * T < L, n_batch, b)
        gate = jnp.zeros((TX, D_MODEL), F32)
        for r in range(n_batch + 1):
            gate = jnp.where(rowid == r, g_ref[r:r + 1, :], gate)
        o_ref[...] = x_ref[...] + gate * acc_ref[...]


def _moe_combine(work, pos2, yw, xf, gate2, n_batch, T, L):
    wt, ws, wf, wl, wv = work
    n, D = xf.shape
    return pl.pallas_call(
        functools.partial(_combine_kernel, n_batch=n_batch, T=T, L=L),
        grid_spec=pltpu.PrefetchScalarGridSpec(
            num_scalar_prefetch=5,
            grid=(wt.shape[0],),
            in_specs=[pl.BlockSpec((TX, 2), lambda w, wt, ws, wf, wl, wv: (wt[w], 0)),
                      pl.BlockSpec((TX, D), lambda w, wt, ws, wf, wl, wv: (ws[w], 0)),
                      pl.BlockSpec((TX, D), lambda w, wt, ws, wf, wl, wv: (wt[w], 0)),
                      pl.BlockSpec((SUBLANES, D), lambda w, wt, ws, wf, wl, wv: (0, 0))],
            out_specs=pl.BlockSpec((TX, D), lambda w, wt, ws, wf, wl, wv: (wt[w], 0)),
            scratch_shapes=[pltpu.VMEM((TX, D), F32)]),
        out_shape=jax.ShapeDtypeStruct((n, D), F32),
        compiler_params=_cparams(("arbitrary",)),
        name="moe_combine",
    )(wt, ws, wf, wl, wv, pos2, yw, xf, gate2)


def _count_below(sorted_vals, x):
    return jnp.sum((sorted_vals[None, :] < x[:, None]).astype(jnp.int32), axis=1)


def _with_flags(g, it, compute, n_work, total):
    live = jnp.arange(n_work, dtype=jnp.int32) < total
    g_prev = jnp.concatenate([jnp.full((1,), -1, jnp.int32), g[:-1]])
    g_next = jnp.concatenate([g[1:], jnp.full((1,), -1, jnp.int32)])
    live_next = jnp.concatenate([live[1:], jnp.zeros((1,), bool)])
    first = live & (g != g_prev)
    last = live & ((g != g_next) | ~live_next)
    i32 = lambda a: a.astype(jnp.int32)
    return i32(g), i32(it), i32(first), i32(last), i32(live & compute)


def _moe_plan(sel, n_tok):
    E = N_EXPERTS
    sel = sel.astype(jnp.int32)
    cnt = jnp.sum(sel, axis=0)
    rank = jnp.cumsum(sel, axis=0) - 1
    gsz = ((cnt + TX - 1) // TX) * TX
    gend = jnp.cumsum(gsz)
    goff = gend - gsz
    n_tiles = (2 * n_tok + E * (TX - 1) + TX - 1) // TX
    P = n_tiles * TX
    pos = jnp.where(sel == 1, goff[None, :] + rank, -1)
    tile_start = jnp.arange(n_tiles, dtype=jnp.int32) * TX
    tile_valid = tile_start < gend[-1]
    tile_expert = jnp.minimum(_count_below(gend, tile_start + 1), E - 1)
    pmax_tok = jnp.max(pos, axis=1)
    psec_tok = jnp.max(jnp.where(pos == pmax_tok[:, None], -1, pos), axis=1)
    pos2 = jnp.stack([psec_tok, pmax_tok], axis=1)

    n_src_tiles = n_tok // TX
    cin = rank[TX - 1::TX] + 1
    ra = tile_start - goff[tile_expert]
    rb = ra + jnp.clip(cnt[tile_expert] - ra, 0, TX) - 1
    cin_t = cin.T[tile_expert]
    lo = jnp.sum((cin_t <= ra[:, None]).astype(jnp.int32), axis=1)
    hi = jnp.sum((cin_t <= rb[:, None]).astype(jnp.int32), axis=1)
    span = jnp.where(tile_valid, hi - lo + 1, 1)
    n_gw = E * n_src_tiles + n_tiles
    cs = jnp.cumsum(span)
    total = cs[-1]
    w = jnp.minimum(jnp.arange(n_gw, dtype=jnp.int32), total - 1)
    jw = _count_below(cs, w + 1)
    sw = jnp.where(tile_valid[jw], lo[jw] + (w - (cs[jw] - span[jw])), 0)
    gwork = _with_flags(jw, sw, tile_valid[jw], n_gw, total)

    pt = pos.reshape(n_src_tiles, TX, E)
    pmax = jnp.max(pt, axis=1)
    pmin = jnp.min(jnp.where(pt >= 0, pt, P), axis=1)
    ta = pmin // TX
    tb = pmax // TX
    cand_item = jnp.stack([ta, tb], axis=-1).reshape(-1)
    cand_valid = jnp.stack([pmax >= 0, (pmax >= 0) & (tb != ta)], axis=-1).reshape(-1).astype(jnp.int32)
    cand_group = jnp.repeat(jnp.arange(n_src_tiles, dtype=jnp.int32), 2 * E)
    n_cw = E * n_src_tiles + n_tiles
    ccs = jnp.cumsum(cand_valid)
    ctotal = ccs[-1]
    cw = jnp.minimum(jnp.arange(n_cw, dtype=jnp.int32), ctotal - 1)
    cidx = _count_below(ccs, cw + 1)
    cwork = _with_flags(cand_group[cidx], cand_item[cidx], jnp.ones((n_cw,), bool), n_cw, ctotal)
    return pos, pos2, tile_expert, tile_valid.astype(jnp.int32), gwork, cwork, n_tiles


def _moe(h2, xs1, cmb, sel, gate2, w1, w3, w2, idx, L):
    B, T, D = xs1.shape
    n_tok = B * T
    E = N_EXPERTS
    assert n_tok % TX == 0
    selm = sel.reshape(n_tok, LANES)[:, :E] > 0.5
    cmbm = cmb.reshape(n_tok, LANES)[:, :E]
    pos, pos2, tile_expert, tile_valid, gwork, cwork, n_tiles = _moe_plan(selm, n_tok)
    pos_t = pos.T.reshape(E, n_tok // TX, 1, TX)
    cmb_t = cmbm.T.reshape(E, n_tok // TX, 1, TX)

    xg, pw = _moe_gather(gwork, tile_expert, pos_t, cmb_t, h2.reshape(n_tok, D), n_tiles)
    yw = _moe_gffn(tile_expert, tile_valid, xg, pw, w1, w3, w2, idx)
    out = _moe_combine(cwork, pos2, yw, xs1.reshape(n_tok, D), gate2, B, T, L)
    return out.reshape(B, T, D)


def _final_kernel(x_ref, g_ref, o_ref):
    x = x_ref[0]
    o_ref[0] = x * lax.rsqrt(jnp.mean(x * x, axis=-1, keepdims=True) + EPS) * g_ref[...]


def _final_norm(xs, g, nct, S):
    B, T, D = xs.shape
    return pl.pallas_call(
        _final_kernel,
        grid=(B, S // TM),
        in_specs=[pl.BlockSpec((1, TM, D), lambda b, i: (b, i + nct, 0)), _const_spec((1, D))],
        out_specs=pl.BlockSpec((1, TM, D), lambda b, i: (b, i, 0)),
        out_shape=jax.ShapeDtypeStruct((B, S, D), F32),
        compiler_params=_cparams(("arbitrary", "arbitrary")),
        name="final_norm",
    )(xs, g)


def _rope_tables(L, S):
    t = np.arange(S)
    pos = np.stack([t // GRID_W, t % GRID_W], axis=-1).astype(np.float32)
    inv_freq = (ROPE_THETA ** (-np.arange(0, AXIS_ROPE_DIM, 2, dtype=np.float32) / AXIS_ROPE_DIM)).astype(np.float32)
    ang = pos[:, :, None] * inv_freq[None, None, :]
    cos, sin = np.cos(ang), np.sin(ang)
    cos64 = np.concatenate([cos[:, 0], cos[:, 0], cos[:, 1], cos[:, 1]], axis=-1)
    sin64 = np.concatenate([-sin[:, 0], sin[:, 0], -sin[:, 1], sin[:, 1]], axis=-1)
    cos_t = np.concatenate([np.ones((L, HEAD_DIM), np.float32), cos64], axis=0)
    sin_t = np.concatenate([np.zeros((L, HEAD_DIM), np.float32), sin64], axis=0)
    rep = LANES // HEAD_DIM
    return jnp.asarray(np.tile(cos_t, (1, rep)), F32), jnp.asarray(np.tile(sin_t, (1, rep)), F32)


def _na_bias_table(rel_bias):
    col = np.arange(GRID_W)
    col_start = np.clip(col - NA_WIN_COLS // 2, 0, GRID_W - NA_WIN_COLS)
    kc = np.arange(GRID_W)
    inside = (kc[None, :] >= col_start[:, None]) & (kc[None, :] < col_start[:, None] + NA_WIN_COLS)
    dc = np.clip(kc[None, :] - col[:, None] + (NA_WIN_COLS - 1), 0, 2 * NA_WIN_COLS - 2)
    case = np.arange(NA_WIN_ROWS)
    dr = np.arange(NA_WIN_ROWS)[None, :] - case[:, None] + (NA_WIN_ROWS - 1)
    t = rel_bias[:, dr]
    t = t[:, :, :, dc]
    t = jnp.where(jnp.asarray(inside)[None, None, None], t, MASK_VALUE)
    t = t.transpose(1, 0, 3, 2, 4)
    return t.reshape(NA_WIN_ROWS, B_HEADS, GRID_W, NA_WIN_ROWS * GRID_W).astype(F32)


def _block_diag_ones(n, blk):
    idx = np.arange(n) // blk
    return jnp.asarray((idx[:, None] == idx[None, :]).astype(np.float32), BF16)


def _pool_block_diag(pool_w):
    depth, g, c, d = pool_w.shape
    eye = jnp.asarray(np.eye(g, dtype=np.float32))
    out = pool_w[:, :, :, None, :] * eye[None, :, None, :, None]
    return out.reshape(depth, g * c, g * d).astype(BF16)


def kernel(x, c, ctx, c_ctx, w_mod, b_mod, norm1_g, norm2_g, w_in, q_norm_g, k_norm_g, na_rel_bias, pool_w,
           pool_scale, w_branch_a, w_branch_b, w_branch_c, w_out, ffn_w1, ffn_w3, ffn_w2, router_w, router_b,
           moe_w1, moe_w3, moe_w2, final_g):
    B, S, D = x.shape
    L = ctx.shape[1]
    T = L + S
    depth = w_mod.shape[0]
    assert D == D_MODEL and L % TM == 0 and S % TM == 0 and TM % GRID_W == 0 and B + 1 <= SUBLANES
    assert S // GRID_W >= NA_WIN_ROWS and TM // GRID_W <= NA_WIN_ROWS // 2 + 1
    nct = L // TM

    cvec = jnp.zeros((SUBLANES, D), F32).at[:B].set(c).at[B].set(c_ctx)
    mod = _mod_vectors(cvec, w_mod, b_mod)
    cos_t, sin_t = _rope_tables(L, S)
    bd = _block_diag_ones(QK_W, HEAD_DIM)
    bf = lambda a: a.astype(BF16)
    w_qkv, w_gate = bf(w_in[:, :, :OFF_G]), bf(w_in[:, :, OFF_G:])
    wpa, wpb, wpc, wo, pbd = bf(w_branch_a), bf(w_branch_b), bf(w_branch_c), bf(w_out), _pool_block_diag(pool_w)
    fw1, fw3, fw2 = bf(ffn_w1), bf(ffn_w3), bf(ffn_w2)
    mw1, mw3, mw2 = bf(moe_w1), bf(moe_w3), bf(moe_w2)

    xs = jnp.concatenate([ctx, x], axis=1)
    for layer in range(depth):
        modr = mod[layer].reshape(SUBLANES, 6, D)
        g1 = norm1_g[layer].reshape(1, D)
        gqk = jnp.concatenate([jnp.tile(q_norm_g[layer], A_HEADS), jnp.tile(k_norm_g[layer], A_KV_HEADS)])
        q, kt, v, bq, bk, bv, cu = _inproj(xs, modr, g1, w_qkv, layer, bd, gqk.reshape(1, QK_W), cos_t, sin_t, nct)
        ya = _attn_a(q, kt, v, nct)
        yb = _attn_b(bq, bk, bv, _na_bias_table(na_rel_bias[layer]), nct, L)
        is_moe = layer % 2 == 1
        i = layer // 2
        router = None
        if is_moe:
            rw = jnp.zeros((D, LANES), F32).at[:, :N_EXPERTS].set(router_w[i])
            rw_hi = rw.astype(BF16)
            rw_lo = (rw - rw_hi.astype(F32)).astype(BF16)
            rb = jnp.zeros((1, LANES), F32).at[0, :N_EXPERTS].set(router_b[i])
            router = (rw_hi, rw_lo, rb)
        outs = _merge(ya, yb, cu, xs, modr, g1, w_gate, wpa, wpb, wpc, pbd, pool_scale[layer].reshape(1, C_WIDTH),
                      wo, norm2_g[layer].reshape(1, D), layer, router, nct, L)
        if is_moe:
            xs1, h2, cmb, sel = outs
            xs = _moe(h2, xs1, cmb, sel, modr[:, 5, :], mw1, mw3, mw2, i, L)
        else:
            xs1, h2 = outs
            xs = _ffn_dense(h2, xs1, modr[:, 5:6, :], fw1, fw3, fw2, i, nct)
    return _final_norm(xs, final_g.reshape(1, D), nct, S)
```

```python
import functools

import numpy as np
import jax
import jax.numpy as jnp
from jax import lax
from jax.experimental import pallas as pl
from jax.experimental.pallas import tpu as pltpu

F32 = jnp.float32
BF16 = jnp.bfloat16

D_MODEL = 1024
DEPTH = 4
GRID_W = 64
HEAD_DIM = 64
A_HEADS = 8
A_KV_HEADS = 2
A_GROUP = A_HEADS // A_KV_HEADS
B_HEADS = 4
C_GROUPS = 4
C_GROUP_DIM = 64
C_WIDTH = C_GROUPS * C_GROUP_DIM
POOL_WINDOWS = (2, 4, 8, 16)
NA_WIN_ROWS = 8
NA_WIN_COLS = 16
ROPE_THETA = 10000.0
AXIS_ROPE_DIM = HEAD_DIM // 2
N_EXPERTS = 8
EPS = 1e-6

A_Q = A_HEADS * HEAD_DIM
A_KV = A_KV_HEADS * HEAD_DIM
B_W = B_HEADS * HEAD_DIM
QK_W = A_Q + A_KV
OFF_AV = QK_W
OFF_BQ = OFF_AV + A_KV
OFF_BK = OFF_BQ + B_W
OFF_BV = OFF_BK + B_W
OFF_CU = OFF_BV + B_W
OFF_G = OFF_CU + C_WIDTH

LANES = 128
SUBLANES = 8
TM = 256
TX = 512
FC = 512
A_KBLK = 8
A_TQ = 256
LOG2E = 1.4426950408889634
POOL_HALO = 8
MASK_VALUE = -1e30
VMEM_LIMIT = 56 * 1024 * 1024


def _cparams(sem):
    return pltpu.CompilerParams(dimension_semantics=sem, vmem_limit_bytes=VMEM_LIMIT)


def _dot(a, b):
    return jnp.dot(a, b, preferred_element_type=F32)


def _dot_nt(a, b):
    return lax.dot_general(a, b, (((1,), (1,)), ((), ())), preferred_element_type=F32)


def _split(a):
    hi = a.astype(BF16)
    lo = (a - hi.astype(F32)).astype(BF16)
    return hi, lo


def _dot3(a, b_hi, b_lo):
    a_hi, a_lo = _split(a)
    return _dot(a_hi, b_hi) + (_dot(a_lo, b_hi) + _dot(a_hi, b_lo))


def _const_spec(shape):
    n = len(shape)
    return pl.BlockSpec(shape, lambda *_: (0,) * n, pipeline_mode=pl.Buffered(1))


def _layer_spec(shape, layer):
    n = len(shape)
    return pl.BlockSpec((1,) + tuple(shape), lambda *_: (layer,) + (0,) * n, pipeline_mode=pl.Buffered(1))


def _rms_mod(x, g, shift, scale):
    y = x * lax.rsqrt(jnp.mean(x * x, axis=-1, keepdims=True) + EPS) * g
    return y * (1.0 + scale) + shift


def _mod_kernel(c_ref, w_ref, b_ref, o_ref):
    c = c_ref[...]
    s = c * jax.nn.sigmoid(c)
    w_hi, w_lo = _split(w_ref[0])
    o_ref[0] = _dot3(s, w_hi, w_lo) + b_ref[0]


def _mod_vectors(cvec, w_mod, b_mod):
    depth, d, n = w_mod.shape
    tn = 1536
    return pl.pallas_call(
        _mod_kernel,
        grid=(depth, n // tn),
        in_specs=[pl.BlockSpec((SUBLANES, d), lambda l, j: (0, 0)),
                  pl.BlockSpec((1, d, tn), lambda l, j: (l, 0, j)),
                  pl.BlockSpec((1, 1, tn), lambda l, j: (l, 0, j))],
        out_specs=pl.BlockSpec((1, SUBLANES, tn), lambda l, j: (l, 0, j)),
        out_shape=jax.ShapeDtypeStruct((depth, SUBLANES, n), F32),
        compiler_params=_cparams(("arbitrary", "arbitrary")),
        name="mod_vectors",
    )(cvec, w_mod, b_mod.reshape(depth, 1, n))


def _swap16(x):
    lane = lax.broadcasted_iota(jnp.int32, (1, LANES), 1)
    first = (lane % 32) < 16
    return jnp.where(first, pltpu.roll(x, LANES - 16, 1), pltpu.roll(x, 16, 1))


def _inproj_kernel(x_ref, mod_ref, g_ref, w_ref, bd_ref, gqk_ref, cos_ref, sin_ref,
                   q_ref, kt_ref, v_ref, bq_ref, bk_ref, bv_ref, cu_ref):
    m = mod_ref[0]
    h = _rms_mod(x_ref[0], g_ref[...], m[0:1], m[1:2])
    px = _dot(h.astype(BF16), w_ref[0])

    qk = px[:, :QK_W]
    sq_hi, sq_lo = _split(qk * qk)
    ss = _dot(sq_hi, bd_ref[...]) + _dot(sq_lo, bd_ref[...])
    qn = qk * lax.rsqrt(ss * (1.0 / HEAD_DIM) + EPS) * gqk_ref[...]
    cos = cos_ref[...]
    sin = sin_ref[...]
    chunks = []
    for j in range(QK_W // LANES):
        c = qn[:, j * LANES:(j + 1) * LANES]
        chunks.append(c * cos + _swap16(c) * sin)
    scale = HEAD_DIM ** -0.5
    for j in range(A_Q // LANES):
        q_ref[0, :, j * LANES:(j + 1) * LANES] = (chunks[j] * (scale * LOG2E)).astype(BF16)
    kt_ref[0, 0] = chunks[A_Q // LANES].T.astype(BF16)

    ones = jnp.ones((TM, LANES - HEAD_DIM), BF16)
    for j in range(A_KV_HEADS):
        vj = px[:, OFF_AV + j * HEAD_DIM:OFF_AV + (j + 1) * HEAD_DIM].astype(BF16)
        v_ref[0, :, j * LANES:(j + 1) * LANES] = jnp.concatenate([vj, ones], axis=1)
    bq_ref[0] = (px[:, OFF_BQ:OFF_BK] * (scale * LOG2E)).astype(BF16)
    bk_ref[0] = px[:, OFF_BK:OFF_BV].astype(BF16)
    for h in range(B_HEADS):
        vh = px[:, OFF_BV + h * HEAD_DIM:OFF_BV + (h + 1) * HEAD_DIM].astype(BF16)
        bv_ref[0, :, h * LANES:(h + 1) * LANES] = jnp.concatenate([vh, ones], axis=1)
    cu_ref[0] = px[:, OFF_CU:OFF_G]


def _inproj(xs, modr, g1, w_qkv, layer, bd, gqk, cos_t, sin_t, nct):
    B, T, D = xs.shape
    nt = T // TM
    tok = lambda w: pl.BlockSpec((1, TM, w), lambda b, i: (b, i, 0))
    out_shape = (
        jax.ShapeDtypeStruct((B, T, A_Q), BF16),
        jax.ShapeDtypeStruct((B, nt, A_KV, TM), BF16),
        jax.ShapeDtypeStruct((B, T, A_KV_HEADS * LANES), BF16),
        jax.ShapeDtypeStruct((B, T, B_W), BF16),
        jax.ShapeDtypeStruct((B, T, B_W), BF16),
        jax.ShapeDtypeStruct((B, T, B_HEADS * LANES), BF16),
        jax.ShapeDtypeStruct((B, T, C_WIDTH), F32),
    )
    return pl.pallas_call(
        _inproj_kernel,
        grid=(B, nt),
        in_specs=[tok(D),
                  pl.BlockSpec((1, 6, D), lambda b, i: (jnp.where(i < nct, B, b), 0, 0)),
                  _const_spec((1, D)),
                  _layer_spec((D, OFF_G), layer),
                  _const_spec((QK_W, QK_W)),
                  _const_spec((1, QK_W)),
                  pl.BlockSpec((TM, LANES), lambda b, i: (i, 0)),
                  pl.BlockSpec((TM, LANES), lambda b, i: (i, 0))],
        out_specs=(tok(A_Q),
                   pl.BlockSpec((1, 1, A_KV, TM), lambda b, i: (b, i, 0, 0)),
                   tok(A_KV_HEADS * LANES), tok(B_W), tok(B_W), tok(B_HEADS * LANES), tok(C_WIDTH)),
        out_shape=out_shape,
        compiler_params=_cparams(("arbitrary", "arbitrary")),
        name="inproj",
    )(xs, modr, g1, w_qkv, bd, gqk, cos_t, sin_t)


def _attn_a_kernel(q_ref, kt_ref, v_ref, o_ref, *, nct, nct_q, n_steps):
    i = pl.program_id(1)
    rows = A_GROUP * A_TQ

    def step(carry, q4, j, blk0, nblk):
        m, acc = carry
        s_list = [_dot(q4, kt_ref[0, blk0 + c, j * HEAD_DIM:(j + 1) * HEAD_DIM, :]) for c in range(nblk)]
        smax = s_list[0]
        for s in s_list[1:]:
            smax = jnp.maximum(smax, s)
        m_new = jnp.maximum(m, jnp.max(smax, axis=-1, keepdims=True))
        alpha = jnp.exp2(m - m_new)
        p = jnp.concatenate([jnp.exp2((s - m_new).astype(BF16)) for s in s_list], axis=1)
        vv = v_ref[0, pl.ds(pl.multiple_of(blk0 * TM, TM), nblk * TM), j * LANES:(j + 1) * LANES]
        return m_new, alpha * acc + _dot(p, vv)

    def run(n_main):
        for j in range(A_KV_HEADS):
            base = j * A_GROUP * HEAD_DIM
            q4 = jnp.concatenate(
                [q_ref[0, :, base + g * HEAD_DIM: base + (g + 1) * HEAD_DIM] for g in range(A_GROUP)], axis=0)
            carry = (jnp.full((rows, 1), MASK_VALUE, F32), jnp.zeros((rows, LANES), F32))
            carry = step(carry, q4, j, 0, nct)
            if n_main:
                carry = lax.fori_loop(
                    0, n_main, lambda k, cr: step(cr, q4, j, nct + k * A_KBLK, A_KBLK), carry)
            acc = carry[1]
            o = acc[:, 0:HEAD_DIM] / acc[:, HEAD_DIM:HEAD_DIM + 1]
            for g in range(A_GROUP):
                o_ref[0, :, base + g * HEAD_DIM: base + (g + 1) * HEAD_DIM] = o[g * A_TQ:(g + 1) * A_TQ].astype(BF16)

    @pl.when(i < nct_q)
    def _():
        run(0)

    @pl.when(i >= nct_q)
    def _():
        run(n_steps)


def _attn_a(q, kt, v, nct):
    B, T, _ = q.shape
    nt = T // TM
    assert (nt - nct) % A_KBLK == 0
    return pl.pallas_call(
        functools.partial(_attn_a_kernel, nct=nct, nct_q=nct * (TM // A_TQ), n_steps=(nt - nct) // A_KBLK),
        grid=(B, T // A_TQ),
        in_specs=[pl.BlockSpec((1, A_TQ, A_Q), lambda b, i: (b, i, 0)),
                  pl.BlockSpec((1, nt, A_KV, TM), lambda b, i: (b, 0, 0, 0)),
                  pl.BlockSpec((1, T, A_KV_HEADS * LANES), lambda b, i: (b, 0, 0))],
        out_specs=pl.BlockSpec((1, A_TQ, A_Q), lambda b, i: (b, i, 0)),
        out_shape=jax.ShapeDtypeStruct((B, T, A_Q), BF16),
        compiler_params=_cparams(("arbitrary", "arbitrary")),
        name="attn_a",
    )(q, kt, v)


def _attn_b_kernel(q_ref, kp_ref, kc_ref, kn_ref, vp_ref, vc_ref, vn_ref, kx_ref, vx_ref, bias_ref,
                   o_ref, kbuf, vbuf, *, nct, grid_rows):
    i = pl.program_id(1)
    rpt = TM // GRID_W
    win = NA_WIN_ROWS * GRID_W

    def normalised(o_ext):
        return (o_ext[:, 0:HEAD_DIM] / o_ext[:, HEAD_DIM:HEAD_DIM + 1]).astype(BF16)

    @pl.when(i < nct)
    def _():
        for h in range(B_HEADS):
            sl = slice(h * HEAD_DIM, (h + 1) * HEAD_DIM)
            vl = slice(h * LANES, (h + 1) * LANES)
            s = _dot_nt(q_ref[0, :, sl], kx_ref[0, :, sl])
            m = jnp.max(s, axis=-1, keepdims=True)
            p = jnp.exp2((s - m).astype(BF16))
            o_ref[0, :, sl] = normalised(_dot(p, vx_ref[0, :, vl]))

    @pl.when(i >= nct)
    def _():
        kbuf[0:TM] = kp_ref[0]
        kbuf[TM:2 * TM] = kc_ref[0]
        kbuf[2 * TM:3 * TM] = kn_ref[0]
        vbuf[0:TM] = vp_ref[0]
        vbuf[TM:2 * TM] = vc_ref[0]
        vbuf[2 * TM:3 * TM] = vn_ref[0]
        r0 = (i - nct) * rpt

        for a in range(rpt):
            qr = r0 + a
            rs = jnp.clip(qr - NA_WIN_ROWS // 2, 0, grid_rows - NA_WIN_ROWS)
            case = qr - rs
            start = pl.multiple_of((rs - r0 + rpt) * GRID_W, GRID_W)
            kw = kbuf[pl.ds(start, win), :]
            vw = vbuf[pl.ds(start, win), :]
            qrow = q_ref[0, a * GRID_W:(a + 1) * GRID_W, :]
            for h in range(B_HEADS):
                sl = slice(h * HEAD_DIM, (h + 1) * HEAD_DIM)
                vl = slice(h * LANES, (h + 1) * LANES)
                s_w = _dot_nt(qrow[:, sl], kw[:, sl]) + bias_ref[0, case, h]
                s_c = _dot_nt(qrow[:, sl], kx_ref[0, :, sl])
                m = jnp.maximum(jnp.max(s_w, axis=-1, keepdims=True), jnp.max(s_c, axis=-1, keepdims=True))
                p_w = jnp.exp2((s_w - m).astype(BF16))
                p_c = jnp.exp2((s_c - m).astype(BF16))
                o_ref[0, a * GRID_W:(a + 1) * GRID_W, sl] = normalised(
                    _dot(p_w, vw[:, vl]) + _dot(p_c, vx_ref[0, :, vl]))


def _attn_b(bq, bk, bv, bias_t, layer, nct, L):
    B, T, _ = bq.shape
    nt = T // TM
    grid_rows = (T - L) // GRID_W
    VW = B_HEADS * LANES
    cur = lambda b, i: (b, i, 0)
    prev = lambda b, i: (b, jnp.maximum(i - 1, nct), 0)
    nxt = lambda b, i: (b, jnp.minimum(i + 1, nt - 1), 0)
    blk = lambda f: pl.BlockSpec((1, TM, B_W), f)
    vblk = lambda f: pl.BlockSpec((1, TM, VW), f)
    ctx = lambda w: pl.BlockSpec((1, L, w), lambda b, i: (b, 0, 0))
    return pl.pallas_call(
        functools.partial(_attn_b_kernel, nct=nct, grid_rows=grid_rows),
        grid=(B, nt),
        in_specs=[blk(cur), blk(prev), blk(cur), blk(nxt), vblk(prev), vblk(cur), vblk(nxt), ctx(B_W), ctx(VW),
                  _layer_spec(bias_t.shape[1:], layer)],
        out_specs=blk(cur),
        out_shape=jax.ShapeDtypeStruct((B, T, B_W), BF16),
        scratch_shapes=[pltpu.VMEM((3 * TM, B_W), BF16), pltpu.VMEM((3 * TM, VW), BF16)],
        compiler_params=_cparams(("arbitrary", "arbitrary")),
        name="attn_b",
    )(bq, bk, bk, bk, bv, bv, bv, bk, bv, bias_t)


def _merge_kernel(*refs, nct, nt, L, S, with_router):
    if with_router:
        (ya_ref, yb_ref, cup_ref, cu_ref, cun_ref, x_ref, mod_ref, g1_ref, wg_ref, wpa_ref, wpb_ref, wpc_ref,
         pbd_ref, psc_ref, wo_ref, g2_ref, rwh_ref, rwl_ref, rb_ref,
         xo_ref, h_ref, cmb_ref, sel_ref, e_scr) = refs
    else:
        (ya_ref, yb_ref, cup_ref, cu_ref, cun_ref, x_ref, mod_ref, g1_ref, wg_ref, wpa_ref, wpb_ref, wpc_ref,
         pbd_ref, psc_ref, wo_ref, g2_ref, xo_ref, h_ref, e_scr) = refs
    i = pl.program_id(1)
    D = D_MODEL
    m = mod_ref[0]
    x_in = x_ref[0]

    at_start = (i == 0) | (i == nct)
    at_end = (i == nct - 1) | (i == nt - 1)
    u = cu_ref[0]
    e_scr[0:POOL_HALO] = jnp.where(at_start, 0.0, cup_ref[0])
    e_scr[POOL_HALO:POOL_HALO + TM] = u
    e_scr[POOL_HALO + TM:] = jnp.where(at_end, 0.0, cun_ref[0])
    sh = lambda k: e_scr[POOL_HALO + k:POOL_HALO + k + TM]
    w2 = sh(-1) + sh(0)
    w4 = w2 + (sh(-2) + sh(1))
    w8 = w4 + ((sh(-4) + sh(-3)) + (sh(2) + sh(3)))
    w16 = w8 + (((sh(-8) + sh(-7)) + (sh(-6) + sh(-5))) + ((sh(4) + sh(5)) + (sh(6) + sh(7))))
    grp = lax.broadcasted_iota(jnp.int32, (1, C_WIDTH), 1) // C_GROUP_DIM
    half = jnp.left_shift(1, grp)
    t_loc = lax.broadcasted_iota(jnp.int32, (TM, 1), 0)
    t_seq = jnp.where(i < nct, i * TM, (i - nct) * TM) + t_loc
    n_seq = jnp.where(i < nct, L, S)
    cnt = jnp.minimum(t_seq + half, n_seq) - jnp.maximum(t_seq - half, 0)
    wsum = jnp.where(grp == 0, w2, jnp.where(grp == 1, w4, jnp.where(grp == 2, w8, w16)))
    dlt = wsum / cnt.astype(F32) - u
    yc = _dot(dlt.astype(BF16), pbd_ref[0]) * psc_ref[...]

    h1 = _rms_mod(x_in, g1_ref[...], m[0:1], m[1:2]).astype(BF16)
    branches = (ya_ref[0], yb_ref[0], yc.astype(BF16))
    weights = (wpa_ref, wpb_ref, wpc_ref)
    mrg = None
    for k in range(3):
        gate = jax.nn.sigmoid(_dot(h1, wg_ref[0, :, k * D:(k + 1) * D]))
        term = gate * _dot(branches[k], weights[k][0])
        mrg = term if mrg is None else mrg + term
    x = x_in + m[2:3] * _dot(mrg.astype(BF16), wo_ref[0])
    xo_ref[0] = x
    h = _rms_mod(x, g2_ref[...], m[3:4], m[4:5])
    h_ref[0] = h.astype(BF16)

    if with_router:
        lane = lax.broadcasted_iota(jnp.int32, (1, LANES), 1).astype(F32)
        lg = _dot3(h, rwh_ref[...], rwl_ref[...]) + rb_ref[...]
        lg = jnp.where(lane < N_EXPERTS, lg, -jnp.inf)
        m1 = jnp.max(lg, axis=-1, keepdims=True)
        i1 = jnp.min(jnp.where(lg == m1, lane, float(LANES)), axis=-1, keepdims=True)
        mask1 = lane == i1
        lg2 = jnp.where(mask1, -jnp.inf, lg)
        m2 = jnp.max(lg2, axis=-1, keepdims=True)
        i2 = jnp.min(jnp.where(lg2 == m2, lane, float(LANES)), axis=-1, keepdims=True)
        mask2 = lane == i2
        e2 = jnp.exp(m2 - m1)
        den = 1.0 + e2
        cmb_ref[0] = jnp.where(mask1, 1.0 / den, 0.0) + jnp.where(mask2, e2 / den, 0.0)
        sel_ref[0] = jnp.where(mask1 | mask2, 1.0, 0.0)


def _merge(ya, yb, cu, xs, modr, g1, w_gate, wpa, wpb, wpc, pbd, psc, wo, g2, layer, router, nct, L):
    B, T, D = xs.shape
    nt = T // TM
    S = T - L
    hb = TM // POOL_HALO
    tok = lambda w: pl.BlockSpec((1, TM, w), lambda b, i: (b, i, 0))
    in_specs = [tok(A_Q), tok(B_W),
                pl.BlockSpec((1, POOL_HALO, C_WIDTH), lambda b, i: (b, jnp.maximum(i * hb - 1, 0), 0)),
                tok(C_WIDTH),
                pl.BlockSpec((1, POOL_HALO, C_WIDTH), lambda b, i: (b, jnp.minimum((i + 1) * hb, nt * hb - 1), 0)),
                tok(D),
                pl.BlockSpec((1, 6, D), lambda b, i: (jnp.where(i < nct, B, b), 0, 0)),
                _const_spec((1, D)),
                _layer_spec((D, 3 * D), layer),
                _layer_spec((A_Q, D), layer), _layer_spec((B_W, D), layer), _layer_spec((C_WIDTH, D), layer),
                _layer_spec((C_WIDTH, C_WIDTH), layer), _const_spec((1, C_WIDTH)), _layer_spec((D, D), layer),
                _const_spec((1, D))]
    args = [ya, yb, cu, cu, cu, xs, modr, g1, w_gate, wpa, wpb, wpc, pbd, psc, wo, g2]
    out_specs = [tok(D), tok(D)]
    out_shape = [jax.ShapeDtypeStruct((B, T, D), F32), jax.ShapeDtypeStruct((B, T, D), BF16)]
    if router is not None:
        in_specs += [_const_spec((D, LANES)), _const_spec((D, LANES)), _const_spec((1, LANES))]
        args += list(router)
        out_specs += [tok(LANES), tok(LANES)]
        out_shape += [jax.ShapeDtypeStruct((B, T, LANES), F32)] * 2
    return pl.pallas_call(
        functools.partial(_merge_kernel, nct=nct, nt=nt, L=L, S=S, with_router=router is not None),
        grid=(B, nt),
        in_specs=in_specs,
        out_specs=tuple(out_specs),
        out_shape=tuple(out_shape),
        scratch_shapes=[pltpu.VMEM((TM + 2 * POOL_HALO, C_WIDTH), F32)],
        compiler_params=_cparams(("arbitrary", "arbitrary")),
        name="merge_router" if router is not None else "merge",
    )(*args)


def _swiglu_acc(h, w1_ref, w3_ref, w2_ref, lead):
    d_ff = w1_ref.shape[-1]
    acc = None
    for f0 in range(0, d_ff, FC):
        f1 = min(f0 + FC, d_ff)
        a = _dot(h, w1_ref[lead + (slice(None), slice(f0, f1))])
        b = _dot(h, w3_ref[lead + (slice(None), slice(f0, f1))])
        t = (a * jax.nn.sigmoid(a) * b).astype(BF16)
        part = _dot(t, w2_ref[lead + (slice(f0, f1), slice(None))])
        acc = part if acc is None else acc + part
    return acc


def _ffn_kernel(h_ref, x_ref, g_ref, w1_ref, w3_ref, w2_ref, o_ref):
    f = _swiglu_acc(h_ref[0], w1_ref, w3_ref, w2_ref, (0,))
    o_ref[0] = x_ref[0] + g_ref[0] * f


def _ffn_dense(h2, xs, gate2, w1, w3, w2, idx, nct):
    B, T, D = xs.shape
    F = w1.shape[-1]
    nt = T // TM
    tok = lambda: pl.BlockSpec((1, TM, D), lambda b, i: (b, i, 0))
    return pl.pallas_call(
        _ffn_kernel,
        grid=(B, nt),
        in_specs=[tok(), tok(),
                  pl.BlockSpec((1, 1, D), lambda b, i: (jnp.where(i < nct, B, b), 0, 0)),
                  _layer_spec((D, F), idx), _layer_spec((D, F), idx), _layer_spec((F, D), idx)],
        out_specs=tok(),
        out_shape=jax.ShapeDtypeStruct((B, T, D), F32),
        compiler_params=_cparams(("arbitrary", "arbitrary")),
        name="ffn_dense",
    )(h2, xs, gate2, w1, w3, w2)


def _gather_kernel(wj_ref, ws_ref, wf_ref, wl_ref, wv_ref, te_ref, pos_ref, cmb_ref, h_ref,
                   o_ref, pw_ref, acc_ref, accw_ref):
    w = pl.program_id(0)

    @pl.when(wf_ref[w] == 1)
    def _():
        acc_ref[...] = jnp.zeros_like(acc_ref)
        accw_ref[...] = jnp.zeros_like(accw_ref)

    @pl.when(wv_ref[w] == 1)
    def _():
        row = lax.broadcasted_iota(jnp.int32, (TX, 1), 0) + wj_ref[w] * TX
        hit = pos_ref[0, 0] == row
        acc_ref[...] += _dot(jnp.where(hit, 1.0, 0.0).astype(BF16), h_ref[...])
        accw_ref[...] += jnp.sum(jnp.where(hit, cmb_ref[0, 0], 0.0), axis=-1, keepdims=True)

    @pl.when(wl_ref[w] == 1)
    def _():
        o_ref[...] = acc_ref[...].astype(BF16)
        pw_ref[...] = accw_ref[...]


def _moe_gather(work, tile_expert, pos_t, cmb_t, h2f, n_tiles):
    wj, ws, wf, wl, wv = work
    n, D = h2f.shape
    row_spec = pl.BlockSpec((1, 1, 1, TX), lambda w, wj, ws, wf, wl, wv, te: (te[wj[w]], ws[w], 0, 0))
    return pl.pallas_call(
        _gather_kernel,
        grid_spec=pltpu.PrefetchScalarGridSpec(
            num_scalar_prefetch=6,
            grid=(wj.shape[0],),
            in_specs=[row_spec, row_spec,
                      pl.BlockSpec((TX, D), lambda w, wj, ws, wf, wl, wv, te: (ws[w], 0))],
            out_specs=(pl.BlockSpec((TX, D), lambda w, wj, ws, wf, wl, wv, te: (wj[w], 0)),
                       pl.BlockSpec((TX, 1), lambda w, wj, ws, wf, wl, wv, te: (wj[w], 0))),
            scratch_shapes=[pltpu.VMEM((TX, D), F32), pltpu.VMEM((TX, 1), F32)]),
        out_shape=(jax.ShapeDtypeStruct((n_tiles * TX, D), BF16),
                   jax.ShapeDtypeStruct((n_tiles * TX, 1), F32)),
        compiler_params=_cparams(("arbitrary",)),
        name="moe_gather",
    )(wj, ws, wf, wl, wv, tile_expert, pos_t, cmb_t, h2f)


def _gffn_kernel(te_ref, tv_ref, x_ref, pw_ref, w1_ref, w3_ref, w2_ref, o_ref):
    j = pl.program_id(0)

    @pl.when(tv_ref[j] == 1)
    def _():
        f = _swiglu_acc(x_ref[...], w1_ref, w3_ref, w2_ref, (0, 0))
        o_ref[...] = (f * pw_ref[...]).astype(BF16)

    @pl.when(tv_ref[j] == 0)
    def _():
        o_ref[...] = jnp.zeros_like(o_ref)


def _moe_gffn(tile_expert, tile_valid, xg, pw, w1, w3, w2, idx):
    P, D = xg.shape
    F = w1.shape[-1]
    wspec = lambda shape: pl.BlockSpec((1, 1) + shape, lambda j, te, tv: (idx, te[j], 0, 0),
                                       pipeline_mode=pl.Buffered(1))
    return pl.pallas_call(
        _gffn_kernel,
        grid_spec=pltpu.PrefetchScalarGridSpec(
            num_scalar_prefetch=2,
            grid=(P // TX,),
            in_specs=[pl.BlockSpec((TX, D), lambda j, te, tv: (j, 0)),
                      pl.BlockSpec((TX, 1), lambda j, te, tv: (j, 0)),
                      wspec((D, F)), wspec((D, F)), wspec((F, D))],
            out_specs=pl.BlockSpec((TX, D), lambda j, te, tv: (j, 0))),
        out_shape=jax.ShapeDtypeStruct((P, D), BF16),
        compiler_params=_cparams(("arbitrary",)),
        name="moe_gffn",
    )(tile_expert, tile_valid, xg, pw, w1, w3, w2)


def _combine_kernel(wt_ref, ws_ref, wf_ref, wl_ref, wv_ref, pos_ref, y_ref, x_ref, g_ref, o_ref, acc_ref,
                    *, n_batch, T, L):
    w = pl.program_id(0)

    @pl.when(wf_ref[w] == 1)
    def _():
        acc_ref[...] = jnp.zeros_like(acc_ref)

    @pl.when(wv_ref[w] == 1)
    def _():
        col = lax.broadcasted_iota(jnp.int32, (1, TX), 1) + ws_ref[w] * TX
        pos = pos_ref[...]
        hit = (pos[:, 0:1] == col) | (pos[:, 1:2] == col)
        acc_ref[...] += _dot(jnp.where(hit, 1.0, 0.0).astype(BF16), y_ref[...])

    @pl.when(wl_ref[w] == 1)
    def _():
        n = lax.broadcasted_iota(jnp.int32, (TX, 1), 0) + wt_ref[w] * TX
        b = jnp.zeros((TX, 1), jnp.int32)
        for k in range(1, n_batch):
            b = b + (n >= k * T).astype(jnp.int32)
        rowid = jnp.where(n - b * T < L, n_batch, b)
        gate = jnp.zeros((TX, D_MODEL), F32)
        for r in range(n_batch + 1):
            gate = jnp.where(rowid == r, g_ref[r:r + 1, :], gate)
        o_ref[...] = x_ref[...] + gate * acc_ref[...]


def _moe_combine(work, pos2, yw, xf, gate2, n_batch, T, L):
    wt, ws, wf, wl, wv = work
    n, D = xf.shape
    return pl.pallas_call(
        functools.partial(_combine_kernel, n_batch=n_batch, T=T, L=L),
        grid_spec=pltpu.PrefetchScalarGridSpec(
            num_scalar_prefetch=5,
            grid=(wt.shape[0],),
            in_specs=[pl.BlockSpec((TX, 2), lambda w, wt, ws, wf, wl, wv: (wt[w], 0)),
                      pl.BlockSpec((TX, D), lambda w, wt, ws, wf, wl, wv: (ws[w], 0)),
                      pl.BlockSpec((TX, D), lambda w, wt, ws, wf, wl, wv: (wt[w], 0)),
                      pl.BlockSpec((SUBLANES, D), lambda w, wt, ws, wf, wl, wv: (0, 0))],
            out_specs=pl.BlockSpec((TX, D), lambda w, wt, ws, wf, wl, wv: (wt[w], 0)),
            scratch_shapes=[pltpu.VMEM((TX, D), F32)]),
        out_shape=jax.ShapeDtypeStruct((n, D), F32),
        compiler_params=_cparams(("arbitrary",)),
        name="moe_combine",
    )(wt, ws, wf, wl, wv, pos2, yw, xf, gate2)


def _count_below(sorted_vals, x):
    return jnp.sum((sorted_vals[None, :] < x[:, None]).astype(jnp.int32), axis=1)


def _with_flags(g, it, compute, n_work, total):
    live = jnp.arange(n_work, dtype=jnp.int32) < total
    g_prev = jnp.concatenate([jnp.full((1,), -1, jnp.int32), g[:-1]])
    g_next = jnp.concatenate([g[1:], jnp.full((1,), -1, jnp.int32)])
    live_next = jnp.concatenate([live[1:], jnp.zeros((1,), bool)])
    first = live & (g != g_prev)
    last = live & ((g != g_next) | ~live_next)
    i32 = lambda a: a.astype(jnp.int32)
    return i32(g), i32(it), i32(first), i32(last), i32(live & compute)


def _moe_plan(sel, n_tok):
    E = N_EXPERTS
    sel = sel.astype(jnp.int32)
    cnt = jnp.sum(sel, axis=0)
    rank = jnp.cumsum(sel, axis=0) - 1
    gsz = ((cnt + TX - 1) // TX) * TX
    gend = jnp.cumsum(gsz)
    goff = gend - gsz
    n_tiles = (2 * n_tok + E * (TX - 1) + TX - 1) // TX
    P = n_tiles * TX
    pos = jnp.where(sel == 1, goff[None, :] + rank, -1)
    tile_start = jnp.arange(n_tiles, dtype=jnp.int32) * TX
    tile_valid = tile_start < gend[-1]
    tile_expert = jnp.minimum(_count_below(gend, tile_start + 1), E - 1)
    pmax_tok = jnp.max(pos, axis=1)
    psec_tok = jnp.max(jnp.where(pos == pmax_tok[:, None], -1, pos), axis=1)
    pos2 = jnp.stack([psec_tok, pmax_tok], axis=1)

    n_src_tiles = n_tok // TX
    cin = rank[TX - 1::TX] + 1
    ra = tile_start - goff[tile_expert]
    rb = ra + jnp.clip(cnt[tile_expert] - ra, 0, TX) - 1
    cin_t = cin.T[tile_expert]
    lo = jnp.sum((cin_t <= ra[:, None]).astype(jnp.int32), axis=1)
    hi = jnp.sum((cin_t <= rb[:, None]).astype(jnp.int32), axis=1)
    span = jnp.where(tile_valid, hi - lo + 1, 1)
    n_gw = E * n_src_tiles + n_tiles
    cs = jnp.cumsum(span)
    total = cs[-1]
    w = jnp.minimum(jnp.arange(n_gw, dtype=jnp.int32), total - 1)
    jw = _count_below(cs, w + 1)
    sw = jnp.where(tile_valid[jw], lo[jw] + (w - (cs[jw] - span[jw])), 0)
    gwork = _with_flags(jw, sw, tile_valid[jw], n_gw, total)

    pt = pos.reshape(n_src_tiles, TX, E)
    pmax = jnp.max(pt, axis=1)
    pmin = jnp.min(jnp.where(pt >= 0, pt, P), axis=1)
    ta = pmin // TX
    tb = pmax // TX
    cand_item = jnp.stack([ta, tb], axis=-1).reshape(-1)
    cand_valid = jnp.stack([pmax >= 0, (pmax >= 0) & (tb != ta)], axis=-1).reshape(-1).astype(jnp.int32)
    cand_group = jnp.repeat(jnp.arange(n_src_tiles, dtype=jnp.int32), 2 * E)
    n_cw = E * n_src_tiles + n_tiles
    ccs = jnp.cumsum(cand_valid)
    ctotal = ccs[-1]
    cw = jnp.minimum(jnp.arange(n_cw, dtype=jnp.int32), ctotal - 1)
    cidx = _count_below(ccs, cw + 1)
    cwork = _with_flags(cand_group[cidx], cand_item[cidx], jnp.ones((n_cw,), bool), n_cw, ctotal)
    return pos, pos2, tile_expert, tile_valid.astype(jnp.int32), gwork, cwork, n_tiles


def _moe(h2, xs1, cmb, sel, gate2, w1, w3, w2, idx, L):
    B, T, D = xs1.shape
    n_tok = B * T
    E = N_EXPERTS
    assert n_tok % TX == 0
    selm = sel.reshape(n_tok, LANES)[:, :E] > 0.5
    cmbm = cmb.reshape(n_tok, LANES)[:, :E]
    pos, pos2, tile_expert, tile_valid, gwork, cwork, n_tiles = _moe_plan(selm, n_tok)
    pos_t = pos.T.reshape(E, n_tok // TX, 1, TX)
    cmb_t = cmbm.T.reshape(E, n_tok // TX, 1, TX)

    xg, pw = _moe_gather(gwork, tile_expert, pos_t, cmb_t, h2.reshape(n_tok, D), n_tiles)
    yw = _moe_gffn(tile_expert, tile_valid, xg, pw, w1, w3, w2, idx)
    out = _moe_combine(cwork, pos2, yw, xs1.reshape(n_tok, D), gate2, B, T, L)
    return out.reshape(B, T, D)


def _final_kernel(x_ref, g_ref, o_ref):
    x = x_ref[0]
    o_ref[0] = x * lax.rsqrt(jnp.mean(x * x, axis=-1, keepdims=True) + EPS) * g_ref[...]


def _final_norm(xs, g, nct, S):
    B, T, D = xs.shape
    return pl.pallas_call(
        _final_kernel,
        grid=(B, S // TM),
        in_specs=[pl.BlockSpec((1, TM, D), lambda b, i: (b, i + nct, 0)), _const_spec((1, D))],
        out_specs=pl.BlockSpec((1, TM, D), lambda b, i: (b, i, 0)),
        out_shape=jax.ShapeDtypeStruct((B, S, D), F32),
        compiler_params=_cparams(("arbitrary", "arbitrary")),
        name="final_norm",
    )(xs, g)


def _rope_tables(L, S):
    t = np.arange(S)
    pos = np.stack([t // GRID_W, t % GRID_W], axis=-1).astype(np.float32)
    inv_freq = (ROPE_THETA ** (-np.arange(0, AXIS_ROPE_DIM, 2, dtype=np.float32) / AXIS_ROPE_DIM)).astype(np.float32)
    ang = pos[:, :, None] * inv_freq[None, None, :]
    cos, sin = np.cos(ang), np.sin(ang)
    cos64 = np.concatenate([cos[:, 0], cos[:, 0], cos[:, 1], cos[:, 1]], axis=-1)
    sin64 = np.concatenate([-sin[:, 0], sin[:, 0], -sin[:, 1], sin[:, 1]], axis=-1)
    cos_t = np.concatenate([np.ones((L, HEAD_DIM), np.float32), cos64], axis=0)
    sin_t = np.concatenate([np.zeros((L, HEAD_DIM), np.float32), sin64], axis=0)
    rep = LANES // HEAD_DIM
    return jnp.asarray(np.tile(cos_t, (1, rep)), F32), jnp.asarray(np.tile(sin_t, (1, rep)), F32)


def _na_bias_table(rel_bias):
    col = np.arange(GRID_W)
    col_start = np.clip(col - NA_WIN_COLS // 2, 0, GRID_W - NA_WIN_COLS)
    kc = np.arange(GRID_W)
    inside = (kc[None, :] >= col_start[:, None]) & (kc[None, :] < col_start[:, None] + NA_WIN_COLS)
    dc = np.clip(kc[None, :] - col[:, None] + (NA_WIN_COLS - 1), 0, 2 * NA_WIN_COLS - 2)
    case = np.arange(NA_WIN_ROWS)
    dr = np.arange(NA_WIN_ROWS)[None, :] - case[:, None] + (NA_WIN_ROWS - 1)
    t = rel_bias[:, :, dr]
    t = t[:, :, :, :, dc] * LOG2E
    t = jnp.where(jnp.asarray(inside)[None, None, None, None], t, MASK_VALUE)
    t = t.transpose(0, 2, 1, 4, 3, 5)
    return t.reshape(-1, NA_WIN_ROWS, B_HEADS, GRID_W, NA_WIN_ROWS * GRID_W).astype(F32)


def _block_diag_ones(n, blk):
    idx = np.arange(n) // blk
    return jnp.asarray((idx[:, None] == idx[None, :]).astype(np.float32), BF16)


def _pool_block_diag(pool_w):
    depth, g, c, d = pool_w.shape
    eye = jnp.asarray(np.eye(g, dtype=np.float32))
    out = pool_w[:, :, :, None, :] * eye[None, :, None, :, None]
    return out.reshape(depth, g * c, g * d).astype(BF16)


def kernel(x, c, ctx, c_ctx, w_mod, b_mod, norm1_g, norm2_g, w_in, q_norm_g, k_norm_g, na_rel_bias, pool_w,
           pool_scale, w_branch_a, w_branch_b, w_branch_c, w_out, ffn_w1, ffn_w3, ffn_w2, router_w, router_b,
           moe_w1, moe_w3, moe_w2, final_g):
    B, S, D = x.shape
    L = ctx.shape[1]
    T = L + S
    depth = w_mod.shape[0]
    assert D == D_MODEL and L % TM == 0 and S % TM == 0 and TM % GRID_W == 0 and B + 1 <= SUBLANES
    assert S // GRID_W >= NA_WIN_ROWS and TM // GRID_W <= NA_WIN_ROWS // 2 + 1
    nct = L // TM

    cvec = jnp.zeros((SUBLANES, D), F32).at[:B].set(c).at[B].set(c_ctx)
    mod = _mod_vectors(cvec, w_mod, b_mod)
    cos_t, sin_t = _rope_tables(L, S)
    bd = _block_diag_ones(QK_W, HEAD_DIM)
    na_bias = _na_bias_table(na_rel_bias)
    bf = lambda a: a.astype(BF16)
    w_qkv, w_gate = bf(w_in[:, :, :OFF_G]), bf(w_in[:, :, OFF_G:])
    wpa, wpb, wpc, wo, pbd = bf(w_branch_a), bf(w_branch_b), bf(w_branch_c), bf(w_out), _pool_block_diag(pool_w)
    fw1, fw3, fw2 = bf(ffn_w1), bf(ffn_w3), bf(ffn_w2)
    mw1, mw3, mw2 = bf(moe_w1), bf(moe_w3), bf(moe_w2)

    xs = jnp.concatenate([ctx, x], axis=1)
    for layer in range(depth):
        modr = mod[layer].reshape(SUBLANES, 6, D)
        g1 = norm1_g[layer].reshape(1, D)
        gqk = jnp.concatenate([jnp.tile(q_norm_g[layer], A_HEADS), jnp.tile(k_norm_g[layer], A_KV_HEADS)])
        q, kt, v, bq, bk, bv, cu = _inproj(xs, modr, g1, w_qkv, layer, bd, gqk.reshape(1, QK_W), cos_t, sin_t, nct)
        ya = _attn_a(q, kt, v, nct)
        yb = _attn_b(bq, bk, bv, na_bias, layer, nct, L)
        is_moe = layer % 2 == 1
        i = layer // 2
        router = None
        if is_moe:
            rw = jnp.zeros((D, LANES), F32).at[:, :N_EXPERTS].set(router_w[i])
            rw_hi = rw.astype(BF16)
            rw_lo = (rw - rw_hi.astype(F32)).astype(BF16)
            rb = jnp.zeros((1, LANES), F32).at[0, :N_EXPERTS].set(router_b[i])
            router = (rw_hi, rw_lo, rb)
        outs = _merge(ya, yb, cu, xs, modr, g1, w_gate, wpa, wpb, wpc, pbd, pool_scale[layer].reshape(1, C_WIDTH),
                      wo, norm2_g[layer].reshape(1, D), layer, router, nct, L)
        if is_moe:
            xs1, h2, cmb, sel = outs
            xs = _moe(h2, xs1, cmb, sel, modr[:, 5, :], mw1, mw3, mw2, i, L)
        else:
            xs1, h2 = outs
            xs = _ffn_dense(h2, xs1, modr[:, 5:6, :], fw1, fw3, fw2, i, nct)
    return _final_norm(xs, final_g.reshape(1, D), nct, S)
```

```python
import functools

import numpy as np
import jax
import jax.numpy as jnp
from jax import lax
from jax.experimental import pallas as pl
from jax.experimental.pallas import tpu as pltpu

F32 = jnp.float32
BF16 = jnp.bfloat16

D_MODEL = 1024
DEPTH = 4
GRID_W = 64
HEAD_DIM = 64
A_HEADS = 8
A_KV_HEADS = 2
A_GROUP = A_HEADS // A_KV_HEADS
B_HEADS = 4
C_GROUPS = 4
C_GROUP_DIM = 64
C_WIDTH = C_GROUPS * C_GROUP_DIM
POOL_WINDOWS = (2, 4, 8, 16)
NA_WIN_ROWS = 8
NA_WIN_COLS = 16
ROPE_THETA = 10000.0
AXIS_ROPE_DIM = HEAD_DIM // 2
N_EXPERTS = 8
EPS = 1e-6

A_Q = A_HEADS * HEAD_DIM
A_KV = A_KV_HEADS * HEAD_DIM
B_W = B_HEADS * HEAD_DIM
QK_W = A_Q + A_KV
OFF_AV = QK_W
OFF_BQ = OFF_AV + A_KV
OFF_BK = OFF_BQ + B_W
OFF_BV = OFF_BK + B_W
OFF_CU = OFF_BV + B_W
OFF_G = OFF_CU + C_WIDTH

LANES = 128
SUBLANES = 8
TM = 256
TX = 512
FC = 512
A_KBLK = 8
A_TQ = 256
LOG2E = 1.4426950408889634
POOL_HALO = 8
MASK_VALUE = -1e30
VMEM_LIMIT = 56 * 1024 * 1024


def _cparams(sem):
    return pltpu.CompilerParams(dimension_semantics=sem, vmem_limit_bytes=VMEM_LIMIT)


def _dot(a, b):
    return jnp.dot(a, b, preferred_element_type=F32)


def _dot_nt(a, b):
    return lax.dot_general(a, b, (((1,), (1,)), ((), ())), preferred_element_type=F32)


def _split(a):
    hi = a.astype(BF16)
    lo = (a - hi.astype(F32)).astype(BF16)
    return hi, lo


def _dot3(a, b_hi, b_lo):
    a_hi, a_lo = _split(a)
    return _dot(a_hi, b_hi) + (_dot(a_lo, b_hi) + _dot(a_hi, b_lo))


def _const_spec(shape):
    n = len(shape)
    return pl.BlockSpec(shape, lambda *_: (0,) * n, pipeline_mode=pl.Buffered(1))


def _layer_spec(shape, layer):
    n = len(shape)
    return pl.BlockSpec((1,) + tuple(shape), lambda *_: (layer,) + (0,) * n, pipeline_mode=pl.Buffered(1))


def _rms_mod(x, g, shift, scale):
    y = x * lax.rsqrt(jnp.mean(x * x, axis=-1, keepdims=True) + EPS) * g
    return y * (1.0 + scale) + shift


def _mod_kernel(c_ref, w_ref, b_ref, o_ref):
    c = c_ref[...]
    s = c * jax.nn.sigmoid(c)
    w_hi, w_lo = _split(w_ref[0])
    o_ref[0] = _dot3(s, w_hi, w_lo) + b_ref[0]


def _mod_vectors(cvec, w_mod, b_mod):
    depth, d, n = w_mod.shape
    tn = 1536
    return pl.pallas_call(
        _mod_kernel,
        grid=(depth, n // tn),
        in_specs=[pl.BlockSpec((SUBLANES, d), lambda l, j: (0, 0)),
                  pl.BlockSpec((1, d, tn), lambda l, j: (l, 0, j)),
                  pl.BlockSpec((1, 1, tn), lambda l, j: (l, 0, j))],
        out_specs=pl.BlockSpec((1, SUBLANES, tn), lambda l, j: (l, 0, j)),
        out_shape=jax.ShapeDtypeStruct((depth, SUBLANES, n), F32),
        compiler_params=_cparams(("arbitrary", "arbitrary")),
        name="mod_vectors",
    )(cvec, w_mod, b_mod.reshape(depth, 1, n))


def _swap16(x):
    lane = lax.broadcasted_iota(jnp.int32, (1, LANES), 1)
    first = (lane % 32) < 16
    return jnp.where(first, pltpu.roll(x, LANES - 16, 1), pltpu.roll(x, 16, 1))


def _inproj_kernel(x_ref, mod_ref, g_ref, w_ref, bd_ref, gqk_ref, cos_ref, sin_ref,
                   q_ref, kt_ref, v_ref, bq_ref, bk_ref, bv_ref, cu_ref):
    m = mod_ref[0]
    h = _rms_mod(x_ref[0], g_ref[...], m[0:1], m[1:2])
    px = _dot(h.astype(BF16), w_ref[0])

    qk = px[:, :QK_W]
    sq_hi, sq_lo = _split(qk * qk)
    ss = _dot(sq_hi, bd_ref[...]) + _dot(sq_lo, bd_ref[...])
    qn = qk * lax.rsqrt(ss * (1.0 / HEAD_DIM) + EPS) * gqk_ref[...]
    cos = cos_ref[...]
    sin = sin_ref[...]
    chunks = []
    for j in range(QK_W // LANES):
        c = qn[:, j * LANES:(j + 1) * LANES]
        chunks.append(c * cos + _swap16(c) * sin)
    scale = HEAD_DIM ** -0.5
    for j in range(A_Q // LANES):
        q_ref[0, :, j * LANES:(j + 1) * LANES] = (chunks[j] * (scale * LOG2E)).astype(BF16)
    kt_ref[0, 0] = chunks[A_Q // LANES].T.astype(BF16)

    ones = jnp.ones((TM, LANES - HEAD_DIM), BF16)
    for j in range(A_KV_HEADS):
        vj = px[:, OFF_AV + j * HEAD_DIM:OFF_AV + (j + 1) * HEAD_DIM].astype(BF16)
        v_ref[0, :, j * LANES:(j + 1) * LANES] = jnp.concatenate([vj, ones], axis=1)
    bq_ref[0] = (px[:, OFF_BQ:OFF_BK] * (scale * LOG2E)).astype(BF16)
    bk_ref[0] = px[:, OFF_BK:OFF_BV].astype(BF16)
    for h in range(B_HEADS):
        vh = px[:, OFF_BV + h * HEAD_DIM:OFF_BV + (h + 1) * HEAD_DIM].astype(BF16)
        bv_ref[0, :, h * LANES:(h + 1) * LANES] = jnp.concatenate([vh, ones], axis=1)
    cu_ref[0] = px[:, OFF_CU:OFF_G]


def _inproj(xs, modr, g1, w_qkv, layer, bd, gqk, cos_t, sin_t, nct):
    B, T, D = xs.shape
    nt = T // TM
    tok = lambda w: pl.BlockSpec((1, TM, w), lambda b, i: (b, i, 0))
    out_shape = (
        jax.ShapeDtypeStruct((B, T, A_Q), BF16),
        jax.ShapeDtypeStruct((B, nt, A_KV, TM), BF16),
        jax.ShapeDtypeStruct((B, T, A_KV_HEADS * LANES), BF16),
        jax.ShapeDtypeStruct((B, T, B_W), BF16),
        jax.ShapeDtypeStruct((B, T, B_W), BF16),
        jax.ShapeDtypeStruct((B, T, B_HEADS * LANES), BF16),
        jax.ShapeDtypeStruct((B, T, C_WIDTH), F32),
    )
    return pl.pallas_call(
        _inproj_kernel,
        grid=(B, nt),
        in_specs=[tok(D),
                  pl.BlockSpec((1, 6, D), lambda b, i: (jnp.where(i < nct, B, b), 0, 0)),
                  _const_spec((1, D)),
                  _layer_spec((D, OFF_G), layer),
                  _const_spec((QK_W, QK_W)),
                  _const_spec((1, QK_W)),
                  pl.BlockSpec((TM, LANES), lambda b, i: (i, 0)),
                  pl.BlockSpec((TM, LANES), lambda b, i: (i, 0))],
        out_specs=(tok(A_Q),
                   pl.BlockSpec((1, 1, A_KV, TM), lambda b, i: (b, i, 0, 0)),
                   tok(A_KV_HEADS * LANES), tok(B_W), tok(B_W), tok(B_HEADS * LANES), tok(C_WIDTH)),
        out_shape=out_shape,
        compiler_params=_cparams(("arbitrary", "arbitrary")),
        name="inproj",
    )(xs, modr, g1, w_qkv, bd, gqk, cos_t, sin_t)


def _attn_a_kernel(q_ref, kt_ref, v_ref, o_ref, *, nct, nct_q, n_steps):
    i = pl.program_id(1)
    rows = A_GROUP * A_TQ

    def step(carry, q4, j, blk0, nblk):
        m, acc = carry
        s_list = [_dot(q4, kt_ref[0, blk0 + c, j * HEAD_DIM:(j + 1) * HEAD_DIM, :]) for c in range(nblk)]
        smax = s_list[0]
        for s in s_list[1:]:
            smax = jnp.maximum(smax, s)
        m_new = jnp.maximum(m, jnp.max(smax, axis=-1, keepdims=True))
        alpha = jnp.exp2(m - m_new)
        p = jnp.concatenate([jnp.exp2((s - m_new).astype(BF16)) for s in s_list], axis=1)
        vv = v_ref[0, pl.ds(pl.multiple_of(blk0 * TM, TM), nblk * TM), j * LANES:(j + 1) * LANES]
        return m_new, alpha * acc + _dot(p, vv)

    def run(n_main):
        for j in range(A_KV_HEADS):
            base = j * A_GROUP * HEAD_DIM
            q4 = jnp.concatenate(
                [q_ref[0, :, base + g * HEAD_DIM: base + (g + 1) * HEAD_DIM] for g in range(A_GROUP)], axis=0)
            carry = (jnp.full((rows, 1), MASK_VALUE, F32), jnp.zeros((rows, LANES), F32))
            carry = step(carry, q4, j, 0, nct)
            if n_main:
                carry = lax.fori_loop(
                    0, n_main, lambda k, cr: step(cr, q4, j, nct + k * A_KBLK, A_KBLK), carry)
            acc = carry[1]
            o = acc[:, 0:HEAD_DIM] / acc[:, HEAD_DIM:HEAD_DIM + 1]
            for g in range(A_GROUP):
                o_ref[0, :, base + g * HEAD_DIM: base + (g + 1) * HEAD_DIM] = o[g * A_TQ:(g + 1) * A_TQ].astype(BF16)

    @pl.when(i < nct_q)
    def _():
        run(0)

    @pl.when(i >= nct_q)
    def _():
        run(n_steps)


def _attn_a(q, kt, v, nct):
    B, T, _ = q.shape
    nt = T // TM
    assert (nt - nct) % A_KBLK == 0
    return pl.pallas_call(
        functools.partial(_attn_a_kernel, nct=nct, nct_q=nct * (TM // A_TQ), n_steps=(nt - nct) // A_KBLK),
        grid=(B, T // A_TQ),
        in_specs=[pl.BlockSpec((1, A_TQ, A_Q), lambda b, i: (b, i, 0)),
                  pl.BlockSpec((1, nt, A_KV, TM), lambda b, i: (b, 0, 0, 0)),
                  pl.BlockSpec((1, T, A_KV_HEADS * LANES), lambda b, i: (b, 0, 0))],
        out_specs=pl.BlockSpec((1, A_TQ, A_Q), lambda b, i: (b, i, 0)),
        out_shape=jax.ShapeDtypeStruct((B, T, A_Q), BF16),
        compiler_params=_cparams(("arbitrary", "arbitrary")),
        name="attn_a",
    )(q, kt, v)


def _attn_b_kernel(q_ref, kp_ref, kc_ref, kn_ref, vp_ref, vc_ref, vn_ref, kx_ref, vx_ref, bias_ref,
                   o_ref, kbuf, vbuf, *, nct, L):
    i = pl.program_id(1)

    def normalised(o_ext):
        return (o_ext[:, 0:HEAD_DIM] / o_ext[:, HEAD_DIM:HEAD_DIM + 1]).astype(BF16)

    @pl.when(i < nct)
    def _():
        for h in range(B_HEADS):
            sl = slice(h * HEAD_DIM, (h + 1) * HEAD_DIM)
            vl = slice(h * LANES, (h + 1) * LANES)
            s = _dot_nt(q_ref[0, :, sl], kx_ref[0, :, sl])
            m = jnp.max(s, axis=-1, keepdims=True)
            p = jnp.exp2((s - m).astype(BF16))
            o_ref[0, :, sl] = normalised(_dot(p, vx_ref[0, :, vl]))

    @pl.when(i >= nct)
    def _():
        for buf, cx, pv, cu, nx in ((kbuf, kx_ref, kp_ref, kc_ref, kn_ref), (vbuf, vx_ref, vp_ref, vc_ref, vn_ref)):
            buf[0:L] = cx[0]
            buf[L:L + TM] = pv[0]
            buf[L + TM:L + 2 * TM] = cu[0]
            buf[L + 2 * TM:L + 3 * TM] = nx[0]
        for h in range(B_HEADS):
            sl = slice(h * HEAD_DIM, (h + 1) * HEAD_DIM)
            s = _dot_nt(q_ref[0, :, sl], kbuf[:, sl])
            s_c = s[:, 0:L]
            s_w = s[:, L:] + bias_ref[0, 0, h]
            m = jnp.maximum(jnp.max(s_w, axis=-1, keepdims=True), jnp.max(s_c, axis=-1, keepdims=True))
            p = jnp.concatenate([jnp.exp2((s_c - m).astype(BF16)), jnp.exp2((s_w - m).astype(BF16))], axis=1)
            o_ref[0, :, sl] = normalised(_dot(p, vbuf[:, h * LANES:(h + 1) * LANES]))


def _attn_b(bq, bk, bv, bias_t, layer, nct, L):
    B, T, _ = bq.shape
    nt = T // TM
    assert L % LANES == 0 and nt - nct >= 2
    VW = B_HEADS * LANES
    case = lambda i: jnp.where(i <= nct, 0, jnp.where(i == nt - 1, 2, 1))
    cur = lambda b, i: (b, i, 0)
    prev = lambda b, i: (b, jnp.maximum(i - 1, nct), 0)
    nxt = lambda b, i: (b, jnp.minimum(i + 1, nt - 1), 0)
    blk = lambda f: pl.BlockSpec((1, TM, B_W), f)
    vblk = lambda f: pl.BlockSpec((1, TM, VW), f)
    ctx = lambda w: pl.BlockSpec((1, L, w), lambda b, i: (b, 0, 0))
    return pl.pallas_call(
        functools.partial(_attn_b_kernel, nct=nct, L=L),
        grid=(B, nt),
        in_specs=[blk(cur), blk(prev), blk(cur), blk(nxt), vblk(prev), vblk(cur), vblk(nxt), ctx(B_W), ctx(VW),
                  pl.BlockSpec((1, 1) + bias_t.shape[2:], lambda b, i: (layer, case(i), 0, 0, 0))],
        out_specs=blk(cur),
        out_shape=jax.ShapeDtypeStruct((B, T, B_W), BF16),
        scratch_shapes=[pltpu.VMEM((L + 3 * TM, B_W), BF16), pltpu.VMEM((L + 3 * TM, VW), BF16)],
        compiler_params=_cparams(("arbitrary", "arbitrary")),
        name="attn_b",
    )(bq, bk, bk, bk, bv, bv, bv, bk, bv, bias_t)


def _merge_kernel(*refs, nct, nt, L, S, with_router):
    if with_router:
        (ya_ref, yb_ref, cup_ref, cu_ref, cun_ref, x_ref, mod_ref, g1_ref, wg_ref, wpa_ref, wpb_ref, wpc_ref,
         pbd_ref, psc_ref, wo_ref, g2_ref, rwh_ref, rwl_ref, rb_ref,
         xo_ref, h_ref, cmb_ref, sel_ref, e_scr) = refs
    else:
        (ya_ref, yb_ref, cup_ref, cu_ref, cun_ref, x_ref, mod_ref, g1_ref, wg_ref, wpa_ref, wpb_ref, wpc_ref,
         pbd_ref, psc_ref, wo_ref, g2_ref, xo_ref, h_ref, e_scr) = refs
    i = pl.program_id(1)
    D = D_MODEL
    m = mod_ref[0]
    x_in = x_ref[0]

    at_start = (i == 0) | (i == nct)
    at_end = (i == nct - 1) | (i == nt - 1)
    u = cu_ref[0]
    e_scr[0:POOL_HALO] = jnp.where(at_start, 0.0, cup_ref[0])
    e_scr[POOL_HALO:POOL_HALO + TM] = u
    e_scr[POOL_HALO + TM:] = jnp.where(at_end, 0.0, cun_ref[0])
    sh = lambda k: e_scr[POOL_HALO + k:POOL_HALO + k + TM]
    w2 = sh(-1) + sh(0)
    w4 = w2 + (sh(-2) + sh(1))
    w8 = w4 + ((sh(-4) + sh(-3)) + (sh(2) + sh(3)))
    w16 = w8 + (((sh(-8) + sh(-7)) + (sh(-6) + sh(-5))) + ((sh(4) + sh(5)) + (sh(6) + sh(7))))
    grp = lax.broadcasted_iota(jnp.int32, (1, C_WIDTH), 1) // C_GROUP_DIM
    half = jnp.left_shift(1, grp)
    t_loc = lax.broadcasted_iota(jnp.int32, (TM, 1), 0)
    t_seq = jnp.where(i < nct, i * TM, (i - nct) * TM) + t_loc
    n_seq = jnp.where(i < nct, L, S)
    cnt = jnp.minimum(t_seq + half, n_seq) - jnp.maximum(t_seq - half, 0)
    wsum = jnp.where(grp == 0, w2, jnp.where(grp == 1, w4, jnp.where(grp == 2, w8, w16)))
    dlt = wsum / cnt.astype(F32) - u
    yc = _dot(dlt.astype(BF16), pbd_ref[0]) * psc_ref[...]

    h1 = _rms_mod(x_in, g1_ref[...], m[0:1], m[1:2]).astype(BF16)
    branches = (ya_ref[0], yb_ref[0], yc.astype(BF16))
    weights = (wpa_ref, wpb_ref, wpc_ref)
    mrg = None
    for k in range(3):
        gate = jax.nn.sigmoid(_dot(h1, wg_ref[0, :, k * D:(k + 1) * D]))
        term = gate * _dot(branches[k], weights[k][0])
        mrg = term if mrg is None else mrg + term
    x = x_in + m[2:3] * _dot(mrg.astype(BF16), wo_ref[0])
    xo_ref[0] = x
    h = _rms_mod(x, g2_ref[...], m[3:4], m[4:5])
    h_ref[0] = h.astype(BF16)

    if with_router:
        lane = lax.broadcasted_iota(jnp.int32, (1, LANES), 1).astype(F32)
        lg = _dot3(h, rwh_ref[...], rwl_ref[...]) + rb_ref[...]
        lg = jnp.where(lane < N_EXPERTS, lg, -jnp.inf)
        m1 = jnp.max(lg, axis=-1, keepdims=True)
        i1 = jnp.min(jnp.where(lg == m1, lane, float(LANES)), axis=-1, keepdims=True)
        mask1 = lane == i1
        lg2 = jnp.where(mask1, -jnp.inf, lg)
        m2 = jnp.max(lg2, axis=-1, keepdims=True)
        i2 = jnp.min(jnp.where(lg2 == m2, lane, float(LANES)), axis=-1, keepdims=True)
        mask2 = lane == i2
        e2 = jnp.exp(m2 - m1)
        den = 1.0 + e2
        cmb_ref[0] = jnp.where(mask1, 1.0 / den, 0.0) + jnp.where(mask2, e2 / den, 0.0)
        sel_ref[0] = jnp.where(mask1 | mask2, 1.0, 0.0)


def _merge(ya, yb, cu, xs, modr, g1, w_gate, wpa, wpb, wpc, pbd, psc, wo, g2, layer, router, nct, L):
    B, T, D = xs.shape
    nt = T // TM
    S = T - L
    hb = TM // POOL_HALO
    tok = lambda w: pl.BlockSpec((1, TM, w), lambda b, i: (b, i, 0))
    in_specs = [tok(A_Q), tok(B_W),
                pl.BlockSpec((1, POOL_HALO, C_WIDTH), lambda b, i: (b, jnp.maximum(i * hb - 1, 0), 0)),
                tok(C_WIDTH),
                pl.BlockSpec((1, POOL_HALO, C_WIDTH), lambda b, i: (b, jnp.minimum((i + 1) * hb, nt * hb - 1), 0)),
                tok(D),
                pl.BlockSpec((1, 6, D), lambda b, i: (jnp.where(i < nct, B, b), 0, 0)),
                _const_spec((1, D)),
                _layer_spec((D, 3 * D), layer),
                _layer_spec((A_Q, D), layer), _layer_spec((B_W, D), layer), _layer_spec((C_WIDTH, D), layer),
                _layer_spec((C_WIDTH, C_WIDTH), layer), _const_spec((1, C_WIDTH)), _layer_spec((D, D), layer),
                _const_spec((1, D))]
    args = [ya, yb, cu, cu, cu, xs, modr, g1, w_gate, wpa, wpb, wpc, pbd, psc, wo, g2]
    out_specs = [tok(D), tok(D)]
    out_shape = [jax.ShapeDtypeStruct((B, T, D), F32), jax.ShapeDtypeStruct((B, T, D), BF16)]
    if router is not None:
        in_specs += [_const_spec((D, LANES)), _const_spec((D, LANES)), _const_spec((1, LANES))]
        args += list(router)
        out_specs += [tok(LANES), tok(LANES)]
        out_shape += [jax.ShapeDtypeStruct((B, T, LANES), F32)] * 2
    return pl.pallas_call(
        functools.partial(_merge_kernel, nct=nct, nt=nt, L=L, S=S, with_router=router is not None),
        grid=(B, nt),
        in_specs=in_specs,
        out_specs=tuple(out_specs),
        out_shape=tuple(out_shape),
        scratch_shapes=[pltpu.VMEM((TM + 2 * POOL_HALO, C_WIDTH), F32)],
        compiler_params=_cparams(("arbitrary", "arbitrary")),
        name="merge_router" if router is not None else "merge",
    )(*args)


def _swiglu_acc(h, w1_ref, w3_ref, w2_ref, lead):
    d_ff = w1_ref.shape[-1]
    acc = None
    for f0 in range(0, d_ff, FC):
        f1 = min(f0 + FC, d_ff)
        a = _dot(h, w1_ref[lead + (slice(None), slice(f0, f1))])
        b = _dot(h, w3_ref[lead + (slice(None), slice(f0, f1))])
        t = (a * jax.nn.sigmoid(a) * b).astype(BF16)
        part = _dot(t, w2_ref[lead + (slice(f0, f1), slice(None))])
        acc = part if acc is None else acc + part
    return acc


def _ffn_kernel(h_ref, x_ref, g_ref, w1_ref, w3_ref, w2_ref, o_ref):
    f = _swiglu_acc(h_ref[0], w1_ref, w3_ref, w2_ref, (0,))
    o_ref[0] = x_ref[0] + g_ref[0] * f


def _ffn_dense(h2, xs, gate2, w1, w3, w2, idx, nct):
    B, T, D = xs.shape
    F = w1.shape[-1]
    nt = T // TM
    tok = lambda: pl.BlockSpec((1, TM, D), lambda b, i: (b, i, 0))
    return pl.pallas_call(
        _ffn_kernel,
        grid=(B, nt),
        in_specs=[tok(), tok(),
                  pl.BlockSpec((1, 1, D), lambda b, i: (jnp.where(i < nct, B, b), 0, 0)),
                  _layer_spec((D, F), idx), _layer_spec((D, F), idx), _layer_spec((F, D), idx)],
        out_specs=tok(),
        out_shape=jax.ShapeDtypeStruct((B, T, D), F32),
        compiler_params=_cparams(("arbitrary", "arbitrary")),
        name="ffn_dense",
    )(h2, xs, gate2, w1, w3, w2)


def _gather_kernel(wj_ref, ws_ref, wf_ref, wl_ref, wv_ref, te_ref, pos_ref, cmb_ref, h_ref,
                   o_ref, pw_ref, acc_ref, accw_ref):
    w = pl.program_id(0)

    @pl.when(wf_ref[w] == 1)
    def _():
        acc_ref[...] = jnp.zeros_like(acc_ref)
        accw_ref[...] = jnp.zeros_like(accw_ref)

    @pl.when(wv_ref[w] == 1)
    def _():
        row = lax.broadcasted_iota(jnp.int32, (TX, 1), 0) + wj_ref[w] * TX
        hit = pos_ref[0, 0] == row
        acc_ref[...] += _dot(jnp.where(hit, 1.0, 0.0).astype(BF16), h_ref[...])
        accw_ref[...] += jnp.sum(jnp.where(hit, cmb_ref[0, 0], 0.0), axis=-1, keepdims=True)

    @pl.when(wl_ref[w] == 1)
    def _():
        o_ref[...] = acc_ref[...].astype(BF16)
        pw_ref[...] = accw_ref[...]


def _moe_gather(work, tile_expert, pos_t, cmb_t, h2f, n_tiles):
    wj, ws, wf, wl, wv = work
    n, D = h2f.shape
    row_spec = pl.BlockSpec((1, 1, 1, TX), lambda w, wj, ws, wf, wl, wv, te: (te[wj[w]], ws[w], 0, 0))
    return pl.pallas_call(
        _gather_kernel,
        grid_spec=pltpu.PrefetchScalarGridSpec(
            num_scalar_prefetch=6,
            grid=(wj.shape[0],),
            in_specs=[row_spec, row_spec,
                      pl.BlockSpec((TX, D), lambda w, wj, ws, wf, wl, wv, te: (ws[w], 0))],
            out_specs=(pl.BlockSpec((TX, D), lambda w, wj, ws, wf, wl, wv, te: (wj[w], 0)),
                       pl.BlockSpec((TX, 1), lambda w, wj, ws, wf, wl, wv, te: (wj[w], 0))),
            scratch_shapes=[pltpu.VMEM((TX, D), F32), pltpu.VMEM((TX, 1), F32)]),
        out_shape=(jax.ShapeDtypeStruct((n_tiles * TX, D), BF16),
                   jax.ShapeDtypeStruct((n_tiles * TX, 1), F32)),
        compiler_params=_cparams(("arbitrary",)),
        name="moe_gather",
    )(wj, ws, wf, wl, wv, tile_expert, pos_t, cmb_t, h2f)


def _gffn_kernel(te_ref, tv_ref, x_ref, pw_ref, w1_ref, w3_ref, w2_ref, o_ref):
    j = pl.program_id(0)

    @pl.when(tv_ref[j] == 1)
    def _():
        f = _swiglu_acc(x_ref[...], w1_ref, w3_ref, w2_ref, (0, 0))
        o_ref[...] = (f * pw_ref[...]).astype(BF16)

    @pl.when(tv_ref[j] == 0)
    def _():
        o_ref[...] = jnp.zeros_like(o_ref)


def _moe_gffn(tile_expert, tile_valid, xg, pw, w1, w3, w2, idx):
    P, D = xg.shape
    F = w1.shape[-1]
    wspec = lambda shape: pl.BlockSpec((1, 1) + shape, lambda j, te, tv: (idx, te[j], 0, 0),
                                       pipeline_mode=pl.Buffered(1))
    return pl.pallas_call(
        _gffn_kernel,
        grid_spec=pltpu.PrefetchScalarGridSpec(
            num_scalar_prefetch=2,
            grid=(P // TX,),
            in_specs=[pl.BlockSpec((TX, D), lambda j, te, tv: (j, 0)),
                      pl.BlockSpec((TX, 1), lambda j, te, tv: (j, 0)),
                      wspec((D, F)), wspec((D, F)), wspec((F, D))],
            out_specs=pl.BlockSpec((TX, D), lambda j, te, tv: (j, 0))),
        out_shape=jax.ShapeDtypeStruct((P, D), BF16),
        compiler_params=_cparams(("arbitrary",)),
        name="moe_gffn",
    )(tile_expert, tile_valid, xg, pw, w1, w3, w2)


def _combine_kernel(wt_ref, ws_ref, wf_ref, wl_ref, wv_ref, pos_ref, y_ref, x_ref, g_ref, o_ref, acc_ref,
                    *, n_batch, T, L):
    w = pl.program_id(0)

    @pl.when(wf_ref[w] == 1)
    def _():
        acc_ref[...] = jnp.zeros_like(acc_ref)

    @pl.when(wv_ref[w] == 1)
    def _():
        col = lax.broadcasted_iota(jnp.int32, (1, TX), 1) + ws_ref[w] * TX
        pos = pos_ref[...]
        hit = (pos[:, 0:1] == col) | (pos[:, 1:2] == col)
        acc_ref[...] += _dot(jnp.where(hit, 1.0, 0.0).astype(BF16), y_ref[...])

    @pl.when(wl_ref[w] == 1)
    def _():
        n = lax.broadcasted_iota(jnp.int32, (TX, 1), 0) + wt_ref[w] * TX
        b = jnp.zeros((TX, 1), jnp.int32)
        for k in range(1, n_batch):
            b = b + (n >= k * T).astype(jnp.int32)
        rowid = jnp.where(n - b * T < L, n_batch, b)
        gate = jnp.zeros((TX, D_MODEL), F32)
        for r in range(n_batch + 1):
            gate = jnp.where(rowid == r, g_ref[r:r + 1, :], gate)
        o_ref[...] = x_ref[...] + gate * acc_ref[...]


def _moe_combine(work, pos2, yw, xf, gate2, n_batch, T, L):
    wt, ws, wf, wl, wv = work
    n, D = xf.shape
    return pl.pallas_call(
        functools.partial(_combine_kernel, n_batch=n_batch, T=T, L=L),
        grid_spec=pltpu.PrefetchScalarGridSpec(
            num_scalar_prefetch=5,
            grid=(wt.shape[0],),
            in_specs=[pl.BlockSpec((TX, 2), lambda w, wt, ws, wf, wl, wv: (wt[w], 0)),
                      pl.BlockSpec((TX, D), lambda w, wt, ws, wf, wl, wv: (ws[w], 0)),
                      pl.BlockSpec((TX, D), lambda w, wt, ws, wf, wl, wv: (wt[w], 0)),
                      pl.BlockSpec((SUBLANES, D), lambda w, wt, ws, wf, wl, wv: (0, 0))],
            out_specs=pl.BlockSpec((TX, D), lambda w, wt, ws, wf, wl, wv: (wt[w], 0)),
            scratch_shapes=[pltpu.VMEM((TX, D), F32)]),
        out_shape=jax.ShapeDtypeStruct((n, D), F32),
        compiler_params=_cparams(("arbitrary",)),
        name="moe_combine",
    )(wt, ws, wf, wl, wv, pos2, yw, xf, gate2)


def _count_below(sorted_vals, x):
    return jnp.sum((sorted_vals[None, :] < x[:, None]).astype(jnp.int32), axis=1)


def _with_flags(g, it, compute, n_work, total):
    live = jnp.arange(n_work, dtype=jnp.int32) < total
    g_prev = jnp.concatenate([jnp.full((1,), -1, jnp.int32), g[:-1]])
    g_next = jnp.concatenate([g[1:], jnp.full((1,), -1, jnp.int32)])
    live_next = jnp.concatenate([live[1:], jnp.zeros((1,), bool)])
    first = live & (g != g_prev)
    last = live & ((g != g_next) | ~live_next)
    i32 = lambda a: a.astype(jnp.int32)
    return i32(g), i32(it), i32(first), i32(last), i32(live & compute)


def _moe_plan(sel, n_tok):
    E = N_EXPERTS
    sel = sel.astype(jnp.int32)
    cnt = jnp.sum(sel, axis=0)
    rank = jnp.cumsum(sel, axis=0) - 1
    gsz = ((cnt + TX - 1) // TX) * TX
    gend = jnp.cumsum(gsz)
    goff = gend - gsz
    n_tiles = (2 * n_tok + E * (TX - 1) + TX - 1) // TX
    P = n_tiles * TX
    pos = jnp.where(sel == 1, goff[None, :] + rank, -1)
    tile_start = jnp.arange(n_tiles, dtype=jnp.int32) * TX
    tile_valid = tile_start < gend[-1]
    tile_expert = jnp.minimum(_count_below(gend, tile_start + 1), E - 1)
    pmax_tok = jnp.max(pos, axis=1)
    psec_tok = jnp.max(jnp.where(pos == pmax_tok[:, None], -1, pos), axis=1)
    pos2 = jnp.stack([psec_tok, pmax_tok], axis=1)

    n_src_tiles = n_tok // TX
    cin = rank[TX - 1::TX] + 1
    ra = tile_start - goff[tile_expert]
    rb = ra + jnp.clip(cnt[tile_expert] - ra, 0, TX) - 1
    cin_t = cin.T[tile_expert]
    lo = jnp.sum((cin_t <= ra[:, None]).astype(jnp.int32), axis=1)
    hi = jnp.sum((cin_t <= rb[:, None]).astype(jnp.int32), axis=1)
    span = jnp.where(tile_valid, hi - lo + 1, 1)
    n_gw = E * n_src_tiles + n_tiles
    cs = jnp.cumsum(span)
    total = cs[-1]
    w = jnp.minimum(jnp.arange(n_gw, dtype=jnp.int32), total - 1)
    jw = _count_below(cs, w + 1)
    sw = jnp.where(tile_valid[jw], lo[jw] + (w - (cs[jw] - span[jw])), 0)
    gwork = _with_flags(jw, sw, tile_valid[jw], n_gw, total)

    pt = pos.reshape(n_src_tiles, TX, E)
    pmax = jnp.max(pt, axis=1)
    pmin = jnp.min(jnp.where(pt >= 0, pt, P), axis=1)
    ta = pmin // TX
    tb = pmax // TX
    cand_item = jnp.stack([ta, tb], axis=-1).reshape(-1)
    cand_valid = jnp.stack([pmax >= 0, (pmax >= 0) & (tb != ta)], axis=-1).reshape(-1).astype(jnp.int32)
    cand_group = jnp.repeat(jnp.arange(n_src_tiles, dtype=jnp.int32), 2 * E)
    n_cw = E * n_src_tiles + n_tiles
    ccs = jnp.cumsum(cand_valid)
    ctotal = ccs[-1]
    cw = jnp.minimum(jnp.arange(n_cw, dtype=jnp.int32), ctotal - 1)
    cidx = _count_below(ccs, cw + 1)
    cwork = _with_flags(cand_group[cidx], cand_item[cidx], jnp.ones((n_cw,), bool), n_cw, ctotal)
    return pos, pos2, tile_expert, tile_valid.astype(jnp.int32), gwork, cwork, n_tiles


def _moe(h2, xs1, cmb, sel, gate2, w1, w3, w2, idx, L):
    B, T, D = xs1.shape
    n_tok = B * T
    E = N_EXPERTS
    assert n_tok % TX == 0
    selm = sel.reshape(n_tok, LANES)[:, :E] > 0.5
    cmbm = cmb.reshape(n_tok, LANES)[:, :E]
    pos, pos2, tile_expert, tile_valid, gwork, cwork, n_tiles = _moe_plan(selm, n_tok)
    pos_t = pos.T.reshape(E, n_tok // TX, 1, TX)
    cmb_t = cmbm.T.reshape(E, n_tok // TX, 1, TX)

    xg, pw = _moe_gather(gwork, tile_expert, pos_t, cmb_t, h2.reshape(n_tok, D), n_tiles)
    yw = _moe_gffn(tile_expert, tile_valid, xg, pw, w1, w3, w2, idx)
    out = _moe_combine(cwork, pos2, yw, xs1.reshape(n_tok, D), gate2, B, T, L)
    return out.reshape(B, T, D)


def _final_kernel(x_ref, g_ref, o_ref):
    x = x_ref[0]
    o_ref[0] = x * lax.rsqrt(jnp.mean(x * x, axis=-1, keepdims=True) + EPS) * g_ref[...]


def _final_norm(xs, g, nct, S):
    B, T, D = xs.shape
    return pl.pallas_call(
        _final_kernel,
        grid=(B, S // TM),
        in_specs=[pl.BlockSpec((1, TM, D), lambda b, i: (b, i + nct, 0)), _const_spec((1, D))],
        out_specs=pl.BlockSpec((1, TM, D), lambda b, i: (b, i, 0)),
        out_shape=jax.ShapeDtypeStruct((B, S, D), F32),
        compiler_params=_cparams(("arbitrary", "arbitrary")),
        name="final_norm",
    )(xs, g)


def _rope_tables(L, S):
    t = np.arange(S)
    pos = np.stack([t // GRID_W, t % GRID_W], axis=-1).astype(np.float32)
    inv_freq = (ROPE_THETA ** (-np.arange(0, AXIS_ROPE_DIM, 2, dtype=np.float32) / AXIS_ROPE_DIM)).astype(np.float32)
    ang = pos[:, :, None] * inv_freq[None, None, :]
    cos, sin = np.cos(ang), np.sin(ang)
    cos64 = np.concatenate([cos[:, 0], cos[:, 0], cos[:, 1], cos[:, 1]], axis=-1)
    sin64 = np.concatenate([-sin[:, 0], sin[:, 0], -sin[:, 1], sin[:, 1]], axis=-1)
    cos_t = np.concatenate([np.ones((L, HEAD_DIM), np.float32), cos64], axis=0)
    sin_t = np.concatenate([np.zeros((L, HEAD_DIM), np.float32), sin64], axis=0)
    rep = LANES // HEAD_DIM
    return jnp.asarray(np.tile(cos_t, (1, rep)), F32), jnp.asarray(np.tile(sin_t, (1, rep)), F32)


def _na_bias_table(rel_bias):
    rpt = TM // GRID_W
    half = NA_WIN_ROWS // 2
    assert half <= rpt and rpt - 1 - half + NA_WIN_ROWS <= 2 * rpt and rpt + half + 1 >= NA_WIN_ROWS
    col = np.arange(GRID_W)
    col_start = np.clip(col - NA_WIN_COLS // 2, 0, GRID_W - NA_WIN_COLS)
    kc = np.arange(GRID_W)
    inside = (kc[None, :] >= col_start[:, None]) & (kc[None, :] < col_start[:, None] + NA_WIN_COLS)
    dc = np.clip(kc[None, :] - col[:, None] + (NA_WIN_COLS - 1), 0, 2 * NA_WIN_COLS - 2)
    cmat = jnp.where(jnp.asarray(inside), rel_bias[:, :, :, dc] * LOG2E, MASK_VALUE)

    a = np.arange(rpt)[:, None]
    t = np.arange(3 * rpt)[None, :]
    first_row = np.stack([rpt + np.maximum(a - half, 0),
                          rpt + a - half,
                          np.minimum(rpt + a - half, 2 * rpt - NA_WIN_ROWS)])
    valid = (t[None] >= first_row) & (t[None] < first_row + NA_WIN_ROWS)
    dr = np.clip(t[None] - (rpt + a)[None] + (NA_WIN_ROWS - 1), 0, 2 * NA_WIN_ROWS - 2) * np.ones_like(valid)
    tab = cmat[:, :, dr]
    tab = jnp.where(jnp.asarray(valid)[None, None, :, :, :, None, None], tab, MASK_VALUE)
    tab = tab.transpose(0, 2, 1, 3, 5, 4, 6)
    return tab.reshape(-1, 3, B_HEADS, TM, 3 * TM).astype(F32)


def _block_diag_ones(n, blk):
    idx = np.arange(n) // blk
    return jnp.asarray((idx[:, None] == idx[None, :]).astype(np.float32), BF16)


def _pool_block_diag(pool_w):
    depth, g, c, d = pool_w.shape
    eye = jnp.asarray(np.eye(g, dtype=np.float32))
    out = pool_w[:, :, :, None, :] * eye[None, :, None, :, None]
    return out.reshape(depth, g * c, g * d).astype(BF16)


def kernel(x, c, ctx, c_ctx, w_mod, b_mod, norm1_g, norm2_g, w_in, q_norm_g, k_norm_g, na_rel_bias, pool_w,
           pool_scale, w_branch_a, w_branch_b, w_branch_c, w_out, ffn_w1, ffn_w3, ffn_w2, router_w, router_b,
           moe_w1, moe_w3, moe_w2, final_g):
    B, S, D = x.shape
    L = ctx.shape[1]
    T = L + S
    depth = w_mod.shape[0]
    assert D == D_MODEL and L % TM == 0 and S % TM == 0 and TM % GRID_W == 0 and B + 1 <= SUBLANES
    assert S // GRID_W >= NA_WIN_ROWS and TM // GRID_W <= NA_WIN_ROWS // 2 + 1
    nct = L // TM

    cvec = jnp.zeros((SUBLANES, D), F32).at[:B].set(c).at[B].set(c_ctx)
    mod = _mod_vectors(cvec, w_mod, b_mod)
    cos_t, sin_t = _rope_tables(L, S)
    bd = _block_diag_ones(QK_W, HEAD_DIM)
    na_bias = _na_bias_table(na_rel_bias)
    bf = lambda a: a.astype(BF16)
    w_qkv, w_gate = bf(w_in[:, :, :OFF_G]), bf(w_in[:, :, OFF_G:])
    wpa, wpb, wpc, wo, pbd = bf(w_branch_a), bf(w_branch_b), bf(w_branch_c), bf(w_out), _pool_block_diag(pool_w)
    fw1, fw3, fw2 = bf(ffn_w1), bf(ffn_w3), bf(ffn_w2)
    mw1, mw3, mw2 = bf(moe_w1), bf(moe_w3), bf(moe_w2)

    xs = jnp.concatenate([ctx, x], axis=1)
    for layer in range(depth):
        modr = mod[layer].reshape(SUBLANES, 6, D)
        g1 = norm1_g[layer].reshape(1, D)
        gqk = jnp.concatenate([jnp.tile(q_norm_g[layer], A_HEADS), jnp.tile(k_norm_g[layer], A_KV_HEADS)])
        q, kt, v, bq, bk, bv, cu = _inproj(xs, modr, g1, w_qkv, layer, bd, gqk.reshape(1, QK_W), cos_t, sin_t, nct)
        ya = _attn_a(q, kt, v, nct)
        yb = _attn_b(bq, bk, bv, na_bias, layer, nct, L)
        is_moe = layer % 2 == 1
        i = layer // 2
        router = None
        if is_moe:
            rw = jnp.zeros((D, LANES), F32).at[:, :N_EXPERTS].set(router_w[i])
            rw_hi = rw.astype(BF16)
            rw_lo = (rw - rw_hi.astype(F32)).astype(BF16)
            rb = jnp.zeros((1, LANES), F32).at[0, :N_EXPERTS].set(router_b[i])
            router = (rw_hi, rw_lo, rb)
        outs = _merge(ya, yb, cu, xs, modr, g1, w_gate, wpa, wpb, wpc, pbd, pool_scale[layer].reshape(1, C_WIDTH),
                      wo, norm2_g[layer].reshape(1, D), layer, router, nct, L)
        if is_moe:
            xs1, h2, cmb, sel = outs
            xs = _moe(h2, xs1, cmb, sel, modr[:, 5, :], mw1, mw3, mw2, i, L)
        else:
            xs1, h2 = outs
            xs = _ffn_dense(h2, xs1, modr[:, 5:6, :], fw1, fw3, fw2, i, nct)
    return _final_norm(xs, final_g.reshape(1, D), nct, S)
```

```python
import functools

import numpy as np
import jax
import jax.numpy as jnp
from jax import lax
from jax.experimental import pallas as pl
from jax.experimental.pallas import tpu as pltpu

F32 = jnp.float32
BF16 = jnp.bfloat16

D_MODEL = 1024
DEPTH = 4
GRID_W = 64
HEAD_DIM = 64
A_HEADS = 8
A_KV_HEADS = 2
A_GROUP = A_HEADS // A_KV_HEADS
B_HEADS = 4
C_GROUPS = 4
C_GROUP_DIM = 64
C_WIDTH = C_GROUPS * C_GROUP_DIM
POOL_WINDOWS = (2, 4, 8, 16)
NA_WIN_ROWS = 8
NA_WIN_COLS = 16
ROPE_THETA = 10000.0
AXIS_ROPE_DIM = HEAD_DIM // 2
N_EXPERTS = 8
EPS = 1e-6

A_Q = A_HEADS * HEAD_DIM
A_KV = A_KV_HEADS * HEAD_DIM
B_W = B_HEADS * HEAD_DIM
QK_W = A_Q + A_KV
OFF_AV = QK_W
OFF_BQ = OFF_AV + A_KV
OFF_BK = OFF_BQ + B_W
OFF_BV = OFF_BK + B_W
OFF_CU = OFF_BV + B_W
OFF_G = OFF_CU + C_WIDTH

LANES = 128
SUBLANES = 8
TM = 256
TX = 512
FC = 512
A_KBLK = 8
LOG2E = 1.4426950408889634
POOL_HALO = 8
MASK_VALUE = -1e30
VMEM_LIMIT = 56 * 1024 * 1024


def _cparams(sem):
    return pltpu.CompilerParams(dimension_semantics=sem, vmem_limit_bytes=VMEM_LIMIT)


def _dot(a, b):
    return jnp.dot(a, b, preferred_element_type=F32)


def _dot_nt(a, b):
    return lax.dot_general(a, b, (((1,), (1,)), ((), ())), preferred_element_type=F32)


def _split(a):
    hi = a.astype(BF16)
    lo = (a - hi.astype(F32)).astype(BF16)
    return hi, lo


def _dot3(a, b_hi, b_lo):
    a_hi, a_lo = _split(a)
    return _dot(a_hi, b_hi) + (_dot(a_lo, b_hi) + _dot(a_hi, b_lo))


def _const_spec(shape):
    n = len(shape)
    return pl.BlockSpec(shape, lambda *_: (0,) * n, pipeline_mode=pl.Buffered(1))


def _layer_spec(shape, layer):
    n = len(shape)
    return pl.BlockSpec((1,) + tuple(shape), lambda *_: (layer,) + (0,) * n, pipeline_mode=pl.Buffered(1))


def _rms_mod(x, g, shift, scale):
    y = x * lax.rsqrt(jnp.mean(x * x, axis=-1, keepdims=True) + EPS) * g
    return y * (1.0 + scale) + shift


def _mod_kernel(c_ref, w_ref, b_ref, o_ref):
    c = c_ref[...]
    s = c * jax.nn.sigmoid(c)
    w_hi, w_lo = _split(w_ref[0])
    o_ref[0] = _dot3(s, w_hi, w_lo) + b_ref[0]


def _mod_vectors(cvec, w_mod, b_mod):
    depth, d, n = w_mod.shape
    tn = 1536
    return pl.pallas_call(
        _mod_kernel,
        grid=(depth, n // tn),
        in_specs=[pl.BlockSpec((SUBLANES, d), lambda l, j: (0, 0)),
                  pl.BlockSpec((1, d, tn), lambda l, j: (l, 0, j)),
                  pl.BlockSpec((1, 1, tn), lambda l, j: (l, 0, j))],
        out_specs=pl.BlockSpec((1, SUBLANES, tn), lambda l, j: (l, 0, j)),
        out_shape=jax.ShapeDtypeStruct((depth, SUBLANES, n), F32),
        compiler_params=_cparams(("arbitrary", "arbitrary")),
        name="mod_vectors",
    )(cvec, w_mod, b_mod.reshape(depth, 1, n))


def _swap16(x):
    lane = lax.broadcasted_iota(jnp.int32, (1, LANES), 1)
    first = (lane % 32) < 16
    return jnp.where(first, pltpu.roll(x, LANES - 16, 1), pltpu.roll(x, 16, 1))


def _inproj_kernel(x_ref, mod_ref, g_ref, w_ref, bd_ref, gqk_ref, cos_ref, sin_ref,
                   q_ref, kt_ref, v_ref, bq_ref, bk_ref, bv_ref, cu_ref):
    m = mod_ref[0]
    h = _rms_mod(x_ref[0], g_ref[...], m[0:1], m[1:2])
    px = _dot(h.astype(BF16), w_ref[0])

    qk = px[:, :QK_W]
    sq_hi, sq_lo = _split(qk * qk)
    ss = _dot(sq_hi, bd_ref[...]) + _dot(sq_lo, bd_ref[...])
    qn = qk * lax.rsqrt(ss * (1.0 / HEAD_DIM) + EPS) * gqk_ref[...]
    cos = cos_ref[...]
    sin = sin_ref[...]
    chunks = []
    for j in range(QK_W // LANES):
        c = qn[:, j * LANES:(j + 1) * LANES]
        chunks.append(c * cos + _swap16(c) * sin)
    scale = HEAD_DIM ** -0.5
    for j in range(A_Q // LANES):
        q_ref[0, :, j * LANES:(j + 1) * LANES] = (chunks[j] * (scale * LOG2E)).astype(BF16)
    kt_ref[0, 0] = chunks[A_Q // LANES].T.astype(BF16)

    ones = jnp.ones((TM, LANES - HEAD_DIM), BF16)
    for j in range(A_KV_HEADS):
        vj = px[:, OFF_AV + j * HEAD_DIM:OFF_AV + (j + 1) * HEAD_DIM].astype(BF16)
        v_ref[0, :, j * LANES:(j + 1) * LANES] = jnp.concatenate([vj, ones], axis=1)
    bq_ref[0] = (px[:, OFF_BQ:OFF_BK] * (scale * LOG2E)).astype(BF16)
    bk_ref[0] = px[:, OFF_BK:OFF_BV].astype(BF16)
    for h in range(B_HEADS):
        vh = px[:, OFF_BV + h * HEAD_DIM:OFF_BV + (h + 1) * HEAD_DIM].astype(BF16)
        bv_ref[0, :, h * LANES:(h + 1) * LANES] = jnp.concatenate([vh, ones], axis=1)
    cu_ref[0] = px[:, OFF_CU:OFF_G]


def _inproj(xs, modr, g1, w_qkv, layer, bd, gqk, cos_t, sin_t, nct):
    B, T, D = xs.shape
    nt = T // TM
    tok = lambda w: pl.BlockSpec((1, TM, w), lambda b, i: (b, i, 0))
    out_shape = (
        jax.ShapeDtypeStruct((B, T, A_Q), BF16),
        jax.ShapeDtypeStruct((B, nt, A_KV, TM), BF16),
        jax.ShapeDtypeStruct((B, T, A_KV_HEADS * LANES), BF16),
        jax.ShapeDtypeStruct((B, T, B_W), BF16),
        jax.ShapeDtypeStruct((B, T, B_W), BF16),
        jax.ShapeDtypeStruct((B, T, B_HEADS * LANES), BF16),
        jax.ShapeDtypeStruct((B, T, C_WIDTH), F32),
    )
    return pl.pallas_call(
        _inproj_kernel,
        grid=(B, nt),
        in_specs=[tok(D),
                  pl.BlockSpec((1, 6, D), lambda b, i: (jnp.where(i < nct, B, b), 0, 0)),
                  _const_spec((1, D)),
                  _layer_spec((D, OFF_G), layer),
                  _const_spec((QK_W, QK_W)),
                  _const_spec((1, QK_W)),
                  pl.BlockSpec((TM, LANES), lambda b, i: (i, 0)),
                  pl.BlockSpec((TM, LANES), lambda b, i: (i, 0))],
        out_specs=(tok(A_Q),
                   pl.BlockSpec((1, 1, A_KV, TM), lambda b, i: (b, i, 0, 0)),
                   tok(A_KV_HEADS * LANES), tok(B_W), tok(B_W), tok(B_HEADS * LANES), tok(C_WIDTH)),
        out_shape=out_shape,
        compiler_params=_cparams(("arbitrary", "arbitrary")),
        name="inproj",
    )(xs, modr, g1, w_qkv, bd, gqk, cos_t, sin_t)


def _attn_a_kernel(q_ref, kt_ref, v_ref, o_ref, *, nct, n_steps):
    i = pl.program_id(1)
    rows = A_GROUP * TM

    def step(carry, q4, j, blk0, nblk):
        m, acc = carry
        s_list = [_dot(q4, kt_ref[0, blk0 + c, j * HEAD_DIM:(j + 1) * HEAD_DIM, :]) for c in range(nblk)]
        smax = s_list[0]
        for s in s_list[1:]:
            smax = jnp.maximum(smax, s)
        m_new = jnp.maximum(m, jnp.max(smax, axis=-1, keepdims=True))
        alpha = jnp.exp2(m - m_new)
        p = jnp.concatenate([jnp.exp2((s - m_new).astype(BF16)) for s in s_list], axis=1)
        vv = v_ref[0, pl.ds(pl.multiple_of(blk0 * TM, TM), nblk * TM), j * LANES:(j + 1) * LANES]
        return m_new, alpha * acc + _dot(p, vv)

    def run(n_main):
        for j in range(A_KV_HEADS):
            base = j * A_GROUP * HEAD_DIM
            q4 = jnp.concatenate(
                [q_ref[0, :, base + g * HEAD_DIM: base + (g + 1) * HEAD_DIM] for g in range(A_GROUP)], axis=0)
            carry = (jnp.full((rows, 1), MASK_VALUE, F32), jnp.zeros((rows, LANES), F32))
            carry = step(carry, q4, j, 0, nct)
            if n_main:
                carry = lax.fori_loop(
                    0, n_main, lambda k, cr: step(cr, q4, j, nct + k * A_KBLK, A_KBLK), carry)
            acc = carry[1]
            o = acc[:, 0:HEAD_DIM] / acc[:, HEAD_DIM:HEAD_DIM + 1]
            for g in range(A_GROUP):
                o_ref[0, :, base + g * HEAD_DIM: base + (g + 1) * HEAD_DIM] = o[g * TM:(g + 1) * TM].astype(BF16)

    @pl.when(i < nct)
    def _():
        run(0)

    @pl.when(i >= nct)
    def _():
        run(n_steps)


def _attn_a(q, kt, v, nct):
    B, T, _ = q.shape
    nt = T // TM
    assert (nt - nct) % A_KBLK == 0
    return pl.pallas_call(
        functools.partial(_attn_a_kernel, nct=nct, n_steps=(nt - nct) // A_KBLK),
        grid=(B, nt),
        in_specs=[pl.BlockSpec((1, TM, A_Q), lambda b, i: (b, i, 0)),
                  pl.BlockSpec((1, nt, A_KV, TM), lambda b, i: (b, 0, 0, 0)),
                  pl.BlockSpec((1, T, A_KV_HEADS * LANES), lambda b, i: (b, 0, 0))],
        out_specs=pl.BlockSpec((1, TM, A_Q), lambda b, i: (b, i, 0)),
        out_shape=jax.ShapeDtypeStruct((B, T, A_Q), BF16),
        compiler_params=_cparams(("arbitrary", "arbitrary")),
        name="attn_a",
    )(q, kt, v)


def _attn_b_kernel(q_ref, kp_ref, kc_ref, kn_ref, vp_ref, vc_ref, vn_ref, kx_ref, vx_ref, bias_ref,
                   o_ref, kbuf, vbuf, *, nct, L):
    i = pl.program_id(1)

    def normalised(o_ext):
        return (o_ext[:, 0:HEAD_DIM] / o_ext[:, HEAD_DIM:HEAD_DIM + 1]).astype(BF16)

    @pl.when(i < nct)
    def _():
        for h in range(B_HEADS):
            sl = slice(h * HEAD_DIM, (h + 1) * HEAD_DIM)
            vl = slice(h * LANES, (h + 1) * LANES)
            s = _dot_nt(q_ref[0, :, sl], kx_ref[0, :, sl])
            m = jnp.max(s, axis=-1, keepdims=True)
            p = jnp.exp2((s - m).astype(BF16))
            o_ref[0, :, sl] = normalised(_dot(p, vx_ref[0, :, vl]))

    @pl.when(i >= nct)
    def _():
        for buf, cx, pv, cu, nx in ((kbuf, kx_ref, kp_ref, kc_ref, kn_ref), (vbuf, vx_ref, vp_ref, vc_ref, vn_ref)):
            buf[0:L] = cx[0]
            buf[L:L + TM] = pv[0]
            buf[L + TM:L + 2 * TM] = cu[0]
            buf[L + 2 * TM:L + 3 * TM] = nx[0]
        for h in range(B_HEADS):
            sl = slice(h * HEAD_DIM, (h + 1) * HEAD_DIM)
            s = _dot_nt(q_ref[0, :, sl], kbuf[:, sl])
            s_c = s[:, 0:L]
            s_w = s[:, L:] + bias_ref[0, 0, h]
            m = jnp.maximum(jnp.max(s_w, axis=-1, keepdims=True), jnp.max(s_c, axis=-1, keepdims=True))
            p = jnp.concatenate([jnp.exp2((s_c - m).astype(BF16)), jnp.exp2((s_w - m).astype(BF16))], axis=1)
            o_ref[0, :, sl] = normalised(_dot(p, vbuf[:, h * LANES:(h + 1) * LANES]))


def _attn_b(bq, bk, bv, bias_t, layer, nct, L):
    B, T, _ = bq.shape
    nt = T // TM
    assert L % LANES == 0 and nt - nct >= 2
    VW = B_HEADS * LANES
    case = lambda i: jnp.where(i <= nct, 0, jnp.where(i == nt - 1, 2, 1))
    cur = lambda b, i: (b, i, 0)
    prev = lambda b, i: (b, jnp.maximum(i - 1, nct), 0)
    nxt = lambda b, i: (b, jnp.minimum(i + 1, nt - 1), 0)
    blk = lambda f: pl.BlockSpec((1, TM, B_W), f)
    vblk = lambda f: pl.BlockSpec((1, TM, VW), f)
    ctx = lambda w: pl.BlockSpec((1, L, w), lambda b, i: (b, 0, 0))
    return pl.pallas_call(
        functools.partial(_attn_b_kernel, nct=nct, L=L),
        grid=(B, nt),
        in_specs=[blk(cur), blk(prev), blk(cur), blk(nxt), vblk(prev), vblk(cur), vblk(nxt), ctx(B_W), ctx(VW),
                  pl.BlockSpec((1, 1) + bias_t.shape[2:], lambda b, i: (layer, case(i), 0, 0, 0))],
        out_specs=blk(cur),
        out_shape=jax.ShapeDtypeStruct((B, T, B_W), BF16),
        scratch_shapes=[pltpu.VMEM((L + 3 * TM, B_W), BF16), pltpu.VMEM((L + 3 * TM, VW), BF16)],
        compiler_params=_cparams(("arbitrary", "arbitrary")),
        name="attn_b",
    )(bq, bk, bk, bk, bv, bv, bv, bk, bv, bias_t)


def _merge_kernel(*refs, nct, nt, L, S, with_router):
    if with_router:
        (ya_ref, yb_ref, cup_ref, cu_ref, cun_ref, x_ref, mod_ref, g1_ref, wg_ref, wpa_ref, wpb_ref, wpc_ref,
         pbd_ref, psc_ref, wo_ref, g2_ref, rwh_ref, rwl_ref, rb_ref,
         xo_ref, h_ref, cmb_ref, sel_ref, e_scr) = refs
    else:
        (ya_ref, yb_ref, cup_ref, cu_ref, cun_ref, x_ref, mod_ref, g1_ref, wg_ref, wpa_ref, wpb_ref, wpc_ref,
         pbd_ref, psc_ref, wo_ref, g2_ref, xo_ref, h_ref, e_scr) = refs
    i = pl.program_id(1)
    D = D_MODEL
    m = mod_ref[0]
    x_in = x_ref[0]

    at_start = (i == 0) | (i == nct)
    at_end = (i == nct - 1) | (i == nt - 1)
    u = cu_ref[0]
    e_scr[0:POOL_HALO] = jnp.where(at_start, 0.0, cup_ref[0])
    e_scr[POOL_HALO:POOL_HALO + TM] = u
    e_scr[POOL_HALO + TM:] = jnp.where(at_end, 0.0, cun_ref[0])
    sh = lambda k: e_scr[POOL_HALO + k:POOL_HALO + k + TM]
    w2 = sh(-1) + sh(0)
    w4 = w2 + (sh(-2) + sh(1))
    w8 = w4 + ((sh(-4) + sh(-3)) + (sh(2) + sh(3)))
    w16 = w8 + (((sh(-8) + sh(-7)) + (sh(-6) + sh(-5))) + ((sh(4) + sh(5)) + (sh(6) + sh(7))))
    grp = lax.broadcasted_iota(jnp.int32, (1, C_WIDTH), 1) // C_GROUP_DIM
    half = jnp.left_shift(1, grp)
    t_loc = lax.broadcasted_iota(jnp.int32, (TM, 1), 0)
    t_seq = jnp.where(i < nct, i * TM, (i - nct) * TM) + t_loc
    n_seq = jnp.where(i < nct, L, S)
    cnt = jnp.minimum(t_seq + half, n_seq) - jnp.maximum(t_seq - half, 0)
    wsum = jnp.where(grp == 0, w2, jnp.where(grp == 1, w4, jnp.where(grp == 2, w8, w16)))
    dlt = wsum / cnt.astype(F32) - u
    yc = _dot(dlt.astype(BF16), pbd_ref[0]) * psc_ref[...]

    h1 = _rms_mod(x_in, g1_ref[...], m[0:1], m[1:2]).astype(BF16)
    branches = (ya_ref[0], yb_ref[0], yc.astype(BF16))
    weights = (wpa_ref, wpb_ref, wpc_ref)
    mrg = None
    for k in range(3):
        gate = jax.nn.sigmoid(_dot(h1, wg_ref[0, :, k * D:(k + 1) * D]))
        term = gate * _dot(branches[k], weights[k][0])
        mrg = term if mrg is None else mrg + term
    x = x_in + m[2:3] * _dot(mrg.astype(BF16), wo_ref[0])
    xo_ref[0] = x
    h = _rms_mod(x, g2_ref[...], m[3:4], m[4:5])
    h_ref[0] = h.astype(BF16)

    if with_router:
        lane = lax.broadcasted_iota(jnp.int32, (1, LANES), 1).astype(F32)
        lg = _dot3(h, rwh_ref[...], rwl_ref[...]) + rb_ref[...]
        lg = jnp.where(lane < N_EXPERTS, lg, -jnp.inf)
        m1 = jnp.max(lg, axis=-1, keepdims=True)
        i1 = jnp.min(jnp.where(lg == m1, lane, float(LANES)), axis=-1, keepdims=True)
        mask1 = lane == i1
        lg2 = jnp.where(mask1, -jnp.inf, lg)
        m2 = jnp.max(lg2, axis=-1, keepdims=True)
        i2 = jnp.min(jnp.where(lg2 == m2, lane, float(LANES)), axis=-1, keepdims=True)
        mask2 = lane == i2
        e2 = jnp.exp(m2 - m1)
        den = 1.0 + e2
        cmb_ref[0] = jnp.where(mask1, 1.0 / den, 0.0) + jnp.where(mask2, e2 / den, 0.0)
        sel_ref[0] = jnp.where(mask1 | mask2, 1.0, 0.0)


def _merge(ya, yb, cu, xs, modr, g1, w_gate, wpa, wpb, wpc, pbd, psc, wo, g2, layer, router, nct, L):
    B, T, D = xs.shape
    nt = T // TM
    S = T - L
    hb = TM // POOL_HALO
    tok = lambda w: pl.BlockSpec((1, TM, w), lambda b, i: (b, i, 0))
    in_specs = [tok(A_Q), tok(B_W),
                pl.BlockSpec((1, POOL_HALO, C_WIDTH), lambda b, i: (b, jnp.maximum(i * hb - 1, 0), 0)),
                tok(C_WIDTH),
                pl.BlockSpec((1, POOL_HALO, C_WIDTH), lambda b, i: (b, jnp.minimum((i + 1) * hb, nt * hb - 1), 0)),
                tok(D),
                pl.BlockSpec((1, 6, D), lambda b, i: (jnp.where(i < nct, B, b), 0, 0)),
                _const_spec((1, D)),
                _layer_spec((D, 3 * D), layer),
                _layer_spec((A_Q, D), layer), _layer_spec((B_W, D), layer), _layer_spec((C_WIDTH, D), layer),
                _layer_spec((C_WIDTH, C_WIDTH), layer), _const_spec((1, C_WIDTH)), _layer_spec((D, D), layer),
                _const_spec((1, D))]
    args = [ya, yb, cu, cu, cu, xs, modr, g1, w_gate, wpa, wpb, wpc, pbd, psc, wo, g2]
    out_specs = [tok(D), tok(D)]
    out_shape = [jax.ShapeDtypeStruct((B, T, D), F32), jax.ShapeDtypeStruct((B, T, D), BF16)]
    if router is not None:
        in_specs += [_const_spec((D, LANES)), _const_spec((D, LANES)), _const_spec((1, LANES))]
        args += list(router)
        out_specs += [tok(LANES), tok(LANES)]
        out_shape += [jax.ShapeDtypeStruct((B, T, LANES), F32)] * 2
    return pl.pallas_call(
        functools.partial(_merge_kernel, nct=nct, nt=nt, L=L, S=S, with_router=router is not None),
        grid=(B, nt),
        in_specs=in_specs,
        out_specs=tuple(out_specs),
        out_shape=tuple(out_shape),
        scratch_shapes=[pltpu.VMEM((TM + 2 * POOL_HALO, C_WIDTH), F32)],
        compiler_params=_cparams(("arbitrary", "arbitrary")),
        name="merge_router" if router is not None else "merge",
    )(*args)


def _swiglu_acc(h, w1_ref, w3_ref, w2_ref, lead):
    d_ff = w1_ref.shape[-1]
    acc = None
    for f0 in range(0, d_ff, FC):
        f1 = min(f0 + FC, d_ff)
        a = _dot(h, w1_ref[lead + (slice(None), slice(f0, f1))])
        b = _dot(h, w3_ref[lead + (slice(None), slice(f0, f1))])
        t = (a * jax.nn.sigmoid(a) * b).astype(BF16)
        part = _dot(t, w2_ref[lead + (slice(f0, f1), slice(None))])
        acc = part if acc is None else acc + part
    return acc


def _ffn_kernel(h_ref, x_ref, g_ref, w1_ref, w3_ref, w2_ref, o_ref):
    f = _swiglu_acc(h_ref[0], w1_ref, w3_ref, w2_ref, (0,))
    o_ref[0] = x_ref[0] + g_ref[0] * f


def _ffn_dense(h2, xs, gate2, w1, w3, w2, idx, nct):
    B, T, D = xs.shape
    F = w1.shape[-1]
    nt = T // TM
    tok = lambda: pl.BlockSpec((1, TM, D), lambda b, i: (b, i, 0))
    return pl.pallas_call(
        _ffn_kernel,
        grid=(B, nt),
        in_specs=[tok(), tok(),
                  pl.BlockSpec((1, 1, D), lambda b, i: (jnp.where(i < nct, B, b), 0, 0)),
                  _layer_spec((D, F), idx), _layer_spec((D, F), idx), _layer_spec((F, D), idx)],
        out_specs=tok(),
        out_shape=jax.ShapeDtypeStruct((B, T, D), F32),
        compiler_params=_cparams(("arbitrary", "arbitrary")),
        name="ffn_dense",
    )(h2, xs, gate2, w1, w3, w2)


def _gather_kernel(wj_ref, ws_ref, wf_ref, wl_ref, wv_ref, te_ref, pos_ref, cmb_ref, h_ref,
                   o_ref, pw_ref, acc_ref, accw_ref):
    w = pl.program_id(0)

    @pl.when(wf_ref[w] == 1)
    def _():
        acc_ref[...] = jnp.zeros_like(acc_ref)
        accw_ref[...] = jnp.zeros_like(accw_ref)

    @pl.when(wv_ref[w] == 1)
    def _():
        row = lax.broadcasted_iota(jnp.int32, (TX, 1), 0) + wj_ref[w] * TX
        hit = pos_ref[0, 0] == row
        acc_ref[...] += _dot(jnp.where(hit, 1.0, 0.0).astype(BF16), h_ref[...])
        accw_ref[...] += jnp.sum(jnp.where(hit, cmb_ref[0, 0], 0.0), axis=-1, keepdims=True)

    @pl.when(wl_ref[w] == 1)
    def _():
        o_ref[...] = acc_ref[...].astype(BF16)
        pw_ref[...] = accw_ref[...]


def _moe_gather(work, tile_expert, pos_t, cmb_t, h2f, n_tiles):
    wj, ws, wf, wl, wv = work
    n, D = h2f.shape
    row_spec = pl.BlockSpec((1, 1, 1, TX), lambda w, wj, ws, wf, wl, wv, te: (te[wj[w]], ws[w], 0, 0))
    return pl.pallas_call(
        _gather_kernel,
        grid_spec=pltpu.PrefetchScalarGridSpec(
            num_scalar_prefetch=6,
            grid=(wj.shape[0],),
            in_specs=[row_spec, row_spec,
                      pl.BlockSpec((TX, D), lambda w, wj, ws, wf, wl, wv, te: (ws[w], 0))],
            out_specs=(pl.BlockSpec((TX, D), lambda w, wj, ws, wf, wl, wv, te: (wj[w], 0)),
                       pl.BlockSpec((TX, 1), lambda w, wj, ws, wf, wl, wv, te: (wj[w], 0))),
            scratch_shapes=[pltpu.VMEM((TX, D), F32), pltpu.VMEM((TX, 1), F32)]),
        out_shape=(jax.ShapeDtypeStruct((n_tiles * TX, D), BF16),
                   jax.ShapeDtypeStruct((n_tiles * TX, 1), F32)),
        compiler_params=_cparams(("arbitrary",)),
        name="moe_gather",
    )(wj, ws, wf, wl, wv, tile_expert, pos_t, cmb_t, h2f)


def _gffn_kernel(te_ref, tv_ref, x_ref, pw_ref, w1_ref, w3_ref, w2_ref, o_ref):
    j = pl.program_id(0)

    @pl.when(tv_ref[j] == 1)
    def _():
        f = _swiglu_acc(x_ref[...], w1_ref, w3_ref, w2_ref, (0, 0))
        o_ref[...] = (f * pw_ref[...]).astype(BF16)

    @pl.when(tv_ref[j] == 0)
    def _():
        o_ref[...] = jnp.zeros_like(o_ref)


def _moe_gffn(tile_expert, tile_valid, xg, pw, w1, w3, w2, idx):
    P, D = xg.shape
    F = w1.shape[-1]
    wspec = lambda shape: pl.BlockSpec((1, 1) + shape, lambda j, te, tv: (idx, te[j], 0, 0),
                                       pipeline_mode=pl.Buffered(1))
    return pl.pallas_call(
        _gffn_kernel,
        grid_spec=pltpu.PrefetchScalarGridSpec(
            num_scalar_prefetch=2,
            grid=(P // TX,),
            in_specs=[pl.BlockSpec((TX, D), lambda j, te, tv: (j, 0)),
                      pl.BlockSpec((TX, 1), lambda j, te, tv: (j, 0)),
                      wspec((D, F)), wspec((D, F)), wspec((F, D))],
            out_specs=pl.BlockSpec((TX, D), lambda j, te, tv: (j, 0))),
        out_shape=jax.ShapeDtypeStruct((P, D), BF16),
        compiler_params=_cparams(("arbitrary",)),
        name="moe_gffn",
    )(tile_expert, tile_valid, xg, pw, w1, w3, w2)


def _combine_kernel(wt_ref, ws_ref, wf_ref, wl_ref, wv_ref, pos_ref, y_ref, x_ref, g_ref, o_ref, acc_ref,
                    *, n_batch, T, L):
    w = pl.program_id(0)

    @pl.when(wf_ref[w] == 1)
    def _():
        acc_ref[...] = jnp.zeros_like(acc_ref)

    @pl.when(wv_ref[w] == 1)
    def _():
        col = lax.broadcasted_iota(jnp.int32, (1, TX), 1) + ws_ref[w] * TX
        pos = pos_ref[...]
        hit = (pos[:, 0:1] == col) | (pos[:, 1:2] == col)
        acc_ref[...] += _dot(jnp.where(hit, 1.0, 0.0).astype(BF16), y_ref[...])

    @pl.when(wl_ref[w] == 1)
    def _():
        n = lax.broadcasted_iota(jnp.int32, (TX, 1), 0) + wt_ref[w] * TX
        b = jnp.zeros((TX, 1), jnp.int32)
        for k in range(1, n_batch):
            b = b + (n >= k * T).astype(jnp.int32)
        rowid = jnp.where(n - b * T < L, n_batch, b)
        gate = jnp.zeros((TX, D_MODEL), F32)
        for r in range(n_batch + 1):
            gate = jnp.where(rowid == r, g_ref[r:r + 1, :], gate)
        o_ref[...] = x_ref[...] + gate * acc_ref[...]


def _moe_combine(work, pos2, yw, xf, gate2, n_batch, T, L):
    wt, ws, wf, wl, wv = work
    n, D = xf.shape
    return pl.pallas_call(
        functools.partial(_combine_kernel, n_batch=n_batch, T=T, L=L),
        grid_spec=pltpu.PrefetchScalarGridSpec(
            num_scalar_prefetch=5,
            grid=(wt.shape[0],),
            in_specs=[pl.BlockSpec((TX, 2), lambda w, wt, ws, wf, wl, wv: (wt[w], 0)),
                      pl.BlockSpec((TX, D), lambda w, wt, ws, wf, wl, wv: (ws[w], 0)),
                      pl.BlockSpec((TX, D), lambda w, wt, ws, wf, wl, wv: (wt[w], 0)),
                      pl.BlockSpec((SUBLANES, D), lambda w, wt, ws, wf, wl, wv: (0, 0))],
            out_specs=pl.BlockSpec((TX, D), lambda w, wt, ws, wf, wl, wv: (wt[w], 0)),
            scratch_shapes=[pltpu.VMEM((TX, D), F32)]),
        out_shape=jax.ShapeDtypeStruct((n, D), F32),
        compiler_params=_cparams(("arbitrary",)),
        name="moe_combine",
    )(wt, ws, wf, wl, wv, pos2, yw, xf, gate2)


def _count_below(sorted_vals, x):
    return jnp.sum((sorted_vals[None, :] < x[:, None]).astype(jnp.int32), axis=1)


def _with_flags(g, it, compute, n_work, total):
    live = jnp.arange(n_work, dtype=jnp.int32) < total
    g_prev = jnp.concatenate([jnp.full((1,), -1, jnp.int32), g[:-1]])
    g_next = jnp.concatenate([g[1:], jnp.full((1,), -1, jnp.int32)])
    live_next = jnp.concatenate([live[1:], jnp.zeros((1,), bool)])
    first = live & (g != g_prev)
    last = live & ((g != g_next) | ~live_next)
    i32 = lambda a: a.astype(jnp.int32)
    return i32(g), i32(it), i32(first), i32(last), i32(live & compute)


def _moe_plan(sel, n_tok):
    E = N_EXPERTS
    sel = sel.astype(jnp.int32)
    cnt = jnp.sum(sel, axis=0)
    rank = jnp.cumsum(sel, axis=0) - 1
    gsz = ((cnt + TX - 1) // TX) * TX
    gend = jnp.cumsum(gsz)
    goff = gend - gsz
    n_tiles = (2 * n_tok + E * (TX - 1) + TX - 1) // TX
    P = n_tiles * TX
    pos = jnp.where(sel == 1, goff[None, :] + rank, -1)
    tile_start = jnp.arange(n_tiles, dtype=jnp.int32) * TX
    tile_valid = tile_start < gend[-1]
    tile_expert = jnp.minimum(_count_below(gend, tile_start + 1), E - 1)
    pmax_tok = jnp.max(pos, axis=1)
    psec_tok = jnp.max(jnp.where(pos == pmax_tok[:, None], -1, pos), axis=1)
    pos2 = jnp.stack([psec_tok, pmax_tok], axis=1)

    n_src_tiles = n_tok // TX
    cin = rank[TX - 1::TX] + 1
    ra = tile_start - goff[tile_expert]
    rb = ra + jnp.clip(cnt[tile_expert] - ra, 0, TX) - 1
    cin_t = cin.T[tile_expert]
    lo = jnp.sum((cin_t <= ra[:, None]).astype(jnp.int32), axis=1)
    hi = jnp.sum((cin_t <= rb[:, None]).astype(jnp.int32), axis=1)
    span = jnp.where(tile_valid, hi - lo + 1, 1)
    n_gw = E * n_src_tiles + n_tiles
    cs = jnp.cumsum(span)
    total = cs[-1]
    w = jnp.minimum(jnp.arange(n_gw, dtype=jnp.int32), total - 1)
    jw = _count_below(cs, w + 1)
    sw = jnp.where(tile_valid[jw], lo[jw] + (w - (cs[jw] - span[jw])), 0)
    gwork = _with_flags(jw, sw, tile_valid[jw], n_gw, total)

    pt = pos.reshape(n_src_tiles, TX, E)
    pmax = jnp.max(pt, axis=1)
    pmin = jnp.min(jnp.where(pt >= 0, pt, P), axis=1)
    ta = pmin // TX
    tb = pmax // TX
    cand_item = jnp.stack([ta, tb], axis=-1).reshape(-1)
    cand_valid = jnp.stack([pmax >= 0, (pmax >= 0) & (tb != ta)], axis=-1).reshape(-1).astype(jnp.int32)
    cand_group = jnp.repeat(jnp.arange(n_src_tiles, dtype=jnp.int32), 2 * E)
    n_cw = E * n_src_tiles + n_tiles
    ccs = jnp.cumsum(cand_valid)
    ctotal = ccs[-1]
    cw = jnp.minimum(jnp.arange(n_cw, dtype=jnp.int32), ctotal - 1)
    cidx = _count_below(ccs, cw + 1)
    cwork = _with_flags(cand_group[cidx], cand_item[cidx], jnp.ones((n_cw,), bool), n_cw, ctotal)
    return pos, pos2, tile_expert, tile_valid.astype(jnp.int32), gwork, cwork, n_tiles


def _moe(h2, xs1, cmb, sel, gate2, w1, w3, w2, idx, L):
    B, T, D = xs1.shape
    n_tok = B * T
    E = N_EXPERTS
    assert n_tok % TX == 0
    selm = sel.reshape(n_tok, LANES)[:, :E] > 0.5
    cmbm = cmb.reshape(n_tok, LANES)[:, :E]
    pos, pos2, tile_expert, tile_valid, gwork, cwork, n_tiles = _moe_plan(selm, n_tok)
    pos_t = pos.T.reshape(E, n_tok // TX, 1, TX)
    cmb_t = cmbm.T.reshape(E, n_tok // TX, 1, TX)

    xg, pw = _moe_gather(gwork, tile_expert, pos_t, cmb_t, h2.reshape(n_tok, D), n_tiles)
    yw = _moe_gffn(tile_expert, tile_valid, xg, pw, w1, w3, w2, idx)
    out = _moe_combine(cwork, pos2, yw, xs1.reshape(n_tok, D), gate2, B, T, L)
    return out.reshape(B, T, D)


def _final_kernel(x_ref, g_ref, o_ref):
    x = x_ref[0]
    o_ref[0] = x * lax.rsqrt(jnp.mean(x * x, axis=-1, keepdims=True) + EPS) * g_ref[...]


def _final_norm(xs, g, nct, S):
    B, T, D = xs.shape
    return pl.pallas_call(
        _final_kernel,
        grid=(B, S // TM),
        in_specs=[pl.BlockSpec((1, TM, D), lambda b, i: (b, i + nct, 0)), _const_spec((1, D))],
        out_specs=pl.BlockSpec((1, TM, D), lambda b, i: (b, i, 0)),
        out_shape=jax.ShapeDtypeStruct((B, S, D), F32),
        compiler_params=_cparams(("arbitrary", "arbitrary")),
        name="final_norm",
    )(xs, g)


def _rope_tables(L, S):
    t = np.arange(S)
    pos = np.stack([t // GRID_W, t % GRID_W], axis=-1).astype(np.float32)
    inv_freq = (ROPE_THETA ** (-np.arange(0, AXIS_ROPE_DIM, 2, dtype=np.float32) / AXIS_ROPE_DIM)).astype(np.float32)
    ang = pos[:, :, None] * inv_freq[None, None, :]
    cos, sin = np.cos(ang), np.sin(ang)
    cos64 = np.concatenate([cos[:, 0], cos[:, 0], cos[:, 1], cos[:, 1]], axis=-1)
    sin64 = np.concatenate([-sin[:, 0], sin[:, 0], -sin[:, 1], sin[:, 1]], axis=-1)
    cos_t = np.concatenate([np.ones((L, HEAD_DIM), np.float32), cos64], axis=0)
    sin_t = np.concatenate([np.zeros((L, HEAD_DIM), np.float32), sin64], axis=0)
    rep = LANES // HEAD_DIM
    return jnp.asarray(np.tile(cos_t, (1, rep)), F32), jnp.asarray(np.tile(sin_t, (1, rep)), F32)


def _na_bias_table(rel_bias):
    rpt = TM // GRID_W
    half = NA_WIN_ROWS // 2
    assert half <= rpt and rpt - 1 - half + NA_WIN_ROWS <= 2 * rpt and rpt + half + 1 >= NA_WIN_ROWS
    col = np.arange(GRID_W)
    col_start = np.clip(col - NA_WIN_COLS // 2, 0, GRID_W - NA_WIN_COLS)
    kc = np.arange(GRID_W)
    inside = (kc[None, :] >= col_start[:, None]) & (kc[None, :] < col_start[:, None] + NA_WIN_COLS)
    relp = jnp.pad(rel_bias.astype(F32), ((0, 0), (0, 0), (0, 0), (GRID_W, GRID_W)))
    off = GRID_W + NA_WIN_COLS - 1
    cmat = jnp.stack([relp[..., off - qc:off - qc + GRID_W] for qc in range(GRID_W)], axis=3)
    cmat = jnp.where(jnp.asarray(inside), cmat * LOG2E, MASK_VALUE)
    masked = jnp.full(cmat.shape[:2] + (GRID_W, GRID_W), MASK_VALUE, F32)

    cases = []
    for case in range(3):
        rows = []
        for a in range(rpt):
            first = (rpt + max(a - half, 0),
                     rpt + a - half,
                     min(rpt + a - half, 2 * rpt - NA_WIN_ROWS))[case]
            blocks = [cmat[:, :, t - (rpt + a) + NA_WIN_ROWS - 1] if first <= t < first + NA_WIN_ROWS else masked
                      for t in range(3 * rpt)]
            rows.append(jnp.concatenate(blocks, axis=-1))
        cases.append(jnp.concatenate(rows, axis=-2))
    return jnp.stack(cases, axis=1)


def _block_diag_ones(n, blk):
    idx = np.arange(n) // blk
    return jnp.asarray((idx[:, None] == idx[None, :]).astype(np.float32), BF16)


def _pool_block_diag(pool_w):
    depth, g, c, d = pool_w.shape
    eye = jnp.asarray(np.eye(g, dtype=np.float32))
    out = pool_w[:, :, :, None, :] * eye[None, :, None, :, None]
    return out.reshape(depth, g * c, g * d).astype(BF16)


def kernel(x, c, ctx, c_ctx, w_mod, b_mod, norm1_g, norm2_g, w_in, q_norm_g, k_norm_g, na_rel_bias, pool_w,
           pool_scale, w_branch_a, w_branch_b, w_branch_c, w_out, ffn_w1, ffn_w3, ffn_w2, router_w, router_b,
           moe_w1, moe_w3, moe_w2, final_g):
    B, S, D = x.shape
    L = ctx.shape[1]
    T = L + S
    depth = w_mod.shape[0]
    assert D == D_MODEL and L % TM == 0 and S % TM == 0 and TM % GRID_W == 0 and B + 1 <= SUBLANES
    assert S // GRID_W >= NA_WIN_ROWS and TM // GRID_W <= NA_WIN_ROWS // 2 + 1
    nct = L // TM

    cvec = jnp.zeros((SUBLANES, D), F32).at[:B].set(c).at[B].set(c_ctx)
    mod = _mod_vectors(cvec, w_mod, b_mod)
    cos_t, sin_t = _rope_tables(L, S)
    bd = _block_diag_ones(QK_W, HEAD_DIM)
    na_bias = _na_bias_table(na_rel_bias)
    bf = lambda a: a.astype(BF16)
    w_qkv, w_gate = bf(w_in[:, :, :OFF_G]), bf(w_in[:, :, OFF_G:])
    wpa, wpb, wpc, wo, pbd = bf(w_branch_a), bf(w_branch_b), bf(w_branch_c), bf(w_out), _pool_block_diag(pool_w)
    fw1, fw3, fw2 = bf(ffn_w1), bf(ffn_w3), bf(ffn_w2)
    mw1, mw3, mw2 = bf(moe_w1), bf(moe_w3), bf(moe_w2)

    xs = jnp.concatenate([ctx, x], axis=1)
    for layer in range(depth):
        modr = mod[layer].reshape(SUBLANES, 6, D)
        g1 = norm1_g[layer].reshape(1, D)
        gqk = jnp.concatenate([jnp.tile(q_norm_g[layer], A_HEADS), jnp.tile(k_norm_g[layer], A_KV_HEADS)])
        q, kt, v, bq, bk, bv, cu = _inproj(xs, modr, g1, w_qkv, layer, bd, gqk.reshape(1, QK_W), cos_t, sin_t, nct)
        ya = _attn_a(q, kt, v, nct)
        yb = _attn_b(bq, bk, bv, na_bias, layer, nct, L)
        is_moe = layer % 2 == 1
        i = layer // 2
        router = None
        if is_moe:
            rw = jnp.zeros((D, LANES), F32).at[:, :N_EXPERTS].set(router_w[i])
            rw_hi = rw.astype(BF16)
            rw_lo = (rw - rw_hi.astype(F32)).astype(BF16)
            rb = jnp.zeros((1, LANES), F32).at[0, :N_EXPERTS].set(router_b[i])
            router = (rw_hi, rw_lo, rb)
        outs = _merge(ya, yb, cu, xs, modr, g1, w_gate, wpa, wpb, wpc, pbd, pool_scale[layer].reshape(1, C_WIDTH),
                      wo, norm2_g[layer].reshape(1, D), layer, router, nct, L)
        if is_moe:
            xs1, h2, cmb, sel = outs
            xs = _moe(h2, xs1, cmb, sel, modr[:, 5, :], mw1, mw3, mw2, i, L)
        else:
            xs1, h2 = outs
            xs = _ffn_dense(h2, xs1, modr[:, 5:6, :], fw1, fw3, fw2, i, nct)
    return _final_norm(xs, final_g.reshape(1, D), nct, S)
```

```python
import functools

import numpy as np
import jax
import jax.numpy as jnp
from jax import lax
from jax.experimental import pallas as pl
from jax.experimental.pallas import tpu as pltpu

F32 = jnp.float32
BF16 = jnp.bfloat16

D_MODEL = 1024
DEPTH = 4
GRID_W = 64
HEAD_DIM = 64
A_HEADS = 8
A_KV_HEADS = 2
A_GROUP = A_HEADS // A_KV_HEADS
B_HEADS = 4
C_GROUPS = 4
C_GROUP_DIM = 64
C_WIDTH = C_GROUPS * C_GROUP_DIM
POOL_WINDOWS = (2, 4, 8, 16)
NA_WIN_ROWS = 8
NA_WIN_COLS = 16
ROPE_THETA = 10000.0
AXIS_ROPE_DIM = HEAD_DIM // 2
N_EXPERTS = 8
EPS = 1e-6

A_Q = A_HEADS * HEAD_DIM
A_KV = A_KV_HEADS * HEAD_DIM
B_W = B_HEADS * HEAD_DIM
QK_W = A_Q + A_KV
OFF_AV = QK_W
OFF_BQ = OFF_AV + A_KV
OFF_BK = OFF_BQ + B_W
OFF_BV = OFF_BK + B_W
OFF_CU = OFF_BV + B_W
OFF_G = OFF_CU + C_WIDTH

LANES = 128
SUBLANES = 8
TM = 256
TX = 512
FC = 512
A_KBLK = 8
LOG2E = 1.4426950408889634
POOL_HALO = 8
MASK_VALUE = -1e30
VMEM_LIMIT = 56 * 1024 * 1024


def _cparams(sem):
    return pltpu.CompilerParams(dimension_semantics=sem, vmem_limit_bytes=VMEM_LIMIT)


def _dot(a, b):
    return jnp.dot(a, b, preferred_element_type=F32)


def _dot_nt(a, b):
    return lax.dot_general(a, b, (((1,), (1,)), ((), ())), preferred_element_type=F32)


def _split(a):
    hi = a.astype(BF16)
    lo = (a - hi.astype(F32)).astype(BF16)
    return hi, lo


def _dot3(a, b_hi, b_lo):
    a_hi, a_lo = _split(a)
    return _dot(a_hi, b_hi) + (_dot(a_lo, b_hi) + _dot(a_hi, b_lo))


def _const_spec(shape):
    n = len(shape)
    return pl.BlockSpec(shape, lambda *_: (0,) * n, pipeline_mode=pl.Buffered(1))


def _layer_spec(shape, layer):
    n = len(shape)
    return pl.BlockSpec((1,) + tuple(shape), lambda *_: (layer,) + (0,) * n, pipeline_mode=pl.Buffered(1))


def _rms_mod(x, g, shift, scale):
    y = x * lax.rsqrt(jnp.mean(x * x, axis=-1, keepdims=True) + EPS) * g
    return y * (1.0 + scale) + shift


def _mod_kernel(c_ref, w_ref, b_ref, o_ref):
    c = c_ref[...]
    s = c * jax.nn.sigmoid(c)
    w_hi, w_lo = _split(w_ref[0])
    o_ref[0] = _dot3(s, w_hi, w_lo) + b_ref[0]


def _mod_vectors(cvec, w_mod, b_mod):
    depth, d, n = w_mod.shape
    tn = 1536
    return pl.pallas_call(
        _mod_kernel,
        grid=(depth, n // tn),
        in_specs=[pl.BlockSpec((SUBLANES, d), lambda l, j: (0, 0)),
                  pl.BlockSpec((1, d, tn), lambda l, j: (l, 0, j)),
                  pl.BlockSpec((1, 1, tn), lambda l, j: (l, 0, j))],
        out_specs=pl.BlockSpec((1, SUBLANES, tn), lambda l, j: (l, 0, j)),
        out_shape=jax.ShapeDtypeStruct((depth, SUBLANES, n), F32),
        compiler_params=_cparams(("arbitrary", "arbitrary")),
        name="mod_vectors",
    )(cvec, w_mod, b_mod.reshape(depth, 1, n))


def _swap16(x):
    lane = lax.broadcasted_iota(jnp.int32, (1, LANES), 1)
    first = (lane % 32) < 16
    return jnp.where(first, pltpu.roll(x, LANES - 16, 1), pltpu.roll(x, 16, 1))


def _inproj_kernel(x_ref, mod_ref, g_ref, w_ref, bd_ref, gqk_ref, cos_ref, sin_ref,
                   q_ref, kt_ref, v_ref, bq_ref, bk_ref, bv_ref, cu_ref):
    m = mod_ref[0]
    h = _rms_mod(x_ref[0], g_ref[...], m[0:1], m[1:2])
    px = _dot(h.astype(BF16), w_ref[0])

    qk = px[:, :QK_W]
    sq_hi, sq_lo = _split(qk * qk)
    ss = _dot(sq_hi, bd_ref[...]) + _dot(sq_lo, bd_ref[...])
    qn = qk * lax.rsqrt(ss * (1.0 / HEAD_DIM) + EPS) * gqk_ref[...]
    cos = cos_ref[...]
    sin = sin_ref[...]
    chunks = []
    for j in range(QK_W // LANES):
        c = qn[:, j * LANES:(j + 1) * LANES]
        chunks.append(c * cos + _swap16(c) * sin)
    scale = HEAD_DIM ** -0.5
    for j in range(A_Q // LANES):
        q_ref[0, :, j * LANES:(j + 1) * LANES] = (chunks[j] * (scale * LOG2E)).astype(BF16)
    kt_ref[0, 0] = chunks[A_Q // LANES].T.astype(BF16)

    ones = jnp.ones((TM, LANES - HEAD_DIM), BF16)
    for j in range(A_KV_HEADS):
        vj = px[:, OFF_AV + j * HEAD_DIM:OFF_AV + (j + 1) * HEAD_DIM].astype(BF16)
        v_ref[0, :, j * LANES:(j + 1) * LANES] = jnp.concatenate([vj, ones], axis=1)
    bq_ref[0] = (px[:, OFF_BQ:OFF_BK] * (scale * LOG2E)).astype(BF16)
    bk_ref[0] = px[:, OFF_BK:OFF_BV].astype(BF16)
    for h in range(B_HEADS):
        vh = px[:, OFF_BV + h * HEAD_DIM:OFF_BV + (h + 1) * HEAD_DIM].astype(BF16)
        bv_ref[0, :, h * LANES:(h + 1) * LANES] = jnp.concatenate([vh, ones], axis=1)
    cu_ref[0] = px[:, OFF_CU:OFF_G]


def _inproj(xs, modr, g1, w_qkv, layer, bd, gqk, cos_t, sin_t, nct):
    B, T, D = xs.shape
    nt = T // TM
    tok = lambda w: pl.BlockSpec((1, TM, w), lambda b, i: (b, i, 0))
    out_shape = (
        jax.ShapeDtypeStruct((B, T, A_Q), BF16),
        jax.ShapeDtypeStruct((B, nt, A_KV, TM), BF16),
        jax.ShapeDtypeStruct((B, T, A_KV_HEADS * LANES), BF16),
        jax.ShapeDtypeStruct((B, T, B_W), BF16),
        jax.ShapeDtypeStruct((B, T, B_W), BF16),
        jax.ShapeDtypeStruct((B, T, B_HEADS * LANES), BF16),
        jax.ShapeDtypeStruct((B, T, C_WIDTH), F32),
    )
    return pl.pallas_call(
        _inproj_kernel,
        grid=(B, nt),
        in_specs=[tok(D),
                  pl.BlockSpec((1, 6, D), lambda b, i: (jnp.where(i < nct, B, b), 0, 0)),
                  _const_spec((1, D)),
                  _layer_spec((D, OFF_G), layer),
                  _const_spec((QK_W, QK_W)),
                  _const_spec((1, QK_W)),
                  pl.BlockSpec((TM, LANES), lambda b, i: (i, 0)),
                  pl.BlockSpec((TM, LANES), lambda b, i: (i, 0))],
        out_specs=(tok(A_Q),
                   pl.BlockSpec((1, 1, A_KV, TM), lambda b, i: (b, i, 0, 0)),
                   tok(A_KV_HEADS * LANES), tok(B_W), tok(B_W), tok(B_HEADS * LANES), tok(C_WIDTH)),
        out_shape=out_shape,
        compiler_params=_cparams(("arbitrary", "arbitrary")),
        name="inproj",
    )(xs, modr, g1, w_qkv, bd, gqk, cos_t, sin_t)


def _attn_a_kernel(q_ref, kt_ref, v_ref, o_ref, *, nct, n_steps):
    i = pl.program_id(1)
    rows = A_GROUP * TM

    def step(carry, q4, j, blk0, nblk):
        m, acc = carry
        s_list = [_dot(q4, kt_ref[0, blk0 + c, j * HEAD_DIM:(j + 1) * HEAD_DIM, :]) for c in range(nblk)]
        smax = s_list[0]
        for s in s_list[1:]:
            smax = jnp.maximum(smax, s)
        m_new = jnp.maximum(m, jnp.max(smax, axis=-1, keepdims=True))
        alpha = jnp.exp2(m - m_new)
        p = jnp.concatenate([jnp.exp2((s - m_new).astype(BF16)) for s in s_list], axis=1)
        vv = v_ref[0, pl.ds(pl.multiple_of(blk0 * TM, TM), nblk * TM), j * LANES:(j + 1) * LANES]
        return m_new, alpha * acc + _dot(p, vv)

    def run(n_main):
        for j in range(A_KV_HEADS):
            base = j * A_GROUP * HEAD_DIM
            q4 = jnp.concatenate(
                [q_ref[0, :, base + g * HEAD_DIM: base + (g + 1) * HEAD_DIM] for g in range(A_GROUP)], axis=0)
            carry = (jnp.full((rows, 1), MASK_VALUE, F32), jnp.zeros((rows, LANES), F32))
            if n_main:
                carry = step(carry, q4, j, 0, nct + A_KBLK)
                carry = lax.fori_loop(
                    1, n_main, lambda k, cr: step(cr, q4, j, nct + k * A_KBLK, A_KBLK), carry)
            else:
                carry = step(carry, q4, j, 0, nct)
            acc = carry[1]
            o = acc[:, 0:HEAD_DIM] / acc[:, HEAD_DIM:HEAD_DIM + 1]
            for g in range(A_GROUP):
                o_ref[0, :, base + g * HEAD_DIM: base + (g + 1) * HEAD_DIM] = o[g * TM:(g + 1) * TM].astype(BF16)

    @pl.when(i < nct)
    def _():
        run(0)

    @pl.when(i >= nct)
    def _():
        run(n_steps)


def _attn_a(q, kt, v, nct):
    B, T, _ = q.shape
    nt = T // TM
    assert (nt - nct) % A_KBLK == 0
    return pl.pallas_call(
        functools.partial(_attn_a_kernel, nct=nct, n_steps=(nt - nct) // A_KBLK),
        grid=(B, nt),
        in_specs=[pl.BlockSpec((1, TM, A_Q), lambda b, i: (b, i, 0)),
                  pl.BlockSpec((1, nt, A_KV, TM), lambda b, i: (b, 0, 0, 0)),
                  pl.BlockSpec((1, T, A_KV_HEADS * LANES), lambda b, i: (b, 0, 0))],
        out_specs=pl.BlockSpec((1, TM, A_Q), lambda b, i: (b, i, 0)),
        out_shape=jax.ShapeDtypeStruct((B, T, A_Q), BF16),
        compiler_params=_cparams(("arbitrary", "arbitrary")),
        name="attn_a",
    )(q, kt, v)


def _attn_b_kernel(q_ref, kp_ref, kc_ref, kn_ref, vp_ref, vc_ref, vn_ref, kx_ref, vx_ref, bias_ref,
                   o_ref, kbuf, vbuf, *, nct, L):
    i = pl.program_id(1)

    def normalised(o_ext):
        return (o_ext[:, 0:HEAD_DIM] / o_ext[:, HEAD_DIM:HEAD_DIM + 1]).astype(BF16)

    @pl.when(i < nct)
    def _():
        for h in range(B_HEADS):
            sl = slice(h * HEAD_DIM, (h + 1) * HEAD_DIM)
            vl = slice(h * LANES, (h + 1) * LANES)
            s = _dot_nt(q_ref[0, :, sl], kx_ref[0, :, sl])
            m = jnp.max(s, axis=-1, keepdims=True)
            p = jnp.exp2((s - m).astype(BF16))
            o_ref[0, :, sl] = normalised(_dot(p, vx_ref[0, :, vl]))

    @pl.when(i >= nct)
    def _():
        for buf, cx, pv, cu, nx in ((kbuf, kx_ref, kp_ref, kc_ref, kn_ref), (vbuf, vx_ref, vp_ref, vc_ref, vn_ref)):
            buf[0:L] = cx[0]
            buf[L:L + TM] = pv[0]
            buf[L + TM:L + 2 * TM] = cu[0]
            buf[L + 2 * TM:L + 3 * TM] = nx[0]
        for h in range(B_HEADS):
            sl = slice(h * HEAD_DIM, (h + 1) * HEAD_DIM)
            s = _dot_nt(q_ref[0, :, sl], kbuf[:, sl])
            s_c = s[:, 0:L]
            s_w = s[:, L:] + bias_ref[0, 0, h]
            m = jnp.maximum(jnp.max(s_w, axis=-1, keepdims=True), jnp.max(s_c, axis=-1, keepdims=True))
            p = jnp.concatenate([jnp.exp2((s_c - m).astype(BF16)), jnp.exp2((s_w - m).astype(BF16))], axis=1)
            o_ref[0, :, sl] = normalised(_dot(p, vbuf[:, h * LANES:(h + 1) * LANES]))


def _attn_b(bq, bk, bv, bias_t, layer, nct, L):
    B, T, _ = bq.shape
    nt = T // TM
    assert L % LANES == 0 and nt - nct >= 2
    VW = B_HEADS * LANES
    case = lambda i: jnp.where(i <= nct, 0, jnp.where(i == nt - 1, 2, 1))
    cur = lambda b, i: (b, i, 0)
    prev = lambda b, i: (b, jnp.maximum(i - 1, nct), 0)
    nxt = lambda b, i: (b, jnp.minimum(i + 1, nt - 1), 0)
    blk = lambda f: pl.BlockSpec((1, TM, B_W), f)
    vblk = lambda f: pl.BlockSpec((1, TM, VW), f)
    ctx = lambda w: pl.BlockSpec((1, L, w), lambda b, i: (b, 0, 0))
    return pl.pallas_call(
        functools.partial(_attn_b_kernel, nct=nct, L=L),
        grid=(B, nt),
        in_specs=[blk(cur), blk(prev), blk(cur), blk(nxt), vblk(prev), vblk(cur), vblk(nxt), ctx(B_W), ctx(VW),
                  pl.BlockSpec((1, 1) + bias_t.shape[2:], lambda b, i: (layer, case(i), 0, 0, 0))],
        out_specs=blk(cur),
        out_shape=jax.ShapeDtypeStruct((B, T, B_W), BF16),
        scratch_shapes=[pltpu.VMEM((L + 3 * TM, B_W), BF16), pltpu.VMEM((L + 3 * TM, VW), BF16)],
        compiler_params=_cparams(("arbitrary", "arbitrary")),
        name="attn_b",
    )(bq, bk, bk, bk, bv, bv, bv, bk, bv, bias_t)


def _merge_kernel(*refs, nct, nt, L, S, with_router):
    if with_router:
        (ya_ref, yb_ref, cup_ref, cu_ref, cun_ref, x_ref, mod_ref, g1_ref, wg_ref, wpa_ref, wpb_ref, wpc_ref,
         pbd_ref, psc_ref, wo_ref, g2_ref, rw_ref, rb_ref,
         xo_ref, h_ref, cmb_ref, sel_ref, e_scr) = refs
    else:
        (ya_ref, yb_ref, cup_ref, cu_ref, cun_ref, x_ref, mod_ref, g1_ref, wg_ref, wpa_ref, wpb_ref, wpc_ref,
         pbd_ref, psc_ref, wo_ref, g2_ref, xo_ref, h_ref, e_scr) = refs
    i = pl.program_id(1)
    D = D_MODEL
    m = mod_ref[0]
    x_in = x_ref[0]

    at_start = (i == 0) | (i == nct)
    at_end = (i == nct - 1) | (i == nt - 1)
    u = cu_ref[0]
    e_scr[0:POOL_HALO] = jnp.where(at_start, 0.0, cup_ref[0])
    e_scr[POOL_HALO:POOL_HALO + TM] = u
    e_scr[POOL_HALO + TM:] = jnp.where(at_end, 0.0, cun_ref[0])
    sh = lambda k: e_scr[POOL_HALO + k:POOL_HALO + k + TM]
    w2 = sh(-1) + sh(0)
    w4 = w2 + (sh(-2) + sh(1))
    w8 = w4 + ((sh(-4) + sh(-3)) + (sh(2) + sh(3)))
    w16 = w8 + (((sh(-8) + sh(-7)) + (sh(-6) + sh(-5))) + ((sh(4) + sh(5)) + (sh(6) + sh(7))))
    grp = lax.broadcasted_iota(jnp.int32, (1, C_WIDTH), 1) // C_GROUP_DIM
    half = jnp.left_shift(1, grp)
    t_loc = lax.broadcasted_iota(jnp.int32, (TM, 1), 0)
    t_seq = jnp.where(i < nct, i * TM, (i - nct) * TM) + t_loc
    n_seq = jnp.where(i < nct, L, S)
    cnt = jnp.minimum(t_seq + half, n_seq) - jnp.maximum(t_seq - half, 0)
    wsum = jnp.where(grp == 0, w2, jnp.where(grp == 1, w4, jnp.where(grp == 2, w8, w16)))
    dlt = wsum / cnt.astype(F32) - u
    yc = _dot(dlt.astype(BF16), pbd_ref[0]) * psc_ref[...]

    h1 = _rms_mod(x_in, g1_ref[...], m[0:1], m[1:2]).astype(BF16)
    branches = (ya_ref[0], yb_ref[0], yc.astype(BF16))
    weights = (wpa_ref, wpb_ref, wpc_ref)
    mrg = None
    for k in range(3):
        gate = jax.nn.sigmoid(_dot(h1, wg_ref[0, :, k * D:(k + 1) * D]))
        term = gate * _dot(branches[k], weights[k][0])
        mrg = term if mrg is None else mrg + term
    x = x_in + m[2:3] * _dot(mrg.astype(BF16), wo_ref[0])
    xo_ref[0] = x
    h = _rms_mod(x, g2_ref[...], m[3:4], m[4:5])
    h_ref[0] = h.astype(BF16)

    if with_router:
        lane = lax.broadcasted_iota(jnp.int32, (1, LANES), 1).astype(F32)
        h_hi, h_lo = _split(h)
        d_hi = _dot(h_hi, rw_ref[...])
        d_lo = _dot(h_lo, rw_ref[...])
        lg = d_hi[:, :LANES] + (d_hi[:, LANES:] + d_lo[:, :LANES]) + rb_ref[...]
        lg = jnp.where(lane < N_EXPERTS, lg, -jnp.inf)
        m1 = jnp.max(lg, axis=-1, keepdims=True)
        i1 = jnp.min(jnp.where(lg == m1, lane, float(LANES)), axis=-1, keepdims=True)
        mask1 = lane == i1
        lg2 = jnp.where(mask1, -jnp.inf, lg)
        m2 = jnp.max(lg2, axis=-1, keepdims=True)
        i2 = jnp.min(jnp.where(lg2 == m2, lane, float(LANES)), axis=-1, keepdims=True)
        mask2 = lane == i2
        e2 = jnp.exp(m2 - m1)
        den = 1.0 + e2
        cmb_ref[0] = jnp.where(mask1, 1.0 / den, 0.0) + jnp.where(mask2, e2 / den, 0.0)
        sel_ref[0] = jnp.where(mask1 | mask2, 1.0, 0.0)


def _merge(ya, yb, cu, xs, modr, g1, w_gate, wpa, wpb, wpc, pbd, psc, wo, g2, layer, router, nct, L):
    B, T, D = xs.shape
    nt = T // TM
    S = T - L
    hb = TM // POOL_HALO
    tok = lambda w: pl.BlockSpec((1, TM, w), lambda b, i: (b, i, 0))
    in_specs = [tok(A_Q), tok(B_W),
                pl.BlockSpec((1, POOL_HALO, C_WIDTH), lambda b, i: (b, jnp.maximum(i * hb - 1, 0), 0)),
                tok(C_WIDTH),
                pl.BlockSpec((1, POOL_HALO, C_WIDTH), lambda b, i: (b, jnp.minimum((i + 1) * hb, nt * hb - 1), 0)),
                tok(D),
                pl.BlockSpec((1, 6, D), lambda b, i: (jnp.where(i < nct, B, b), 0, 0)),
                _const_spec((1, D)),
                _layer_spec((D, 3 * D), layer),
                _layer_spec((A_Q, D), layer), _layer_spec((B_W, D), layer), _layer_spec((C_WIDTH, D), layer),
                _layer_spec((C_WIDTH, C_WIDTH), layer), _const_spec((1, C_WIDTH)), _layer_spec((D, D), layer),
                _const_spec((1, D))]
    args = [ya, yb, cu, cu, cu, xs, modr, g1, w_gate, wpa, wpb, wpc, pbd, psc, wo, g2]
    out_specs = [tok(D), tok(D)]
    out_shape = [jax.ShapeDtypeStruct((B, T, D), F32), jax.ShapeDtypeStruct((B, T, D), BF16)]
    if router is not None:
        in_specs += [_const_spec((D, 2 * LANES)), _const_spec((1, LANES))]
        args += list(router)
        out_specs += [tok(LANES), tok(LANES)]
        out_shape += [jax.ShapeDtypeStruct((B, T, LANES), F32)] * 2
    return pl.pallas_call(
        functools.partial(_merge_kernel, nct=nct, nt=nt, L=L, S=S, with_router=router is not None),
        grid=(B, nt),
        in_specs=in_specs,
        out_specs=tuple(out_specs),
        out_shape=tuple(out_shape),
        scratch_shapes=[pltpu.VMEM((TM + 2 * POOL_HALO, C_WIDTH), F32)],
        compiler_params=_cparams(("arbitrary", "arbitrary")),
        name="merge_router" if router is not None else "merge",
    )(*args)


def _swiglu_acc(h, w1_ref, w3_ref, w2_ref, lead):
    d_ff = w1_ref.shape[-1]
    acc = None
    for f0 in range(0, d_ff, FC):
        f1 = min(f0 + FC, d_ff)
        a = _dot(h, w1_ref[lead + (slice(None), slice(f0, f1))])
        b = _dot(h, w3_ref[lead + (slice(None), slice(f0, f1))])
        t = (a * jax.nn.sigmoid(a) * b).astype(BF16)
        part = _dot(t, w2_ref[lead + (slice(f0, f1), slice(None))])
        acc = part if acc is None else acc + part
    return acc


def _ffn_kernel(h_ref, x_ref, g_ref, w1_ref, w3_ref, w2_ref, o_ref):
    f = _swiglu_acc(h_ref[0], w1_ref, w3_ref, w2_ref, (0,))
    o_ref[0] = x_ref[0] + g_ref[0] * f


def _ffn_dense(h2, xs, gate2, w1, w3, w2, idx, nct):
    B, T, D = xs.shape
    F = w1.shape[-1]
    nt = T // TM
    tok = lambda: pl.BlockSpec((1, TM, D), lambda b, i: (b, i, 0))
    return pl.pallas_call(
        _ffn_kernel,
        grid=(B, nt),
        in_specs=[tok(), tok(),
                  pl.BlockSpec((1, 1, D), lambda b, i: (jnp.where(i < nct, B, b), 0, 0)),
                  _layer_spec((D, F), idx), _layer_spec((D, F), idx), _layer_spec((F, D), idx)],
        out_specs=tok(),
        out_shape=jax.ShapeDtypeStruct((B, T, D), F32),
        compiler_params=_cparams(("arbitrary", "arbitrary")),
        name="ffn_dense",
    )(h2, xs, gate2, w1, w3, w2)


def _gather_kernel(wj_ref, ws_ref, wf_ref, wl_ref, wv_ref, te_ref, pos_ref, cmb_ref, h_ref,
                   o_ref, pw_ref, acc_ref, accw_ref):
    w = pl.program_id(0)

    @pl.when(wf_ref[w] == 1)
    def _():
        acc_ref[...] = jnp.zeros_like(acc_ref)
        accw_ref[...] = jnp.zeros_like(accw_ref)

    @pl.when(wv_ref[w] == 1)
    def _():
        row = lax.broadcasted_iota(jnp.int32, (TX, 1), 0) + wj_ref[w] * TX
        hit = pos_ref[0, 0] == row
        acc_ref[...] += _dot(jnp.where(hit, 1.0, 0.0).astype(BF16), h_ref[...])
        accw_ref[...] += jnp.sum(jnp.where(hit, cmb_ref[0, 0], 0.0), axis=-1, keepdims=True)

    @pl.when(wl_ref[w] == 1)
    def _():
        o_ref[...] = acc_ref[...].astype(BF16)
        pw_ref[...] = accw_ref[...]


def _moe_gather(work, tile_expert, pos_t, cmb_t, h2f, n_tiles):
    wj, ws, wf, wl, wv = work
    n, D = h2f.shape
    row_spec = pl.BlockSpec((1, 1, 1, TX), lambda w, wj, ws, wf, wl, wv, te: (te[wj[w]], ws[w], 0, 0))
    return pl.pallas_call(
        _gather_kernel,
        grid_spec=pltpu.PrefetchScalarGridSpec(
            num_scalar_prefetch=6,
            grid=(wj.shape[0],),
            in_specs=[row_spec, row_spec,
                      pl.BlockSpec((TX, D), lambda w, wj, ws, wf, wl, wv, te: (ws[w], 0))],
            out_specs=(pl.BlockSpec((TX, D), lambda w, wj, ws, wf, wl, wv, te: (wj[w], 0)),
                       pl.BlockSpec((TX, 1), lambda w, wj, ws, wf, wl, wv, te: (wj[w], 0))),
            scratch_shapes=[pltpu.VMEM((TX, D), F32), pltpu.VMEM((TX, 1), F32)]),
        out_shape=(jax.ShapeDtypeStruct((n_tiles * TX, D), BF16),
                   jax.ShapeDtypeStruct((n_tiles * TX, 1), F32)),
        compiler_params=_cparams(("arbitrary",)),
        name="moe_gather",
    )(wj, ws, wf, wl, wv, tile_expert, pos_t, cmb_t, h2f)


def _gffn_kernel(te_ref, tv_ref, x_ref, pw_ref, w1_ref, w3_ref, w2_ref, o_ref):
    j = pl.program_id(0)

    @pl.when(tv_ref[j] == 1)
    def _():
        f = _swiglu_acc(x_ref[...], w1_ref, w3_ref, w2_ref, (0, 0))
        o_ref[...] = (f * pw_ref[...]).astype(BF16)

    @pl.when(tv_ref[j] == 0)
    def _():
        o_ref[...] = jnp.zeros_like(o_ref)


def _moe_gffn(tile_expert, tile_valid, xg, pw, w1, w3, w2, idx):
    P, D = xg.shape
    F = w1.shape[-1]
    wspec = lambda shape: pl.BlockSpec((1, 1) + shape, lambda j, te, tv: (idx, te[j], 0, 0),
                                       pipeline_mode=pl.Buffered(1))
    return pl.pallas_call(
        _gffn_kernel,
        grid_spec=pltpu.PrefetchScalarGridSpec(
            num_scalar_prefetch=2,
            grid=(P // TX,),
            in_specs=[pl.BlockSpec((TX, D), lambda j, te, tv: (j, 0)),
                      pl.BlockSpec((TX, 1), lambda j, te, tv: (j, 0)),
                      wspec((D, F)), wspec((D, F)), wspec((F, D))],
            out_specs=pl.BlockSpec((TX, D), lambda j, te, tv: (j, 0))),
        out_shape=jax.ShapeDtypeStruct((P, D), BF16),
        compiler_params=_cparams(("arbitrary",)),
        name="moe_gffn",
    )(tile_expert, tile_valid, xg, pw, w1, w3, w2)


def _combine_kernel(wt_ref, ws_ref, wf_ref, wl_ref, wv_ref, pos_ref, y_ref, x_ref, g_ref, o_ref, acc_ref,
                    *, n_batch, T, L):
    w = pl.program_id(0)

    @pl.when(wf_ref[w] == 1)
    def _():
        acc_ref[...] = jnp.zeros_like(acc_ref)

    @pl.when(wv_ref[w] == 1)
    def _():
        col = lax.broadcasted_iota(jnp.int32, (1, TX), 1) + ws_ref[w] * TX
        pos = pos_ref[...]
        hit = (pos[:, 0:1] == col) | (pos[:, 1:2] == col)
        acc_ref[...] += _dot(jnp.where(hit, 1.0, 0.0).astype(BF16), y_ref[...])

    @pl.when(wl_ref[w] == 1)
    def _():
        n = lax.broadcasted_iota(jnp.int32, (TX, 1), 0) + wt_ref[w] * TX
        b = jnp.zeros((TX, 1), jnp.int32)
        for k in range(1, n_batch):
            b = b + (n >= k * T).astype(jnp.int32)
        rowid = jnp.where(n - b * T < L, n_batch, b)
        gate = jnp.zeros((TX, D_MODEL), F32)
        for r in range(n_batch + 1):
            gate = jnp.where(rowid == r, g_ref[r:r + 1, :], gate)
        o_ref[...] = x_ref[...] + gate * acc_ref[...]


def _moe_combine(work, pos2, yw, xf, gate2, n_batch, T, L):
    wt, ws, wf, wl, wv = work
    n, D = xf.shape
    return pl.pallas_call(
        functools.partial(_combine_kernel, n_batch=n_batch, T=T, L=L),
        grid_spec=pltpu.PrefetchScalarGridSpec(
            num_scalar_prefetch=5,
            grid=(wt.shape[0],),
            in_specs=[pl.BlockSpec((TX, 2), lambda w, wt, ws, wf, wl, wv: (wt[w], 0)),
                      pl.BlockSpec((TX, D), lambda w, wt, ws, wf, wl, wv: (ws[w], 0)),
                      pl.BlockSpec((TX, D), lambda w, wt, ws, wf, wl, wv: (wt[w], 0)),
                      pl.BlockSpec((SUBLANES, D), lambda w, wt, ws, wf, wl, wv: (0, 0))],
            out_specs=pl.BlockSpec((TX, D), lambda w, wt, ws, wf, wl, wv: (wt[w], 0)),
            scratch_shapes=[pltpu.VMEM((TX, D), F32)]),
        out_shape=jax.ShapeDtypeStruct((n, D), F32),
        compiler_params=_cparams(("arbitrary",)),
        name="moe_combine",
    )(wt, ws, wf, wl, wv, pos2, yw, xf, gate2)


def _count_below(sorted_vals, x):
    return jnp.sum((sorted_vals[None, :] < x[:, None]).astype(jnp.int32), axis=1)


def _with_flags(g, it, compute, n_work, total):
    live = jnp.arange(n_work, dtype=jnp.int32) < total
    g_prev = jnp.concatenate([jnp.full((1,), -1, jnp.int32), g[:-1]])
    g_next = jnp.concatenate([g[1:], jnp.full((1,), -1, jnp.int32)])
    live_next = jnp.concatenate([live[1:], jnp.zeros((1,), bool)])
    first = live & (g != g_prev)
    last = live & ((g != g_next) | ~live_next)
    i32 = lambda a: a.astype(jnp.int32)
    return i32(g), i32(it), i32(first), i32(last), i32(live & compute)


def _moe_plan(sel, n_tok):
    E = N_EXPERTS
    sel = sel.astype(jnp.int32)
    cnt = jnp.sum(sel, axis=0)
    rank = jnp.cumsum(sel, axis=0) - 1
    gsz = ((cnt + TX - 1) // TX) * TX
    gend = jnp.cumsum(gsz)
    goff = gend - gsz
    n_tiles = (2 * n_tok + E * (TX - 1) + TX - 1) // TX
    P = n_tiles * TX
    pos = jnp.where(sel == 1, goff[None, :] + rank, -1)
    tile_start = jnp.arange(n_tiles, dtype=jnp.int32) * TX
    tile_valid = tile_start < gend[-1]
    tile_expert = jnp.minimum(_count_below(gend, tile_start + 1), E - 1)
    pmax_tok = jnp.max(pos, axis=1)
    psec_tok = jnp.max(jnp.where(pos == pmax_tok[:, None], -1, pos), axis=1)
    pos2 = jnp.stack([psec_tok, pmax_tok], axis=1)

    n_src_tiles = n_tok // TX
    cin = rank[TX - 1::TX] + 1
    ra = tile_start - goff[tile_expert]
    rb = ra + jnp.clip(cnt[tile_expert] - ra, 0, TX) - 1
    cin_t = cin.T[tile_expert]
    lo = jnp.sum((cin_t <= ra[:, None]).astype(jnp.int32), axis=1)
    hi = jnp.sum((cin_t <= rb[:, None]).astype(jnp.int32), axis=1)
    span = jnp.where(tile_valid, hi - lo + 1, 1)
    n_gw = E * n_src_tiles + n_tiles
    cs = jnp.cumsum(span)
    total = cs[-1]
    w = jnp.minimum(jnp.arange(n_gw, dtype=jnp.int32), total - 1)
    jw = _count_below(cs, w + 1)
    sw = jnp.where(tile_valid[jw], lo[jw] + (w - (cs[jw] - span[jw])), 0)
    gwork = _with_flags(jw, sw, tile_valid[jw], n_gw, total)

    pt = pos.reshape(n_src_tiles, TX, E)
    pmax = jnp.max(pt, axis=1)
    pmin = jnp.min(jnp.where(pt >= 0, pt, P), axis=1)
    ta = pmin // TX
    tb = pmax // TX
    cand_item = jnp.stack([ta, tb], axis=-1).reshape(-1)
    cand_valid = jnp.stack([pmax >= 0, (pmax >= 0) & (tb != ta)], axis=-1).reshape(-1).astype(jnp.int32)
    cand_group = jnp.repeat(jnp.arange(n_src_tiles, dtype=jnp.int32), 2 * E)
    n_cw = E * n_src_tiles + n_tiles
    ccs = jnp.cumsum(cand_valid)
    ctotal = ccs[-1]
    cw = jnp.minimum(jnp.arange(n_cw, dtype=jnp.int32), ctotal - 1)
    cidx = _count_below(ccs, cw + 1)
    cwork = _with_flags(cand_group[cidx], cand_item[cidx], jnp.ones((n_cw,), bool), n_cw, ctotal)
    return pos, pos2, tile_expert, tile_valid.astype(jnp.int32), gwork, cwork, n_tiles


def _moe(h2, xs1, cmb, sel, gate2, w1, w3, w2, idx, L):
    B, T, D = xs1.shape
    n_tok = B * T
    E = N_EXPERTS
    assert n_tok % TX == 0
    selm = sel.reshape(n_tok, LANES)[:, :E] > 0.5
    cmbm = cmb.reshape(n_tok, LANES)[:, :E]
    pos, pos2, tile_expert, tile_valid, gwork, cwork, n_tiles = _moe_plan(selm, n_tok)
    pos_t = pos.T.reshape(E, n_tok // TX, 1, TX)
    cmb_t = cmbm.T.reshape(E, n_tok // TX, 1, TX)

    xg, pw = _moe_gather(gwork, tile_expert, pos_t, cmb_t, h2.reshape(n_tok, D), n_tiles)
    yw = _moe_gffn(tile_expert, tile_valid, xg, pw, w1, w3, w2, idx)
    out = _moe_combine(cwork, pos2, yw, xs1.reshape(n_tok, D), gate2, B, T, L)
    return out.reshape(B, T, D)


def _final_kernel(x_ref, g_ref, o_ref):
    x = x_ref[0]
    o_ref[0] = x * lax.rsqrt(jnp.mean(x * x, axis=-1, keepdims=True) + EPS) * g_ref[...]


def _final_norm(xs, g, nct, S):
    B, T, D = xs.shape
    return pl.pallas_call(
        _final_kernel,
        grid=(B, S // TM),
        in_specs=[pl.BlockSpec((1, TM, D), lambda b, i: (b, i + nct, 0)), _const_spec((1, D))],
        out_specs=pl.BlockSpec((1, TM, D), lambda b, i: (b, i, 0)),
        out_shape=jax.ShapeDtypeStruct((B, S, D), F32),
        compiler_params=_cparams(("arbitrary", "arbitrary")),
        name="final_norm",
    )(xs, g)


def _rope_tables(L, S):
    t = np.arange(S)
    pos = np.stack([t // GRID_W, t % GRID_W], axis=-1).astype(np.float32)
    inv_freq = (ROPE_THETA ** (-np.arange(0, AXIS_ROPE_DIM, 2, dtype=np.float32) / AXIS_ROPE_DIM)).astype(np.float32)
    ang = pos[:, :, None] * inv_freq[None, None, :]
    cos, sin = np.cos(ang), np.sin(ang)
    cos64 = np.concatenate([cos[:, 0], cos[:, 0], cos[:, 1], cos[:, 1]], axis=-1)
    sin64 = np.concatenate([-sin[:, 0], sin[:, 0], -sin[:, 1], sin[:, 1]], axis=-1)
    cos_t = np.concatenate([np.ones((L, HEAD_DIM), np.float32), cos64], axis=0)
    sin_t = np.concatenate([np.zeros((L, HEAD_DIM), np.float32), sin64], axis=0)
    rep = LANES // HEAD_DIM
    return jnp.asarray(np.tile(cos_t, (1, rep)), F32), jnp.asarray(np.tile(sin_t, (1, rep)), F32)


def _na_bias_table(rel_bias):
    rpt = TM // GRID_W
    half = NA_WIN_ROWS // 2
    assert half <= rpt and rpt - 1 - half + NA_WIN_ROWS <= 2 * rpt and rpt + half + 1 >= NA_WIN_ROWS
    col = np.arange(GRID_W)
    col_start = np.clip(col - NA_WIN_COLS // 2, 0, GRID_W - NA_WIN_COLS)
    kc = np.arange(GRID_W)
    inside = (kc[None, :] >= col_start[:, None]) & (kc[None, :] < col_start[:, None] + NA_WIN_COLS)
    relp = jnp.pad(rel_bias.astype(F32), ((0, 0), (0, 0), (0, 0), (GRID_W, GRID_W)))
    off = GRID_W + NA_WIN_COLS - 1
    cmat = jnp.stack([relp[..., off - qc:off - qc + GRID_W] for qc in range(GRID_W)], axis=3)
    cmat = jnp.where(jnp.asarray(inside), cmat * LOG2E, MASK_VALUE)
    masked = jnp.full(cmat.shape[:2] + (GRID_W, GRID_W), MASK_VALUE, F32)

    cases = []
    for case in range(3):
        rows = []
        for a in range(rpt):
            first = (rpt + max(a - half, 0),
                     rpt + a - half,
                     min(rpt + a - half, 2 * rpt - NA_WIN_ROWS))[case]
            blocks = [cmat[:, :, t - (rpt + a) + NA_WIN_ROWS - 1] if first <= t < first + NA_WIN_ROWS else masked
                      for t in range(3 * rpt)]
            rows.append(jnp.concatenate(blocks, axis=-1))
        cases.append(jnp.concatenate(rows, axis=-2))
    return jnp.stack(cases, axis=1)


def _block_diag_ones(n, blk):
    idx = np.arange(n) // blk
    return jnp.asarray((idx[:, None] == idx[None, :]).astype(np.float32), BF16)


def _pool_block_diag(pool_w):
    depth, g, c, d = pool_w.shape
    eye = jnp.asarray(np.eye(g, dtype=np.float32))
    out = pool_w[:, :, :, None, :] * eye[None, :, None, :, None]
    return out.reshape(depth, g * c, g * d).astype(BF16)


def kernel(x, c, ctx, c_ctx, w_mod, b_mod, norm1_g, norm2_g, w_in, q_norm_g, k_norm_g, na_rel_bias, pool_w,
           pool_scale, w_branch_a, w_branch_b, w_branch_c, w_out, ffn_w1, ffn_w3, ffn_w2, router_w, router_b,
           moe_w1, moe_w3, moe_w2, final_g):
    B, S, D = x.shape
    L = ctx.shape[1]
    T = L + S
    depth = w_mod.shape[0]
    assert D == D_MODEL and L % TM == 0 and S % TM == 0 and TM % GRID_W == 0 and B + 1 <= SUBLANES
    assert S // GRID_W >= NA_WIN_ROWS and TM // GRID_W <= NA_WIN_ROWS // 2 + 1
    nct = L // TM

    cvec = jnp.zeros((SUBLANES, D), F32).at[:B].set(c).at[B].set(c_ctx)
    mod = _mod_vectors(cvec, w_mod, b_mod)
    cos_t, sin_t = _rope_tables(L, S)
    bd = _block_diag_ones(QK_W, HEAD_DIM)
    na_bias = _na_bias_table(na_rel_bias)
    bf = lambda a: a.astype(BF16)
    w_qkv, w_gate = bf(w_in[:, :, :OFF_G]), bf(w_in[:, :, OFF_G:])
    wpa, wpb, wpc, wo, pbd = bf(w_branch_a), bf(w_branch_b), bf(w_branch_c), bf(w_out), _pool_block_diag(pool_w)
    fw1, fw3, fw2 = bf(ffn_w1), bf(ffn_w3), bf(ffn_w2)
    mw1, mw3, mw2 = bf(moe_w1), bf(moe_w3), bf(moe_w2)

    xs = jnp.concatenate([ctx, x], axis=1)
    for layer in range(depth):
        modr = mod[layer].reshape(SUBLANES, 6, D)
        g1 = norm1_g[layer].reshape(1, D)
        gqk = jnp.concatenate([jnp.tile(q_norm_g[layer], A_HEADS), jnp.tile(k_norm_g[layer], A_KV_HEADS)])
        q, kt, v, bq, bk, bv, cu = _inproj(xs, modr, g1, w_qkv, layer, bd, gqk.reshape(1, QK_W), cos_t, sin_t, nct)
        ya = _attn_a(q, kt, v, nct)
        yb = _attn_b(bq, bk, bv, na_bias, layer, nct, L)
        is_moe = layer % 2 == 1
        i = layer // 2
        router = None
        if is_moe:
            rw = jnp.zeros((D, LANES), F32).at[:, :N_EXPERTS].set(router_w[i])
            rw_hi = rw.astype(BF16)
            rw_lo = (rw - rw_hi.astype(F32)).astype(BF16)
            rb = jnp.zeros((1, LANES), F32).at[0, :N_EXPERTS].set(router_b[i])
            router = (jnp.concatenate([rw_hi, rw_lo], axis=1), rb)
        outs = _merge(ya, yb, cu, xs, modr, g1, w_gate, wpa, wpb, wpc, pbd, pool_scale[layer].reshape(1, C_WIDTH),
                      wo, norm2_g[layer].reshape(1, D), layer, router, nct, L)
        if is_moe:
            xs1, h2, cmb, sel = outs
            xs = _moe(h2, xs1, cmb, sel, modr[:, 5, :], mw1, mw3, mw2, i, L)
        else:
            xs1, h2 = outs
            xs = _ffn_dense(h2, xs1, modr[:, 5:6, :], fw1, fw3, fw2, i, nct)
    return _final_norm(xs, final_g.reshape(1, D), nct, S)
```

```python
import functools

import numpy as np
import jax
import jax.numpy as jnp
from jax import lax
from jax.experimental import pallas as pl
from jax.experimental.pallas import tpu as pltpu

F32 = jnp.float32
BF16 = jnp.bfloat16

D_MODEL = 1024
DEPTH = 4
GRID_W = 64
HEAD_DIM = 64
A_HEADS = 8
A_KV_HEADS = 2
A_GROUP = A_HEADS // A_KV_HEADS
B_HEADS = 4
C_GROUPS = 4
C_GROUP_DIM = 64
C_WIDTH = C_GROUPS * C_GROUP_DIM
POOL_WINDOWS = (2, 4, 8, 16)
NA_WIN_ROWS = 8
NA_WIN_COLS = 16
ROPE_THETA = 10000.0
AXIS_ROPE_DIM = HEAD_DIM // 2
N_EXPERTS = 8
EPS = 1e-6

A_Q = A_HEADS * HEAD_DIM
A_KV = A_KV_HEADS * HEAD_DIM
B_W = B_HEADS * HEAD_DIM
QK_W = A_Q + A_KV
OFF_AV = QK_W
OFF_BQ = OFF_AV + A_KV
OFF_BK = OFF_BQ + B_W
OFF_BV = OFF_BK + B_W
OFF_CU = OFF_BV + B_W
OFF_G = OFF_CU + C_WIDTH

LANES = 128
SUBLANES = 8
TM = 256
TX = 512
FC = 512
A_KBLK = 8
LOG2E = 1.4426950408889634
POOL_HALO = 8
MASK_VALUE = -1e30
VMEM_LIMIT = 56 * 1024 * 1024


def _cparams(sem):
    return pltpu.CompilerParams(dimension_semantics=sem, vmem_limit_bytes=VMEM_LIMIT)


def _dot(a, b):
    return jnp.dot(a, b, preferred_element_type=F32)


def _dot_nt(a, b):
    return lax.dot_general(a, b, (((1,), (1,)), ((), ())), preferred_element_type=F32)


def _split(a):
    hi = a.astype(BF16)
    lo = (a - hi.astype(F32)).astype(BF16)
    return hi, lo


def _dot3(a, b_hi, b_lo):
    a_hi, a_lo = _split(a)
    return _dot(a_hi, b_hi) + (_dot(a_lo, b_hi) + _dot(a_hi, b_lo))


def _const_spec(shape):
    n = len(shape)
    return pl.BlockSpec(shape, lambda *_: (0,) * n, pipeline_mode=pl.Buffered(1))


def _layer_spec(shape, layer):
    n = len(shape)
    return pl.BlockSpec((1,) + tuple(shape), lambda *_: (layer,) + (0,) * n, pipeline_mode=pl.Buffered(1))


def _rms_mod(x, g, shift, scale):
    y = x * lax.rsqrt(jnp.mean(x * x, axis=-1, keepdims=True) + EPS) * g
    return y * (1.0 + scale) + shift


def _mod_kernel(c_ref, w_ref, b_ref, o_ref):
    c = c_ref[...]
    s = c * jax.nn.sigmoid(c)
    w_hi, w_lo = _split(w_ref[0])
    o_ref[0] = _dot3(s, w_hi, w_lo) + b_ref[0]


def _mod_vectors(cvec, w_mod, b_mod):
    depth, d, n = w_mod.shape
    tn = 1536
    return pl.pallas_call(
        _mod_kernel,
        grid=(depth, n // tn),
        in_specs=[pl.BlockSpec((SUBLANES, d), lambda l, j: (0, 0)),
                  pl.BlockSpec((1, d, tn), lambda l, j: (l, 0, j)),
                  pl.BlockSpec((1, 1, tn), lambda l, j: (l, 0, j))],
        out_specs=pl.BlockSpec((1, SUBLANES, tn), lambda l, j: (l, 0, j)),
        out_shape=jax.ShapeDtypeStruct((depth, SUBLANES, n), F32),
        compiler_params=_cparams(("arbitrary", "arbitrary")),
        name="mod_vectors",
    )(cvec, w_mod, b_mod.reshape(depth, 1, n))


def _rotate_half(x):
    half = AXIS_ROPE_DIM // 2
    lane = lax.broadcasted_iota(jnp.int32, (1, LANES), 1)
    first = (lane % AXIS_ROPE_DIM) < half
    return jnp.where(first, pltpu.roll(x, LANES - half, 1), pltpu.roll(x, half, 1))


def _inproj_kernel(x_ref, mod_ref, g_ref, w_ref, bd_ref, gqk_ref, cos_ref, sin_ref,
                   q_ref, kt_ref, v_ref, bq_ref, bk_ref, bv_ref, cu_ref):
    m = mod_ref[0]
    h = _rms_mod(x_ref[0], g_ref[...], m[0:1], m[1:2])
    px = _dot(h.astype(BF16), w_ref[0])

    qk = px[:, :QK_W]
    sq_hi, sq_lo = _split(qk * qk)
    ss = _dot(sq_hi, bd_ref[...]) + _dot(sq_lo, bd_ref[...])
    qn = qk * lax.rsqrt(ss * (1.0 / HEAD_DIM) + EPS) * gqk_ref[...]
    cos = cos_ref[...]
    sin = sin_ref[...]
    chunks = []
    for j in range(QK_W // LANES):
        c = qn[:, j * LANES:(j + 1) * LANES]
        chunks.append(c * cos + _rotate_half(c) * sin)
    scale = HEAD_DIM ** -0.5
    for j in range(A_Q // LANES):
        q_ref[0, :, j * LANES:(j + 1) * LANES] = (chunks[j] * (scale * LOG2E)).astype(BF16)
    kt_ref[0, 0] = chunks[A_Q // LANES].T.astype(BF16)

    ones = jnp.ones((TM, LANES - HEAD_DIM), BF16)
    for j in range(A_KV_HEADS):
        vj = px[:, OFF_AV + j * HEAD_DIM:OFF_AV + (j + 1) * HEAD_DIM].astype(BF16)
        v_ref[0, :, j * LANES:(j + 1) * LANES] = jnp.concatenate([vj, ones], axis=1)
    bq_ref[0] = (px[:, OFF_BQ:OFF_BK] * (scale * LOG2E)).astype(BF16)
    bk_ref[0] = px[:, OFF_BK:OFF_BV].astype(BF16)
    for h in range(B_HEADS):
        vh = px[:, OFF_BV + h * HEAD_DIM:OFF_BV + (h + 1) * HEAD_DIM].astype(BF16)
        bv_ref[0, :, h * LANES:(h + 1) * LANES] = jnp.concatenate([vh, ones], axis=1)
    cu_ref[0] = px[:, OFF_CU:OFF_G]


def _inproj(xs, modr, g1, w_qkv, layer, bd, gqk, cos_t, sin_t, nct):
    B, T, D = xs.shape
    nt = T // TM
    tok = lambda w: pl.BlockSpec((1, TM, w), lambda b, i: (b, i, 0))
    out_shape = (
        jax.ShapeDtypeStruct((B, T, A_Q), BF16),
        jax.ShapeDtypeStruct((B, nt, A_KV, TM), BF16),
        jax.ShapeDtypeStruct((B, T, A_KV_HEADS * LANES), BF16),
        jax.ShapeDtypeStruct((B, T, B_W), BF16),
        jax.ShapeDtypeStruct((B, T, B_W), BF16),
        jax.ShapeDtypeStruct((B, T, B_HEADS * LANES), BF16),
        jax.ShapeDtypeStruct((B, T, C_WIDTH), F32),
    )
    return pl.pallas_call(
        _inproj_kernel,
        grid=(B, nt),
        in_specs=[tok(D),
                  pl.BlockSpec((1, 6, D), lambda b, i: (jnp.where(i < nct, B, b), 0, 0)),
                  _const_spec((1, D)),
                  _layer_spec((D, OFF_G), layer),
                  _const_spec((QK_W, QK_W)),
                  _const_spec((1, QK_W)),
                  pl.BlockSpec((TM, LANES), lambda b, i: (i, 0)),
                  pl.BlockSpec((TM, LANES), lambda b, i: (i, 0))],
        out_specs=(tok(A_Q),
                   pl.BlockSpec((1, 1, A_KV, TM), lambda b, i: (b, i, 0, 0)),
                   tok(A_KV_HEADS * LANES), tok(B_W), tok(B_W), tok(B_HEADS * LANES), tok(C_WIDTH)),
        out_shape=out_shape,
        compiler_params=_cparams(("arbitrary", "arbitrary")),
        name="inproj",
    )(xs, modr, g1, w_qkv, bd, gqk, cos_t, sin_t)


def _attn_a_kernel(q_ref, kt_ref, v_ref, o_ref, *, nct, n_steps):
    i = pl.program_id(1)
    rows = A_GROUP * TM

    def step(carry, q4, j, blk0, nblk):
        m, acc = carry
        s_list = [_dot(q4, kt_ref[0, blk0 + c, j * HEAD_DIM:(j + 1) * HEAD_DIM, :]) for c in range(nblk)]
        smax = s_list[0]
        for s in s_list[1:]:
            smax = jnp.maximum(smax, s)
        m_new = jnp.maximum(m, jnp.max(smax, axis=-1, keepdims=True))
        alpha = jnp.exp2(m - m_new)
        p = jnp.concatenate([jnp.exp2((s - m_new).astype(BF16)) for s in s_list], axis=1)
        vv = v_ref[0, pl.ds(pl.multiple_of(blk0 * TM, TM), nblk * TM), j * LANES:(j + 1) * LANES]
        return m_new, alpha * acc + _dot(p, vv)

    def run(n_main):
        for j in range(A_KV_HEADS):
            base = j * A_GROUP * HEAD_DIM
            q4 = jnp.concatenate(
                [q_ref[0, :, base + g * HEAD_DIM: base + (g + 1) * HEAD_DIM] for g in range(A_GROUP)], axis=0)
            carry = (jnp.full((rows, 1), MASK_VALUE, F32), jnp.zeros((rows, LANES), F32))
            if n_main:
                carry = step(carry, q4, j, 0, nct + A_KBLK)
                carry = lax.fori_loop(
                    1, n_main, lambda k, cr: step(cr, q4, j, nct + k * A_KBLK, A_KBLK), carry)
            else:
                carry = step(carry, q4, j, 0, nct)
            acc = carry[1]
            o = acc[:, 0:HEAD_DIM] / acc[:, HEAD_DIM:HEAD_DIM + 1]
            for g in range(A_GROUP):
                o_ref[0, :, base + g * HEAD_DIM: base + (g + 1) * HEAD_DIM] = o[g * TM:(g + 1) * TM].astype(BF16)

    @pl.when(i < nct)
    def _():
        run(0)

    @pl.when(i >= nct)
    def _():
        run(n_steps)


def _attn_a(q, kt, v, nct):
    B, T, _ = q.shape
    nt = T // TM
    assert (nt - nct) % A_KBLK == 0
    return pl.pallas_call(
        functools.partial(_attn_a_kernel, nct=nct, n_steps=(nt - nct) // A_KBLK),
        grid=(B, nt),
        in_specs=[pl.BlockSpec((1, TM, A_Q), lambda b, i: (b, i, 0)),
                  pl.BlockSpec((1, nt, A_KV, TM), lambda b, i: (b, 0, 0, 0)),
                  pl.BlockSpec((1, T, A_KV_HEADS * LANES), lambda b, i: (b, 0, 0))],
        out_specs=pl.BlockSpec((1, TM, A_Q), lambda b, i: (b, i, 0)),
        out_shape=jax.ShapeDtypeStruct((B, T, A_Q), BF16),
        compiler_params=_cparams(("arbitrary", "arbitrary")),
        name="attn_a",
    )(q, kt, v)


def _attn_b_kernel(q_ref, kp_ref, kc_ref, kn_ref, vp_ref, vc_ref, vn_ref, kx_ref, vx_ref, bias_ref,
                   o_ref, kbuf, vbuf, *, nct, L):
    i = pl.program_id(1)

    def normalised(o_ext):
        return (o_ext[:, 0:HEAD_DIM] / o_ext[:, HEAD_DIM:HEAD_DIM + 1]).astype(BF16)

    @pl.when(i < nct)
    def _():
        for h in range(B_HEADS):
            sl = slice(h * HEAD_DIM, (h + 1) * HEAD_DIM)
            vl = slice(h * LANES, (h + 1) * LANES)
            s = _dot_nt(q_ref[0, :, sl], kx_ref[0, :, sl])
            m = jnp.max(s, axis=-1, keepdims=True)
            p = jnp.exp2((s - m).astype(BF16))
            o_ref[0, :, sl] = normalised(_dot(p, vx_ref[0, :, vl]))

    @pl.when(i >= nct)
    def _():
        for buf, cx, pv, cu, nx in ((kbuf, kx_ref, kp_ref, kc_ref, kn_ref), (vbuf, vx_ref, vp_ref, vc_ref, vn_ref)):
            buf[0:L] = cx[0]
            buf[L:L + TM] = pv[0]
            buf[L + TM:L + 2 * TM] = cu[0]
            buf[L + 2 * TM:L + 3 * TM] = nx[0]
        for h in range(B_HEADS):
            sl = slice(h * HEAD_DIM, (h + 1) * HEAD_DIM)
            s = _dot_nt(q_ref[0, :, sl], kbuf[:, sl])
            s_c = s[:, 0:L]
            s_w = s[:, L:] + bias_ref[0, 0, h]
            m = jnp.maximum(jnp.max(s_w, axis=-1, keepdims=True), jnp.max(s_c, axis=-1, keepdims=True))
            p = jnp.concatenate([jnp.exp2((s_c - m).astype(BF16)), jnp.exp2((s_w - m).astype(BF16))], axis=1)
            o_ref[0, :, sl] = normalised(_dot(p, vbuf[:, h * LANES:(h + 1) * LANES]))


def _attn_b(bq, bk, bv, bias_t, layer, nct, L):
    B, T, _ = bq.shape
    nt = T // TM
    assert L % LANES == 0 and nt - nct >= 2
    VW = B_HEADS * LANES
    case = lambda i: jnp.where(i <= nct, 0, jnp.where(i == nt - 1, 2, 1))
    cur = lambda b, i: (b, i, 0)
    prev = lambda b, i: (b, jnp.maximum(i - 1, nct), 0)
    nxt = lambda b, i: (b, jnp.minimum(i + 1, nt - 1), 0)
    blk = lambda f: pl.BlockSpec((1, TM, B_W), f)
    vblk = lambda f: pl.BlockSpec((1, TM, VW), f)
    ctx = lambda w: pl.BlockSpec((1, L, w), lambda b, i: (b, 0, 0))
    return pl.pallas_call(
        functools.partial(_attn_b_kernel, nct=nct, L=L),
        grid=(B, nt),
        in_specs=[blk(cur), blk(prev), blk(cur), blk(nxt), vblk(prev), vblk(cur), vblk(nxt), ctx(B_W), ctx(VW),
                  pl.BlockSpec((1, 1) + bias_t.shape[2:], lambda b, i: (layer, case(i), 0, 0, 0))],
        out_specs=blk(cur),
        out_shape=jax.ShapeDtypeStruct((B, T, B_W), BF16),
        scratch_shapes=[pltpu.VMEM((L + 3 * TM, B_W), BF16), pltpu.VMEM((L + 3 * TM, VW), BF16)],
        compiler_params=_cparams(("arbitrary", "arbitrary")),
        name="attn_b",
    )(bq, bk, bk, bk, bv, bv, bv, bk, bv, bias_t)


def _merge_kernel(*refs, nct, nt, L, S, with_router):
    if with_router:
        (ya_ref, yb_ref, cup_ref, cu_ref, cun_ref, x_ref, mod_ref, g1_ref, wg_ref, wpa_ref, wpb_ref, wpc_ref,
         pbd_ref, psc_ref, wo_ref, g2_ref, rw_ref, rb_ref,
         xo_ref, h_ref, cmb_ref, sel_ref, e_scr) = refs
    else:
        (ya_ref, yb_ref, cup_ref, cu_ref, cun_ref, x_ref, mod_ref, g1_ref, wg_ref, wpa_ref, wpb_ref, wpc_ref,
         pbd_ref, psc_ref, wo_ref, g2_ref, xo_ref, h_ref, e_scr) = refs
    i = pl.program_id(1)
    D = D_MODEL
    m = mod_ref[0]
    x_in = x_ref[0]

    at_start = (i == 0) | (i == nct)
    at_end = (i == nct - 1) | (i == nt - 1)
    u = cu_ref[0]
    e_scr[0:POOL_HALO] = jnp.where(at_start, 0.0, cup_ref[0])
    e_scr[POOL_HALO:POOL_HALO + TM] = u
    e_scr[POOL_HALO + TM:] = jnp.where(at_end, 0.0, cun_ref[0])
    sh = lambda k: e_scr[POOL_HALO + k:POOL_HALO + k + TM]
    grp = lax.broadcasted_iota(jnp.int32, (1, C_WIDTH), 1) // C_GROUP_DIM
    t_loc = lax.broadcasted_iota(jnp.int32, (TM, 1), 0)
    t_seq = jnp.where(i < nct, i * TM, (i - nct) * TM) + t_loc
    n_seq = jnp.where(i < nct, L, S)
    wsum, cnt, running, lo, hi = None, None, None, 0, 0
    for g, w in enumerate(POOL_WINDOWS):
        for k in list(range(-(w // 2), lo)) + list(range(hi, w - w // 2)):
            running = sh(k) if running is None else running + sh(k)
        lo, hi = -(w // 2), w - w // 2
        cnt_g = jnp.minimum(t_seq + hi, n_seq) - jnp.maximum(t_seq + lo, 0)
        wsum = running if wsum is None else jnp.where(grp == g, running, wsum)
        cnt = cnt_g if cnt is None else jnp.where(grp == g, cnt_g, cnt)
    dlt = wsum / cnt.astype(F32) - u
    yc = _dot(dlt.astype(BF16), pbd_ref[0]) * psc_ref[...]

    h1 = _rms_mod(x_in, g1_ref[...], m[0:1], m[1:2]).astype(BF16)
    branches = (ya_ref[0], yb_ref[0], yc.astype(BF16))
    weights = (wpa_ref, wpb_ref, wpc_ref)
    mrg = None
    for k in range(3):
        gate = jax.nn.sigmoid(_dot(h1, wg_ref[0, :, k * D:(k + 1) * D]))
        term = gate * _dot(branches[k], weights[k][0])
        mrg = term if mrg is None else mrg + term
    x = x_in + m[2:3] * _dot(mrg.astype(BF16), wo_ref[0])
    xo_ref[0] = x
    h = _rms_mod(x, g2_ref[...], m[3:4], m[4:5])
    h_ref[0] = h.astype(BF16)

    if with_router:
        lane = lax.broadcasted_iota(jnp.int32, (1, LANES), 1).astype(F32)
        h_hi, h_lo = _split(h)
        d_hi = _dot(h_hi, rw_ref[...])
        d_lo = _dot(h_lo, rw_ref[...])
        lg = d_hi[:, :LANES] + (d_hi[:, LANES:] + d_lo[:, :LANES]) + rb_ref[...]
        lg = jnp.where(lane < N_EXPERTS, lg, -jnp.inf)
        m1 = jnp.max(lg, axis=-1, keepdims=True)
        i1 = jnp.min(jnp.where(lg == m1, lane, float(LANES)), axis=-1, keepdims=True)
        mask1 = lane == i1
        lg2 = jnp.where(mask1, -jnp.inf, lg)
        m2 = jnp.max(lg2, axis=-1, keepdims=True)
        i2 = jnp.min(jnp.where(lg2 == m2, lane, float(LANES)), axis=-1, keepdims=True)
        mask2 = lane == i2
        e2 = jnp.exp(m2 - m1)
        den = 1.0 + e2
        cmb_ref[0] = jnp.where(mask1, 1.0 / den, 0.0) + jnp.where(mask2, e2 / den, 0.0)
        sel_ref[0] = jnp.where(mask1 | mask2, 1.0, 0.0)


def _merge(ya, yb, cu, xs, modr, g1, w_gate, wpa, wpb, wpc, pbd, psc, wo, g2, layer, router, nct, L):
    B, T, D = xs.shape
    nt = T // TM
    S = T - L
    hb = TM // POOL_HALO
    tok = lambda w: pl.BlockSpec((1, TM, w), lambda b, i: (b, i, 0))
    in_specs = [tok(A_Q), tok(B_W),
                pl.BlockSpec((1, POOL_HALO, C_WIDTH), lambda b, i: (b, jnp.maximum(i * hb - 1, 0), 0)),
                tok(C_WIDTH),
                pl.BlockSpec((1, POOL_HALO, C_WIDTH), lambda b, i: (b, jnp.minimum((i + 1) * hb, nt * hb - 1), 0)),
                tok(D),
                pl.BlockSpec((1, 6, D), lambda b, i: (jnp.where(i < nct, B, b), 0, 0)),
                _const_spec((1, D)),
                _layer_spec((D, 3 * D), layer),
                _layer_spec((A_Q, D), layer), _layer_spec((B_W, D), layer), _layer_spec((C_WIDTH, D), layer),
                _layer_spec((C_WIDTH, C_WIDTH), layer), _const_spec((1, C_WIDTH)), _layer_spec((D, D), layer),
                _const_spec((1, D))]
    args = [ya, yb, cu, cu, cu, xs, modr, g1, w_gate, wpa, wpb, wpc, pbd, psc, wo, g2]
    out_specs = [tok(D), tok(D)]
    out_shape = [jax.ShapeDtypeStruct((B, T, D), F32), jax.ShapeDtypeStruct((B, T, D), BF16)]
    if router is not None:
        in_specs += [_const_spec((D, 2 * LANES)), _const_spec((1, LANES))]
        args += list(router)
        out_specs += [tok(LANES), tok(LANES)]
        out_shape += [jax.ShapeDtypeStruct((B, T, LANES), F32)] * 2
    return pl.pallas_call(
        functools.partial(_merge_kernel, nct=nct, nt=nt, L=L, S=S, with_router=router is not None),
        grid=(B, nt),
        in_specs=in_specs,
        out_specs=tuple(out_specs),
        out_shape=tuple(out_shape),
        scratch_shapes=[pltpu.VMEM((TM + 2 * POOL_HALO, C_WIDTH), F32)],
        compiler_params=_cparams(("arbitrary", "arbitrary")),
        name="merge_router" if router is not None else "merge",
    )(*args)


def _swiglu_acc(h, w1_ref, w3_ref, w2_ref, lead):
    d_ff = w1_ref.shape[-1]
    acc = None
    for f0 in range(0, d_ff, FC):
        f1 = min(f0 + FC, d_ff)
        a = _dot(h, w1_ref[lead + (slice(None), slice(f0, f1))])
        b = _dot(h, w3_ref[lead + (slice(None), slice(f0, f1))])
        t = (a * jax.nn.sigmoid(a) * b).astype(BF16)
        part = _dot(t, w2_ref[lead + (slice(f0, f1), slice(None))])
        acc = part if acc is None else acc + part
    return acc


def _ffn_kernel(h_ref, x_ref, g_ref, w1_ref, w3_ref, w2_ref, o_ref):
    f = _swiglu_acc(h_ref[0], w1_ref, w3_ref, w2_ref, (0,))
    o_ref[0] = x_ref[0] + g_ref[0] * f


def _ffn_dense(h2, xs, gate2, w1, w3, w2, idx, nct):
    B, T, D = xs.shape
    F = w1.shape[-1]
    nt = T // TM
    tok = lambda: pl.BlockSpec((1, TM, D), lambda b, i: (b, i, 0))
    return pl.pallas_call(
        _ffn_kernel,
        grid=(B, nt),
        in_specs=[tok(), tok(),
                  pl.BlockSpec((1, 1, D), lambda b, i: (jnp.where(i < nct, B, b), 0, 0)),
                  _layer_spec((D, F), idx), _layer_spec((D, F), idx), _layer_spec((F, D), idx)],
        out_specs=tok(),
        out_shape=jax.ShapeDtypeStruct((B, T, D), F32),
        compiler_params=_cparams(("arbitrary", "arbitrary")),
        name="ffn_dense",
    )(h2, xs, gate2, w1, w3, w2)


def _gather_kernel(wj_ref, ws_ref, wf_ref, wl_ref, wv_ref, te_ref, pos_ref, cmb_ref, h_ref,
                   o_ref, pw_ref, acc_ref, accw_ref):
    w = pl.program_id(0)

    @pl.when(wf_ref[w] == 1)
    def _():
        acc_ref[...] = jnp.zeros_like(acc_ref)
        accw_ref[...] = jnp.zeros_like(accw_ref)

    @pl.when(wv_ref[w] == 1)
    def _():
        row = lax.broadcasted_iota(jnp.int32, (TX, 1), 0) + wj_ref[w] * TX
        hit = pos_ref[0, 0] == row
        acc_ref[...] += _dot(jnp.where(hit, 1.0, 0.0).astype(BF16), h_ref[...])
        accw_ref[...] += jnp.sum(jnp.where(hit, cmb_ref[0, 0], 0.0), axis=-1, keepdims=True)

    @pl.when(wl_ref[w] == 1)
    def _():
        o_ref[...] = acc_ref[...].astype(BF16)
        pw_ref[...] = accw_ref[...]


def _moe_gather(work, tile_expert, pos_t, cmb_t, h2f, n_tiles):
    wj, ws, wf, wl, wv = work
    n, D = h2f.shape
    row_spec = pl.BlockSpec((1, 1, 1, TX), lambda w, wj, ws, wf, wl, wv, te: (te[wj[w]], ws[w], 0, 0))
    return pl.pallas_call(
        _gather_kernel,
        grid_spec=pltpu.PrefetchScalarGridSpec(
            num_scalar_prefetch=6,
            grid=(wj.shape[0],),
            in_specs=[row_spec, row_spec,
                      pl.BlockSpec((TX, D), lambda w, wj, ws, wf, wl, wv, te: (ws[w], 0))],
            out_specs=(pl.BlockSpec((TX, D), lambda w, wj, ws, wf, wl, wv, te: (wj[w], 0)),
                       pl.BlockSpec((TX, 1), lambda w, wj, ws, wf, wl, wv, te: (wj[w], 0))),
            scratch_shapes=[pltpu.VMEM((TX, D), F32), pltpu.VMEM((TX, 1), F32)]),
        out_shape=(jax.ShapeDtypeStruct((n_tiles * TX, D), BF16),
                   jax.ShapeDtypeStruct((n_tiles * TX, 1), F32)),
        compiler_params=_cparams(("arbitrary",)),
        name="moe_gather",
    )(wj, ws, wf, wl, wv, tile_expert, pos_t, cmb_t, h2f)


def _gffn_kernel(te_ref, tv_ref, x_ref, pw_ref, w1_ref, w3_ref, w2_ref, o_ref):
    j = pl.program_id(0)

    @pl.when(tv_ref[j] == 1)
    def _():
        f = _swiglu_acc(x_ref[...], w1_ref, w3_ref, w2_ref, (0, 0))
        o_ref[...] = (f * pw_ref[...]).astype(BF16)

    @pl.when(tv_ref[j] == 0)
    def _():
        o_ref[...] = jnp.zeros_like(o_ref)


def _moe_gffn(tile_expert, tile_valid, xg, pw, w1, w3, w2, idx):
    P, D = xg.shape
    F = w1.shape[-1]
    wspec = lambda shape: pl.BlockSpec((1, 1) + shape, lambda j, te, tv: (idx, te[j], 0, 0),
                                       pipeline_mode=pl.Buffered(1))
    return pl.pallas_call(
        _gffn_kernel,
        grid_spec=pltpu.PrefetchScalarGridSpec(
            num_scalar_prefetch=2,
            grid=(P // TX,),
            in_specs=[pl.BlockSpec((TX, D), lambda j, te, tv: (j, 0)),
                      pl.BlockSpec((TX, 1), lambda j, te, tv: (j, 0)),
                      wspec((D, F)), wspec((D, F)), wspec((F, D))],
            out_specs=pl.BlockSpec((TX, D), lambda j, te, tv: (j, 0))),
        out_shape=jax.ShapeDtypeStruct((P, D), BF16),
        compiler_params=_cparams(("arbitrary",)),
        name="moe_gffn",
    )(tile_expert, tile_valid, xg, pw, w1, w3, w2)


def _combine_kernel(wt_ref, ws_ref, wf_ref, wl_ref, wv_ref, pos_ref, y_ref, x_ref, g_ref, o_ref, acc_ref,
                    *, n_batch, T, L):
    w = pl.program_id(0)

    @pl.when(wf_ref[w] == 1)
    def _():
        acc_ref[...] = jnp.zeros_like(acc_ref)

    @pl.when(wv_ref[w] == 1)
    def _():
        col = lax.broadcasted_iota(jnp.int32, (1, TX), 1) + ws_ref[w] * TX
        pos = pos_ref[...]
        hit = (pos[:, 0:1] == col) | (pos[:, 1:2] == col)
        acc_ref[...] += _dot(jnp.where(hit, 1.0, 0.0).astype(BF16), y_ref[...])

    @pl.when(wl_ref[w] == 1)
    def _():
        n = lax.broadcasted_iota(jnp.int32, (TX, 1), 0) + wt_ref[w] * TX
        b = jnp.zeros((TX, 1), jnp.int32)
        for k in range(1, n_batch):
            b = b + (n >= k * T).astype(jnp.int32)
        rowid = jnp.where(n - b * T < L, n_batch, b)
        gate = jnp.zeros((TX, D_MODEL), F32)
        for r in range(n_batch + 1):
            gate = jnp.where(rowid == r, g_ref[r:r + 1, :], gate)
        o_ref[...] = x_ref[...] + gate * acc_ref[...]


def _moe_combine(work, pos2, yw, xf, gate2, n_batch, T, L):
    wt, ws, wf, wl, wv = work
    n, D = xf.shape
    return pl.pallas_call(
        functools.partial(_combine_kernel, n_batch=n_batch, T=T, L=L),
        grid_spec=pltpu.PrefetchScalarGridSpec(
            num_scalar_prefetch=5,
            grid=(wt.shape[0],),
            in_specs=[pl.BlockSpec((TX, 2), lambda w, wt, ws, wf, wl, wv: (wt[w], 0)),
                      pl.BlockSpec((TX, D), lambda w, wt, ws, wf, wl, wv: (ws[w], 0)),
                      pl.BlockSpec((TX, D), lambda w, wt, ws, wf, wl, wv: (wt[w], 0)),
                      pl.BlockSpec((SUBLANES, D), lambda w, wt, ws, wf, wl, wv: (0, 0))],
            out_specs=pl.BlockSpec((TX, D), lambda w, wt, ws, wf, wl, wv: (wt[w], 0)),
            scratch_shapes=[pltpu.VMEM((TX, D), F32)]),
        out_shape=jax.ShapeDtypeStruct((n, D), F32),
        compiler_params=_cparams(("arbitrary",)),
        name="moe_combine",
    )(wt, ws, wf, wl, wv, pos2, yw, xf, gate2)


def _count_below(sorted_vals, x):
    return jnp.sum((sorted_vals[None, :] < x[:, None]).astype(jnp.int32), axis=1)


def _with_flags(g, it, compute, n_work, total):
    live = jnp.arange(n_work, dtype=jnp.int32) < total
    g_prev = jnp.concatenate([jnp.full((1,), -1, jnp.int32), g[:-1]])
    g_next = jnp.concatenate([g[1:], jnp.full((1,), -1, jnp.int32)])
    live_next = jnp.concatenate([live[1:], jnp.zeros((1,), bool)])
    first = live & (g != g_prev)
    last = live & ((g != g_next) | ~live_next)
    i32 = lambda a: a.astype(jnp.int32)
    return i32(g), i32(it), i32(first), i32(last), i32(live & compute)


def _moe_plan(sel, n_tok):
    E = N_EXPERTS
    sel = sel.astype(jnp.int32)
    cnt = jnp.sum(sel, axis=0)
    rank = jnp.cumsum(sel, axis=0) - 1
    gsz = ((cnt + TX - 1) // TX) * TX
    gend = jnp.cumsum(gsz)
    goff = gend - gsz
    n_tiles = (2 * n_tok + E * (TX - 1) + TX - 1) // TX
    P = n_tiles * TX
    pos = jnp.where(sel == 1, goff[None, :] + rank, -1)
    tile_start = jnp.arange(n_tiles, dtype=jnp.int32) * TX
    tile_valid = tile_start < gend[-1]
    tile_expert = jnp.minimum(_count_below(gend, tile_start + 1), E - 1)
    pmax_tok = jnp.max(pos, axis=1)
    psec_tok = jnp.max(jnp.where(pos == pmax_tok[:, None], -1, pos), axis=1)
    pos2 = jnp.stack([psec_tok, pmax_tok], axis=1)

    n_src_tiles = n_tok // TX
    cin = rank[TX - 1::TX] + 1
    ra = tile_start - goff[tile_expert]
    rb = ra + jnp.clip(cnt[tile_expert] - ra, 0, TX) - 1
    cin_t = cin.T[tile_expert]
    lo = jnp.sum((cin_t <= ra[:, None]).astype(jnp.int32), axis=1)
    hi = jnp.sum((cin_t <= rb[:, None]).astype(jnp.int32), axis=1)
    span = jnp.where(tile_valid, hi - lo + 1, 1)
    n_gw = E * n_src_tiles + n_tiles
    cs = jnp.cumsum(span)
    total = cs[-1]
    w = jnp.minimum(jnp.arange(n_gw, dtype=jnp.int32), total - 1)
    jw = _count_below(cs, w + 1)
    sw = jnp.where(tile_valid[jw], lo[jw] + (w - (cs[jw] - span[jw])), 0)
    gwork = _with_flags(jw, sw, tile_valid[jw], n_gw, total)

    pt = pos.reshape(n_src_tiles, TX, E)
    pmax = jnp.max(pt, axis=1)
    pmin = jnp.min(jnp.where(pt >= 0, pt, P), axis=1)
    ta = pmin // TX
    tb = pmax // TX
    cand_item = jnp.stack([ta, tb], axis=-1).reshape(-1)
    cand_valid = jnp.stack([pmax >= 0, (pmax >= 0) & (tb != ta)], axis=-1).reshape(-1).astype(jnp.int32)
    cand_group = jnp.repeat(jnp.arange(n_src_tiles, dtype=jnp.int32), 2 * E)
    n_cw = E * n_src_tiles + n_tiles
    ccs = jnp.cumsum(cand_valid)
    ctotal = ccs[-1]
    cw = jnp.minimum(jnp.arange(n_cw, dtype=jnp.int32), ctotal - 1)
    cidx = _count_below(ccs, cw + 1)
    cwork = _with_flags(cand_group[cidx], cand_item[cidx], jnp.ones((n_cw,), bool), n_cw, ctotal)
    return pos, pos2, tile_expert, tile_valid.astype(jnp.int32), gwork, cwork, n_tiles


def _moe(h2, xs1, cmb, sel, gate2, w1, w3, w2, idx, L):
    B, T, D = xs1.shape
    n_tok = B * T
    E = N_EXPERTS
    assert n_tok % TX == 0
    selm = sel.reshape(n_tok, LANES)[:, :E] > 0.5
    cmbm = cmb.reshape(n_tok, LANES)[:, :E]
    pos, pos2, tile_expert, tile_valid, gwork, cwork, n_tiles = _moe_plan(selm, n_tok)
    pos_t = pos.T.reshape(E, n_tok // TX, 1, TX)
    cmb_t = cmbm.T.reshape(E, n_tok // TX, 1, TX)

    xg, pw = _moe_gather(gwork, tile_expert, pos_t, cmb_t, h2.reshape(n_tok, D), n_tiles)
    yw = _moe_gffn(tile_expert, tile_valid, xg, pw, w1, w3, w2, idx)
    out = _moe_combine(cwork, pos2, yw, xs1.reshape(n_tok, D), gate2, B, T, L)
    return out.reshape(B, T, D)


def _final_kernel(x_ref, g_ref, o_ref):
    x = x_ref[0]
    o_ref[0] = x * lax.rsqrt(jnp.mean(x * x, axis=-1, keepdims=True) + EPS) * g_ref[...]


def _final_norm(xs, g, nct, S):
    B, T, D = xs.shape
    return pl.pallas_call(
        _final_kernel,
        grid=(B, S // TM),
        in_specs=[pl.BlockSpec((1, TM, D), lambda b, i: (b, i + nct, 0)), _const_spec((1, D))],
        out_specs=pl.BlockSpec((1, TM, D), lambda b, i: (b, i, 0)),
        out_shape=jax.ShapeDtypeStruct((B, S, D), F32),
        compiler_params=_cparams(("arbitrary", "arbitrary")),
        name="final_norm",
    )(xs, g)


def _rope_tables(L, S):
    t = np.arange(S)
    pos = np.stack([t // GRID_W, t % GRID_W], axis=-1).astype(np.float32)
    inv_freq = (ROPE_THETA ** (-np.arange(0, AXIS_ROPE_DIM, 2, dtype=np.float32) / AXIS_ROPE_DIM)).astype(np.float32)
    ang = pos[:, :, None] * inv_freq[None, None, :]
    cos, sin = np.cos(ang), np.sin(ang)
    cos64 = np.concatenate([cos[:, 0], cos[:, 0], cos[:, 1], cos[:, 1]], axis=-1)
    sin64 = np.concatenate([-sin[:, 0], sin[:, 0], -sin[:, 1], sin[:, 1]], axis=-1)
    cos_t = np.concatenate([np.ones((L, HEAD_DIM), np.float32), cos64], axis=0)
    sin_t = np.concatenate([np.zeros((L, HEAD_DIM), np.float32), sin64], axis=0)
    rep = LANES // HEAD_DIM
    return jnp.asarray(np.tile(cos_t, (1, rep)), F32), jnp.asarray(np.tile(sin_t, (1, rep)), F32)


def _na_bias_table(rel_bias):
    rpt = TM // GRID_W
    half = NA_WIN_ROWS // 2
    assert half <= rpt and rpt - 1 - half + NA_WIN_ROWS <= 2 * rpt and rpt + half + 1 >= NA_WIN_ROWS
    col = np.arange(GRID_W)
    col_start = np.clip(col - NA_WIN_COLS // 2, 0, GRID_W - NA_WIN_COLS)
    kc = np.arange(GRID_W)
    inside = (kc[None, :] >= col_start[:, None]) & (kc[None, :] < col_start[:, None] + NA_WIN_COLS)
    relp = jnp.pad(rel_bias, ((0, 0), (0, 0), (0, 0), (GRID_W, GRID_W)))
    off = GRID_W + NA_WIN_COLS - 1
    cmat = jnp.stack([relp[..., off - qc:off - qc + GRID_W] for qc in range(GRID_W)], axis=3)
    cmat = jnp.where(jnp.asarray(inside), cmat * LOG2E, MASK_VALUE)
    masked = jnp.full(cmat.shape[:2] + (GRID_W, GRID_W), MASK_VALUE, F32)

    cases = []
    for case in range(3):
        rows = []
        for a in range(rpt):
            first = (rpt + max(a - half, 0),
                     rpt + a - half,
                     min(rpt + a - half, 2 * rpt - NA_WIN_ROWS))[case]
            blocks = [cmat[:, :, t - (rpt + a) + NA_WIN_ROWS - 1] if first <= t < first + NA_WIN_ROWS else masked
                      for t in range(3 * rpt)]
            rows.append(jnp.concatenate(blocks, axis=-1))
        cases.append(jnp.concatenate(rows, axis=-2))
    return jnp.stack(cases, axis=1)


def _block_diag_ones(n, blk):
    idx = np.arange(n) // blk
    return jnp.asarray((idx[:, None] == idx[None, :]).astype(np.float32), BF16)


def _pool_block_diag(pool_w):
    depth, g, c, d = pool_w.shape
    eye = jnp.asarray(np.eye(g, dtype=np.float32))
    out = pool_w[:, :, :, None, :] * eye[None, :, None, :, None]
    return out.reshape(depth, g * c, g * d).astype(BF16)


def kernel(x, c, ctx, c_ctx, w_mod, b_mod, norm1_g, norm2_g, w_in, q_norm_g, k_norm_g, na_rel_bias, pool_w,
           pool_scale, w_branch_a, w_branch_b, w_branch_c, w_out, ffn_w1, ffn_w3, ffn_w2, router_w, router_b,
           moe_w1, moe_w3, moe_w2, final_g):
    B, S, D = x.shape
    L = ctx.shape[1]
    T = L + S
    depth = w_mod.shape[0]
    assert D == D_MODEL and L % TM == 0 and S % TM == 0 and TM % GRID_W == 0 and B + 1 <= SUBLANES
    assert S // GRID_W >= NA_WIN_ROWS and len(POOL_WINDOWS) == C_GROUPS
    assert list(POOL_WINDOWS) == sorted(POOL_WINDOWS) and POOL_WINDOWS[-1] - POOL_WINDOWS[-1] // 2 <= POOL_HALO
    nct = L // TM

    cvec = jnp.zeros((SUBLANES, D), F32).at[:B].set(c).at[B].set(c_ctx)
    mod = _mod_vectors(cvec, w_mod, b_mod)
    cos_t, sin_t = _rope_tables(L, S)
    bd = _block_diag_ones(QK_W, HEAD_DIM)
    na_bias = _na_bias_table(na_rel_bias)
    bf = lambda a: a.astype(BF16)
    w_qkv, w_gate = bf(w_in[:, :, :OFF_G]), bf(w_in[:, :, OFF_G:])
    wpa, wpb, wpc, wo, pbd = bf(w_branch_a), bf(w_branch_b), bf(w_branch_c), bf(w_out), _pool_block_diag(pool_w)
    fw1, fw3, fw2 = bf(ffn_w1), bf(ffn_w3), bf(ffn_w2)
    mw1, mw3, mw2 = bf(moe_w1), bf(moe_w3), bf(moe_w2)

    xs = jnp.concatenate([ctx, x], axis=1)
    for layer in range(depth):
        modr = mod[layer].reshape(SUBLANES, 6, D)
        g1 = norm1_g[layer].reshape(1, D)
        gqk = jnp.concatenate([jnp.tile(q_norm_g[layer], A_HEADS), jnp.tile(k_norm_g[layer], A_KV_HEADS)])
        q, kt, v, bq, bk, bv, cu = _inproj(xs, modr, g1, w_qkv, layer, bd, gqk.reshape(1, QK_W), cos_t, sin_t, nct)
        ya = _attn_a(q, kt, v, nct)
        yb = _attn_b(bq, bk, bv, na_bias, layer, nct, L)
        is_moe = layer % 2 == 1
        i = layer // 2
        router = None
        if is_moe:
            rw = jnp.zeros((D, LANES), F32).at[:, :N_EXPERTS].set(router_w[i])
            rw_hi = rw.astype(BF16)
            rw_lo = (rw - rw_hi.astype(F32)).astype(BF16)
            rb = jnp.zeros((1, LANES), F32).at[0, :N_EXPERTS].set(router_b[i])
            router = (jnp.concatenate([rw_hi, rw_lo], axis=1), rb)
        outs = _merge(ya, yb, cu, xs, modr, g1, w_gate, wpa, wpb, wpc, pbd, pool_scale[layer].reshape(1, C_WIDTH),
                      wo, norm2_g[layer].reshape(1, D), layer, router, nct, L)
        if is_moe:
            xs1, h2, cmb, sel = outs
            xs = _moe(h2, xs1, cmb, sel, modr[:, 5, :], mw1, mw3, mw2, i, L)
        else:
            xs1, h2 = outs
            xs = _ffn_dense(h2, xs1, modr[:, 5:6, :], fw1, fw3, fw2, i, nct)
    return _final_norm(xs, final_g.reshape(1, D), nct, S)
```

```python
import functools

import numpy as np
import jax
import jax.numpy as jnp
from jax import lax
from jax.experimental import pallas as pl
from jax.experimental.pallas import tpu as pltpu

F32 = jnp.float32
BF16 = jnp.bfloat16

D_MODEL = 1024
DEPTH = 4
GRID_W = 64
HEAD_DIM = 64
A_HEADS = 8
A_KV_HEADS = 2
A_GROUP = A_HEADS // A_KV_HEADS
B_HEADS = 4
C_GROUPS = 4
C_GROUP_DIM = 64
C_WIDTH = C_GROUPS * C_GROUP_DIM
POOL_WINDOWS = (2, 4, 8, 16)
NA_WIN_ROWS = 8
NA_WIN_COLS = 16
ROPE_THETA = 10000.0
AXIS_ROPE_DIM = HEAD_DIM // 2
N_EXPERTS = 8
EPS = 1e-6

A_Q = A_HEADS * HEAD_DIM
A_KV = A_KV_HEADS * HEAD_DIM
B_W = B_HEADS * HEAD_DIM
QK_W = A_Q + A_KV
OFF_AV = QK_W
OFF_BQ = OFF_AV + A_KV
OFF_BK = OFF_BQ + B_W
OFF_BV = OFF_BK + B_W
OFF_CU = OFF_BV + B_W
OFF_G = OFF_CU + C_WIDTH

LANES = 128
SUBLANES = 8
TM = 256
TX = 512
FC = 512
A_KBLK = 8
LOG2E = 1.4426950408889634
POOL_HALO = 8
MASK_VALUE = -1e30
VMEM_LIMIT = 56 * 1024 * 1024


def _cparams(sem):
    return pltpu.CompilerParams(dimension_semantics=sem, vmem_limit_bytes=VMEM_LIMIT)


def _dot(a, b):
    return jnp.dot(a, b, preferred_element_type=F32)


def _dot_nt(a, b):
    return lax.dot_general(a, b, (((1,), (1,)), ((), ())), preferred_element_type=F32)


def _split(a):
    hi = a.astype(BF16)
    lo = (a - hi.astype(F32)).astype(BF16)
    return hi, lo


def _dot3(a, b_hi, b_lo):
    a_hi, a_lo = _split(a)
    return _dot(a_hi, b_hi) + (_dot(a_lo, b_hi) + _dot(a_hi, b_lo))


def _const_spec(shape):
    n = len(shape)
    return pl.BlockSpec(shape, lambda *_: (0,) * n, pipeline_mode=pl.Buffered(1))


def _layer_spec(shape, layer):
    n = len(shape)
    return pl.BlockSpec((1,) + tuple(shape), lambda *_: (layer,) + (0,) * n, pipeline_mode=pl.Buffered(1))


def _rms_mod(x, g, shift, scale):
    y = x * lax.rsqrt(jnp.mean(x * x, axis=-1, keepdims=True) + EPS) * g
    return y * (1.0 + scale) + shift


def _mod_kernel(c_ref, w_ref, b_ref, o_ref):
    c = c_ref[...]
    s = c * jax.nn.sigmoid(c)
    w_hi, w_lo = _split(w_ref[0])
    o_ref[0] = _dot3(s, w_hi, w_lo) + b_ref[0]


def _mod_vectors(cvec, w_mod, b_mod):
    depth, d, n = w_mod.shape
    tn = 1536
    return pl.pallas_call(
        _mod_kernel,
        grid=(depth, n // tn),
        in_specs=[pl.BlockSpec((SUBLANES, d), lambda l, j: (0, 0)),
                  pl.BlockSpec((1, d, tn), lambda l, j: (l, 0, j)),
                  pl.BlockSpec((1, 1, tn), lambda l, j: (l, 0, j))],
        out_specs=pl.BlockSpec((1, SUBLANES, tn), lambda l, j: (l, 0, j)),
        out_shape=jax.ShapeDtypeStruct((depth, SUBLANES, n), F32),
        compiler_params=_cparams(("arbitrary", "arbitrary")),
        name="mod_vectors",
    )(cvec, w_mod, b_mod.reshape(depth, 1, n))


def _rotate_half(x):
    half = AXIS_ROPE_DIM // 2
    lane = lax.broadcasted_iota(jnp.int32, (1, LANES), 1)
    first = (lane % AXIS_ROPE_DIM) < half
    return jnp.where(first, pltpu.roll(x, LANES - half, 1), pltpu.roll(x, half, 1))


def _inproj_kernel(x_ref, mod_ref, g_ref, w_ref, bd_ref, gqk_ref, cos_ref, sin_ref,
                   q_ref, kt_ref, v_ref, bq_ref, bk_ref, bv_ref, cu_ref):
    m = mod_ref[0]
    h = _rms_mod(x_ref[0], g_ref[...], m[0:1], m[1:2])
    px = _dot(h.astype(BF16), w_ref[0])

    qk = px[:, :QK_W]
    sq_hi, sq_lo = _split(qk * qk)
    ss = _dot(sq_hi, bd_ref[...]) + _dot(sq_lo, bd_ref[...])
    qn = qk * lax.rsqrt(ss * (1.0 / HEAD_DIM) + EPS) * gqk_ref[...]
    cos = cos_ref[...]
    sin = sin_ref[...]
    chunks = []
    for j in range(QK_W // LANES):
        c = qn[:, j * LANES:(j + 1) * LANES]
        chunks.append(c * cos + _rotate_half(c) * sin)
    scale = HEAD_DIM ** -0.5
    for j in range(A_Q // LANES):
        q_ref[0, :, j * LANES:(j + 1) * LANES] = (chunks[j] * (scale * LOG2E)).astype(BF16)
    kt_ref[0, 0] = chunks[A_Q // LANES].T.astype(BF16)

    ones = jnp.ones((TM, LANES - HEAD_DIM), BF16)
    for j in range(A_KV_HEADS):
        vj = px[:, OFF_AV + j * HEAD_DIM:OFF_AV + (j + 1) * HEAD_DIM].astype(BF16)
        v_ref[0, :, j * LANES:(j + 1) * LANES] = jnp.concatenate([vj, ones], axis=1)
    bq_ref[0] = (px[:, OFF_BQ:OFF_BK] * (scale * LOG2E)).astype(BF16)
    bk_ref[0] = px[:, OFF_BK:OFF_BV].astype(BF16)
    for h in range(B_HEADS):
        vh = px[:, OFF_BV + h * HEAD_DIM:OFF_BV + (h + 1) * HEAD_DIM].astype(BF16)
        bv_ref[0, :, h * LANES:(h + 1) * LANES] = jnp.concatenate([vh, ones], axis=1)
    cu_ref[0] = px[:, OFF_CU:OFF_G]


def _inproj(xs, modr, g1, w_qkv, layer, bd, gqk, cos_t, sin_t, nct):
    B, T, D = xs.shape
    nt = T // TM
    tok = lambda w: pl.BlockSpec((1, TM, w), lambda b, i: (b, i, 0))
    out_shape = (
        jax.ShapeDtypeStruct((B, T, A_Q), BF16),
        jax.ShapeDtypeStruct((B, nt, A_KV, TM), BF16),
        jax.ShapeDtypeStruct((B, T, A_KV_HEADS * LANES), BF16),
        jax.ShapeDtypeStruct((B, T, B_W), BF16),
        jax.ShapeDtypeStruct((B, T, B_W), BF16),
        jax.ShapeDtypeStruct((B, T, B_HEADS * LANES), BF16),
        jax.ShapeDtypeStruct((B, T, C_WIDTH), F32),
    )
    return pl.pallas_call(
        _inproj_kernel,
        grid=(B, nt),
        in_specs=[tok(D),
                  pl.BlockSpec((1, 6, D), lambda b, i: (jnp.where(i < nct, B, b), 0, 0)),
                  _const_spec((1, D)),
                  _layer_spec((D, OFF_G), layer),
                  _const_spec((QK_W, QK_W)),
                  _const_spec((1, QK_W)),
                  pl.BlockSpec((TM, LANES), lambda b, i: (i, 0)),
                  pl.BlockSpec((TM, LANES), lambda b, i: (i, 0))],
        out_specs=(tok(A_Q),
                   pl.BlockSpec((1, 1, A_KV, TM), lambda b, i: (b, i, 0, 0)),
                   tok(A_KV_HEADS * LANES), tok(B_W), tok(B_W), tok(B_HEADS * LANES), tok(C_WIDTH)),
        out_shape=out_shape,
        compiler_params=_cparams(("arbitrary", "arbitrary")),
        name="inproj",
    )(xs, modr, g1, w_qkv, bd, gqk, cos_t, sin_t)


def _attn_a_kernel(q_ref, kt_ref, v_ref, o_ref, *, nct, n_steps):
    i = pl.program_id(1)
    rows = A_GROUP * TM

    def step(carry, q4, j, blk0, nblk):
        m, acc = carry
        s_list = [_dot(q4, kt_ref[0, blk0 + c, j * HEAD_DIM:(j + 1) * HEAD_DIM, :]) for c in range(nblk)]
        smax = s_list[0]
        for s in s_list[1:]:
            smax = jnp.maximum(smax, s)
        m_new = jnp.maximum(m, jnp.max(smax, axis=-1, keepdims=True))
        alpha = jnp.exp2(m - m_new)
        p = jnp.concatenate([jnp.exp2((s - m_new).astype(BF16)) for s in s_list], axis=1)
        vv = v_ref[0, pl.ds(pl.multiple_of(blk0 * TM, TM), nblk * TM), j * LANES:(j + 1) * LANES]
        return m_new, alpha * acc + _dot(p, vv)

    def run(n_main):
        for j in range(A_KV_HEADS):
            base = j * A_GROUP * HEAD_DIM
            q4 = jnp.concatenate(
                [q_ref[0, :, base + g * HEAD_DIM: base + (g + 1) * HEAD_DIM] for g in range(A_GROUP)], axis=0)
            carry = (jnp.full((rows, 1), MASK_VALUE, F32), jnp.zeros((rows, LANES), F32))
            if n_main:
                carry = step(carry, q4, j, 0, nct + A_KBLK)
                carry = lax.fori_loop(
                    1, n_main, lambda k, cr: step(cr, q4, j, nct + k * A_KBLK, A_KBLK), carry)
            else:
                carry = step(carry, q4, j, 0, nct)
            acc = carry[1]
            o = acc[:, 0:HEAD_DIM] / acc[:, HEAD_DIM:HEAD_DIM + 1]
            for g in range(A_GROUP):
                o_ref[0, :, base + g * HEAD_DIM: base + (g + 1) * HEAD_DIM] = o[g * TM:(g + 1) * TM].astype(BF16)

    @pl.when(i < nct)
    def _():
        run(0)

    @pl.when(i >= nct)
    def _():
        run(n_steps)


def _attn_a(q, kt, v, nct):
    B, T, _ = q.shape
    nt = T // TM
    assert (nt - nct) % A_KBLK == 0
    return pl.pallas_call(
        functools.partial(_attn_a_kernel, nct=nct, n_steps=(nt - nct) // A_KBLK),
        grid=(B, nt),
        in_specs=[pl.BlockSpec((1, TM, A_Q), lambda b, i: (b, i, 0)),
                  pl.BlockSpec((1, nt, A_KV, TM), lambda b, i: (b, 0, 0, 0)),
                  pl.BlockSpec((1, T, A_KV_HEADS * LANES), lambda b, i: (b, 0, 0))],
        out_specs=pl.BlockSpec((1, TM, A_Q), lambda b, i: (b, i, 0)),
        out_shape=jax.ShapeDtypeStruct((B, T, A_Q), BF16),
        compiler_params=_cparams(("arbitrary", "arbitrary")),
        name="attn_a",
    )(q, kt, v)


def _attn_b_kernel(q_ref, kp_ref, kc_ref, kn_ref, vp_ref, vc_ref, vn_ref, kx_ref, vx_ref, bias_ref,
                   o_ref, kbuf, vbuf, *, nct, L):
    i = pl.program_id(1)

    def normalised(o_ext):
        return (o_ext[:, 0:HEAD_DIM] / o_ext[:, HEAD_DIM:HEAD_DIM + 1]).astype(BF16)

    @pl.when(i < nct)
    def _():
        for h in range(B_HEADS):
            sl = slice(h * HEAD_DIM, (h + 1) * HEAD_DIM)
            vl = slice(h * LANES, (h + 1) * LANES)
            s = _dot_nt(q_ref[0, :, sl], kx_ref[0, :, sl])
            m = jnp.max(s, axis=-1, keepdims=True)
            p = jnp.exp2((s - m).astype(BF16))
            o_ref[0, :, sl] = normalised(_dot(p, vx_ref[0, :, vl]))

    @pl.when(i >= nct)
    def _():
        for buf, cx, pv, cu, nx in ((kbuf, kx_ref, kp_ref, kc_ref, kn_ref), (vbuf, vx_ref, vp_ref, vc_ref, vn_ref)):
            buf[0:L] = cx[0]
            buf[L:L + TM] = pv[0]
            buf[L + TM:L + 2 * TM] = cu[0]
            buf[L + 2 * TM:L + 3 * TM] = nx[0]
        for h in range(B_HEADS):
            sl = slice(h * HEAD_DIM, (h + 1) * HEAD_DIM)
            s = _dot_nt(q_ref[0, :, sl], kbuf[:, sl])
            s_c = s[:, 0:L]
            s_w = s[:, L:] + bias_ref[0, 0, h]
            m = jnp.maximum(jnp.max(s_w, axis=-1, keepdims=True), jnp.max(s_c, axis=-1, keepdims=True))
            p = jnp.concatenate([jnp.exp2((s_c - m).astype(BF16)), jnp.exp2((s_w - m).astype(BF16))], axis=1)
            o_ref[0, :, sl] = normalised(_dot(p, vbuf[:, h * LANES:(h + 1) * LANES]))


def _attn_b(bq, bk, bv, bias_t, layer, nct, L):
    B, T, _ = bq.shape
    nt = T // TM
    assert L % LANES == 0 and nt - nct >= 2
    VW = B_HEADS * LANES
    case = lambda i: jnp.where(i <= nct, 0, jnp.where(i == nt - 1, 2, 1))
    cur = lambda b, i: (b, i, 0)
    prev = lambda b, i: (b, jnp.maximum(i - 1, nct), 0)
    nxt = lambda b, i: (b, jnp.minimum(i + 1, nt - 1), 0)
    blk = lambda f: pl.BlockSpec((1, TM, B_W), f)
    vblk = lambda f: pl.BlockSpec((1, TM, VW), f)
    ctx = lambda w: pl.BlockSpec((1, L, w), lambda b, i: (b, 0, 0))
    return pl.pallas_call(
        functools.partial(_attn_b_kernel, nct=nct, L=L),
        grid=(B, nt),
        in_specs=[blk(cur), blk(prev), blk(cur), blk(nxt), vblk(prev), vblk(cur), vblk(nxt), ctx(B_W), ctx(VW),
                  pl.BlockSpec((1, 1) + bias_t.shape[2:], lambda b, i: (layer, case(i), 0, 0, 0))],
        out_specs=blk(cur),
        out_shape=jax.ShapeDtypeStruct((B, T, B_W), BF16),
        scratch_shapes=[pltpu.VMEM((L + 3 * TM, B_W), BF16), pltpu.VMEM((L + 3 * TM, VW), BF16)],
        compiler_params=_cparams(("arbitrary", "arbitrary")),
        name="attn_b",
    )(bq, bk, bk, bk, bv, bv, bv, bk, bv, bias_t)


def _merge_kernel(*refs, nct, nt, L, S, with_router):
    if with_router:
        (ya_ref, yb_ref, cup_ref, cu_ref, cun_ref, x_ref, mod_ref, g1_ref, wg_ref, wpa_ref, wpb_ref, wpc_ref,
         pbd_ref, psc_ref, wo_ref, g2_ref, rw_ref, rb_ref,
         xo_ref, h_ref, cmb_ref, sel_ref, e_scr) = refs
    else:
        (ya_ref, yb_ref, cup_ref, cu_ref, cun_ref, x_ref, mod_ref, g1_ref, wg_ref, wpa_ref, wpb_ref, wpc_ref,
         pbd_ref, psc_ref, wo_ref, g2_ref, xo_ref, h_ref, e_scr) = refs
    i = pl.program_id(1)
    D = D_MODEL
    m = mod_ref[0]
    x_in = x_ref[0]

    at_start = (i == 0) | (i == nct)
    at_end = (i == nct - 1) | (i == nt - 1)
    u = cu_ref[0]
    e_scr[0:POOL_HALO] = jnp.where(at_start, 0.0, cup_ref[0])
    e_scr[POOL_HALO:POOL_HALO + TM] = u
    e_scr[POOL_HALO + TM:] = jnp.where(at_end, 0.0, cun_ref[0])
    sh = lambda k: e_scr[POOL_HALO + k:POOL_HALO + k + TM]
    grp = lax.broadcasted_iota(jnp.int32, (1, C_WIDTH), 1) // C_GROUP_DIM
    t_loc = lax.broadcasted_iota(jnp.int32, (TM, 1), 0)
    t_seq = jnp.where(i < nct, i * TM, (i - nct) * TM) + t_loc
    n_seq = jnp.where(i < nct, L, S)
    wsum, running, lo, hi = None, None, 0, 0
    lo_v = jnp.zeros((1, C_WIDTH), jnp.int32)
    hi_v = jnp.zeros((1, C_WIDTH), jnp.int32)
    for g, w in enumerate(POOL_WINDOWS):
        for k in list(range(-(w // 2), lo)) + list(range(hi, w - w // 2)):
            running = sh(k) if running is None else running + sh(k)
        lo, hi = -(w // 2), w - w // 2
        wsum = running if wsum is None else jnp.where(grp == g, running, wsum)
        lo_v = jnp.where(grp == g, lo, lo_v)
        hi_v = jnp.where(grp == g, hi, hi_v)
    cnt = jnp.minimum(t_seq + hi_v, n_seq) - jnp.maximum(t_seq + lo_v, 0)
    dlt = wsum / cnt.astype(F32) - u
    yc = _dot(dlt.astype(BF16), pbd_ref[0]) * psc_ref[...]

    h1 = _rms_mod(x_in, g1_ref[...], m[0:1], m[1:2]).astype(BF16)
    branches = (ya_ref[0], yb_ref[0], yc.astype(BF16))
    weights = (wpa_ref, wpb_ref, wpc_ref)
    mrg = None
    for k in range(3):
        gate = jax.nn.sigmoid(_dot(h1, wg_ref[0, :, k * D:(k + 1) * D]))
        term = gate * _dot(branches[k], weights[k][0])
        mrg = term if mrg is None else mrg + term
    x = x_in + m[2:3] * _dot(mrg.astype(BF16), wo_ref[0])
    xo_ref[0] = x
    h = _rms_mod(x, g2_ref[...], m[3:4], m[4:5])
    h_ref[0] = h.astype(BF16)

    if with_router:
        lane = lax.broadcasted_iota(jnp.int32, (1, LANES), 1).astype(F32)
        h_hi, h_lo = _split(h)
        d_hi = _dot(h_hi, rw_ref[...])
        d_lo = _dot(h_lo, rw_ref[...])
        lg = d_hi[:, :LANES] + (d_hi[:, LANES:] + d_lo[:, :LANES]) + rb_ref[...]
        lg = jnp.where(lane < N_EXPERTS, lg, -jnp.inf)
        m1 = jnp.max(lg, axis=-1, keepdims=True)
        i1 = jnp.min(jnp.where(lg == m1, lane, float(LANES)), axis=-1, keepdims=True)
        mask1 = lane == i1
        lg2 = jnp.where(mask1, -jnp.inf, lg)
        m2 = jnp.max(lg2, axis=-1, keepdims=True)
        i2 = jnp.min(jnp.where(lg2 == m2, lane, float(LANES)), axis=-1, keepdims=True)
        mask2 = lane == i2
        e2 = jnp.exp(m2 - m1)
        den = 1.0 + e2
        cmb_ref[0] = jnp.where(mask1, 1.0 / den, 0.0) + jnp.where(mask2, e2 / den, 0.0)
        sel_ref[0] = jnp.where(mask1 | mask2, 1.0, 0.0)


def _merge(ya, yb, cu, xs, modr, g1, w_gate, wpa, wpb, wpc, pbd, psc, wo, g2, layer, router, nct, L):
    B, T, D = xs.shape
    nt = T // TM
    S = T - L
    hb = TM // POOL_HALO
    tok = lambda w: pl.BlockSpec((1, TM, w), lambda b, i: (b, i, 0))
    in_specs = [tok(A_Q), tok(B_W),
                pl.BlockSpec((1, POOL_HALO, C_WIDTH), lambda b, i: (b, jnp.maximum(i * hb - 1, 0), 0)),
                tok(C_WIDTH),
                pl.BlockSpec((1, POOL_HALO, C_WIDTH), lambda b, i: (b, jnp.minimum((i + 1) * hb, nt * hb - 1), 0)),
                tok(D),
                pl.BlockSpec((1, 6, D), lambda b, i: (jnp.where(i < nct, B, b), 0, 0)),
                _const_spec((1, D)),
                _layer_spec((D, 3 * D), layer),
                _layer_spec((A_Q, D), layer), _layer_spec((B_W, D), layer), _layer_spec((C_WIDTH, D), layer),
                _layer_spec((C_WIDTH, C_WIDTH), layer), _const_spec((1, C_WIDTH)), _layer_spec((D, D), layer),
                _const_spec((1, D))]
    args = [ya, yb, cu, cu, cu, xs, modr, g1, w_gate, wpa, wpb, wpc, pbd, psc, wo, g2]
    out_specs = [tok(D), tok(D)]
    out_shape = [jax.ShapeDtypeStruct((B, T, D), F32), jax.ShapeDtypeStruct((B, T, D), BF16)]
    if router is not None:
        in_specs += [_const_spec((D, 2 * LANES)), _const_spec((1, LANES))]
        args += list(router)
        out_specs += [tok(LANES), tok(LANES)]
        out_shape += [jax.ShapeDtypeStruct((B, T, LANES), F32)] * 2
    return pl.pallas_call(
        functools.partial(_merge_kernel, nct=nct, nt=nt, L=L, S=S, with_router=router is not None),
        grid=(B, nt),
        in_specs=in_specs,
        out_specs=tuple(out_specs),
        out_shape=tuple(out_shape),
        scratch_shapes=[pltpu.VMEM((TM + 2 * POOL_HALO, C_WIDTH), F32)],
        compiler_params=_cparams(("arbitrary", "arbitrary")),
        name="merge_router" if router is not None else "merge",
    )(*args)


def _swiglu_acc(h, w1_ref, w3_ref, w2_ref, lead):
    d_ff = w1_ref.shape[-1]
    acc = None
    for f0 in range(0, d_ff, FC):
        f1 = min(f0 + FC, d_ff)
        a = _dot(h, w1_ref[lead + (slice(None), slice(f0, f1))])
        b = _dot(h, w3_ref[lead + (slice(None), slice(f0, f1))])
        t = (a * jax.nn.sigmoid(a) * b).astype(BF16)
        part = _dot(t, w2_ref[lead + (slice(f0, f1), slice(None))])
        acc = part if acc is None else acc + part
    return acc


def _ffn_kernel(h_ref, x_ref, g_ref, w1_ref, w3_ref, w2_ref, o_ref):
    f = _swiglu_acc(h_ref[0], w1_ref, w3_ref, w2_ref, (0,))
    o_ref[0] = x_ref[0] + g_ref[0] * f


def _ffn_dense(h2, xs, gate2, w1, w3, w2, idx, nct):
    B, T, D = xs.shape
    F = w1.shape[-1]
    nt = T // TM
    tok = lambda: pl.BlockSpec((1, TM, D), lambda b, i: (b, i, 0))
    return pl.pallas_call(
        _ffn_kernel,
        grid=(B, nt),
        in_specs=[tok(), tok(),
                  pl.BlockSpec((1, 1, D), lambda b, i: (jnp.where(i < nct, B, b), 0, 0)),
                  _layer_spec((D, F), idx), _layer_spec((D, F), idx), _layer_spec((F, D), idx)],
        out_specs=tok(),
        out_shape=jax.ShapeDtypeStruct((B, T, D), F32),
        compiler_params=_cparams(("arbitrary", "arbitrary")),
        name="ffn_dense",
    )(h2, xs, gate2, w1, w3, w2)


def _gather_kernel(wj_ref, ws_ref, wf_ref, wl_ref, wv_ref, te_ref, pos_ref, cmb_ref, h_ref,
                   o_ref, pw_ref, acc_ref, accw_ref):
    w = pl.program_id(0)

    @pl.when(wf_ref[w] == 1)
    def _():
        acc_ref[...] = jnp.zeros_like(acc_ref)
        accw_ref[...] = jnp.zeros_like(accw_ref)

    @pl.when(wv_ref[w] == 1)
    def _():
        row = lax.broadcasted_iota(jnp.int32, (TX, 1), 0) + wj_ref[w] * TX
        hit = pos_ref[0, 0] == row
        acc_ref[...] += _dot(jnp.where(hit, 1.0, 0.0).astype(BF16), h_ref[...])
        accw_ref[...] += jnp.sum(jnp.where(hit, cmb_ref[0, 0], 0.0), axis=-1, keepdims=True)

    @pl.when(wl_ref[w] == 1)
    def _():
        o_ref[...] = acc_ref[...].astype(BF16)
        pw_ref[...] = accw_ref[...]


def _moe_gather(work, tile_expert, pos_t, cmb_t, h2f, n_tiles):
    wj, ws, wf, wl, wv = work
    n, D = h2f.shape
    row_spec = pl.BlockSpec((1, 1, 1, TX), lambda w, wj, ws, wf, wl, wv, te: (te[wj[w]], ws[w], 0, 0))
    return pl.pallas_call(
        _gather_kernel,
        grid_spec=pltpu.PrefetchScalarGridSpec(
            num_scalar_prefetch=6,
            grid=(wj.shape[0],),
            in_specs=[row_spec, row_spec,
                      pl.BlockSpec((TX, D), lambda w, wj, ws, wf, wl, wv, te: (ws[w], 0))],
            out_specs=(pl.BlockSpec((TX, D), lambda w, wj, ws, wf, wl, wv, te: (wj[w], 0)),
                       pl.BlockSpec((TX, 1), lambda w, wj, ws, wf, wl, wv, te: (wj[w], 0))),
            scratch_shapes=[pltpu.VMEM((TX, D), F32), pltpu.VMEM((TX, 1), F32)]),
        out_shape=(jax.ShapeDtypeStruct((n_tiles * TX, D), BF16),
                   jax.ShapeDtypeStruct((n_tiles * TX, 1), F32)),
        compiler_params=_cparams(("arbitrary",)),
        name="moe_gather",
    )(wj, ws, wf, wl, wv, tile_expert, pos_t, cmb_t, h2f)


def _gffn_kernel(te_ref, tv_ref, x_ref, pw_ref, w1_ref, w3_ref, w2_ref, o_ref):
    j = pl.program_id(0)

    @pl.when(tv_ref[j] == 1)
    def _():
        f = _swiglu_acc(x_ref[...], w1_ref, w3_ref, w2_ref, (0, 0))
        o_ref[...] = (f * pw_ref[...]).astype(BF16)

    @pl.when(tv_ref[j] == 0)
    def _():
        o_ref[...] = jnp.zeros_like(o_ref)


def _moe_gffn(tile_expert, tile_valid, xg, pw, w1, w3, w2, idx):
    P, D = xg.shape
    F = w1.shape[-1]
    wspec = lambda shape: pl.BlockSpec((1, 1) + shape, lambda j, te, tv: (idx, te[j], 0, 0))
    return pl.pallas_call(
        _gffn_kernel,
        grid_spec=pltpu.PrefetchScalarGridSpec(
            num_scalar_prefetch=2,
            grid=(P // TX,),
            in_specs=[pl.BlockSpec((TX, D), lambda j, te, tv: (j, 0)),
                      pl.BlockSpec((TX, 1), lambda j, te, tv: (j, 0)),
                      wspec((D, F)), wspec((D, F)), wspec((F, D))],
            out_specs=pl.BlockSpec((TX, D), lambda j, te, tv: (j, 0))),
        out_shape=jax.ShapeDtypeStruct((P, D), BF16),
        compiler_params=_cparams(("arbitrary",)),
        name="moe_gffn",
    )(tile_expert, tile_valid, xg, pw, w1, w3, w2)


def _combine_kernel(wt_ref, ws_ref, wf_ref, wl_ref, wv_ref, pos_ref, y_ref, x_ref, g_ref, o_ref, acc_ref,
                    *, n_batch, T, L):
    w = pl.program_id(0)

    @pl.when(wf_ref[w] == 1)
    def _():
        acc_ref[...] = jnp.zeros_like(acc_ref)

    @pl.when(wv_ref[w] == 1)
    def _():
        col = lax.broadcasted_iota(jnp.int32, (1, TX), 1) + ws_ref[w] * TX
        pos = pos_ref[...]
        hit = (pos[:, 0:1] == col) | (pos[:, 1:2] == col)
        acc_ref[...] += _dot(jnp.where(hit, 1.0, 0.0).astype(BF16), y_ref[...])

    @pl.when(wl_ref[w] == 1)
    def _():
        n = lax.broadcasted_iota(jnp.int32, (TX, 1), 0) + wt_ref[w] * TX
        b = jnp.zeros((TX, 1), jnp.int32)
        for k in range(1, n_batch):
            b = b + (n >= k * T).astype(jnp.int32)
        rowid = jnp.where(n - b * T < L, n_batch, b)
        gate = jnp.zeros((TX, D_MODEL), F32)
        for r in range(n_batch + 1):
            gate = jnp.where(rowid == r, g_ref[r:r + 1, :], gate)
        o_ref[...] = x_ref[...] + gate * acc_ref[...]


def _moe_combine(work, pos2, yw, xf, gate2, n_batch, T, L):
    wt, ws, wf, wl, wv = work
    n, D = xf.shape
    return pl.pallas_call(
        functools.partial(_combine_kernel, n_batch=n_batch, T=T, L=L),
        grid_spec=pltpu.PrefetchScalarGridSpec(
            num_scalar_prefetch=5,
            grid=(wt.shape[0],),
            in_specs=[pl.BlockSpec((TX, 2), lambda w, wt, ws, wf, wl, wv: (wt[w], 0)),
                      pl.BlockSpec((TX, D), lambda w, wt, ws, wf, wl, wv: (ws[w], 0)),
                      pl.BlockSpec((TX, D), lambda w, wt, ws, wf, wl, wv: (wt[w], 0)),
                      pl.BlockSpec((SUBLANES, D), lambda w, wt, ws, wf, wl, wv: (0, 0))],
            out_specs=pl.BlockSpec((TX, D), lambda w, wt, ws, wf, wl, wv: (wt[w], 0)),
            scratch_shapes=[pltpu.VMEM((TX, D), F32)]),
        out_shape=jax.ShapeDtypeStruct((n, D), F32),
        compiler_params=_cparams(("arbitrary",)),
        name="moe_combine",
    )(wt, ws, wf, wl, wv, pos2, yw, xf, gate2)


def _count_below(sorted_vals, x):
    return jnp.sum((sorted_vals[None, :] < x[:, None]).astype(jnp.int32), axis=1)


def _with_flags(g, it, compute, n_work, total):
    live = jnp.arange(n_work, dtype=jnp.int32) < total
    g_prev = jnp.concatenate([jnp.full((1,), -1, jnp.int32), g[:-1]])
    g_next = jnp.concatenate([g[1:], jnp.full((1,), -1, jnp.int32)])
    live_next = jnp.concatenate([live[1:], jnp.zeros((1,), bool)])
    first = live & (g != g_prev)
    last = live & ((g != g_next) | ~live_next)
    i32 = lambda a: a.astype(jnp.int32)
    return i32(g), i32(it), i32(first), i32(last), i32(live & compute)


def _moe_plan(sel, n_tok):
    E = N_EXPERTS
    sel = sel.astype(jnp.int32)
    cnt = jnp.sum(sel, axis=0)
    rank = jnp.cumsum(sel, axis=0) - 1
    gsz = ((cnt + TX - 1) // TX) * TX
    gend = jnp.cumsum(gsz)
    goff = gend - gsz
    n_tiles = (2 * n_tok + E * (TX - 1) + TX - 1) // TX
    P = n_tiles * TX
    pos = jnp.where(sel == 1, goff[None, :] + rank, -1)
    tile_start = jnp.arange(n_tiles, dtype=jnp.int32) * TX
    tile_valid = tile_start < gend[-1]
    tile_expert = jnp.minimum(_count_below(gend, tile_start + 1), E - 1)
    pmax_tok = jnp.max(pos, axis=1)
    psec_tok = jnp.max(jnp.where(pos == pmax_tok[:, None], -1, pos), axis=1)
    pos2 = jnp.stack([psec_tok, pmax_tok], axis=1)

    n_src_tiles = n_tok // TX
    cin = rank[TX - 1::TX] + 1
    ra = tile_start - goff[tile_expert]
    rb = ra + jnp.clip(cnt[tile_expert] - ra, 0, TX) - 1
    cin_t = cin.T[tile_expert]
    lo = jnp.sum((cin_t <= ra[:, None]).astype(jnp.int32), axis=1)
    hi = jnp.sum((cin_t <= rb[:, None]).astype(jnp.int32), axis=1)
    span = jnp.where(tile_valid, hi - lo + 1, 1)
    n_gw = E * n_src_tiles + n_tiles
    cs = jnp.cumsum(span)
    total = cs[-1]
    w = jnp.minimum(jnp.arange(n_gw, dtype=jnp.int32), total - 1)
    jw = _count_below(cs, w + 1)
    sw = jnp.where(tile_valid[jw], lo[jw] + (w - (cs[jw] - span[jw])), 0)
    gwork = _with_flags(jw, sw, tile_valid[jw], n_gw, total)

    pt = pos.reshape(n_src_tiles, TX, E)
    pmax = jnp.max(pt, axis=1)
    pmin = jnp.min(jnp.where(pt >= 0, pt, P), axis=1)
    ta = pmin // TX
    tb = pmax // TX
    cand_item = jnp.stack([ta, tb], axis=-1).reshape(-1)
    cand_valid = jnp.stack([pmax >= 0, (pmax >= 0) & (tb != ta)], axis=-1).reshape(-1).astype(jnp.int32)
    cand_group = jnp.repeat(jnp.arange(n_src_tiles, dtype=jnp.int32), 2 * E)
    n_cw = E * n_src_tiles + n_tiles
    ccs = jnp.cumsum(cand_valid)
    ctotal = ccs[-1]
    cw = jnp.minimum(jnp.arange(n_cw, dtype=jnp.int32), ctotal - 1)
    cidx = _count_below(ccs, cw + 1)
    cwork = _with_flags(cand_group[cidx], cand_item[cidx], jnp.ones((n_cw,), bool), n_cw, ctotal)
    return pos, pos2, tile_expert, tile_valid.astype(jnp.int32), gwork, cwork, n_tiles


def _moe(h2, xs1, cmb, sel, gate2, w1, w3, w2, idx, L):
    B, T, D = xs1.shape
    n_tok = B * T
    E = N_EXPERTS
    assert n_tok % TX == 0
    selm = sel.reshape(n_tok, LANES)[:, :E] > 0.5
    cmbm = cmb.reshape(n_tok, LANES)[:, :E]
    pos, pos2, tile_expert, tile_valid, gwork, cwork, n_tiles = _moe_plan(selm, n_tok)
    pos_t = pos.T.reshape(E, n_tok // TX, 1, TX)
    cmb_t = cmbm.T.reshape(E, n_tok // TX, 1, TX)

    xg, pw = _moe_gather(gwork, tile_expert, pos_t, cmb_t, h2.reshape(n_tok, D), n_tiles)
    yw = _moe_gffn(tile_expert, tile_valid, xg, pw, w1, w3, w2, idx)
    out = _moe_combine(cwork, pos2, yw, xs1.reshape(n_tok, D), gate2, B, T, L)
    return out.reshape(B, T, D)


def _final_kernel(x_ref, g_ref, o_ref):
    x = x_ref[0]
    o_ref[0] = x * lax.rsqrt(jnp.mean(x * x, axis=-1, keepdims=True) + EPS) * g_ref[...]


def _final_norm(xs, g, nct, S):
    B, T, D = xs.shape
    return pl.pallas_call(
        _final_kernel,
        grid=(B, S // TM),
        in_specs=[pl.BlockSpec((1, TM, D), lambda b, i: (b, i + nct, 0)), _const_spec((1, D))],
        out_specs=pl.BlockSpec((1, TM, D), lambda b, i: (b, i, 0)),
        out_shape=jax.ShapeDtypeStruct((B, S, D), F32),
        compiler_params=_cparams(("arbitrary", "arbitrary")),
        name="final_norm",
    )(xs, g)


def _rope_tables(L, S):
    t = np.arange(S)
    pos = np.stack([t // GRID_W, t % GRID_W], axis=-1).astype(np.float32)
    inv_freq = (ROPE_THETA ** (-np.arange(0, AXIS_ROPE_DIM, 2, dtype=np.float32) / AXIS_ROPE_DIM)).astype(np.float32)
    ang = pos[:, :, None] * inv_freq[None, None, :]
    cos, sin = np.cos(ang), np.sin(ang)
    cos64 = np.concatenate([cos[:, 0], cos[:, 0], cos[:, 1], cos[:, 1]], axis=-1)
    sin64 = np.concatenate([-sin[:, 0], sin[:, 0], -sin[:, 1], sin[:, 1]], axis=-1)
    cos_t = np.concatenate([np.ones((L, HEAD_DIM), np.float32), cos64], axis=0)
    sin_t = np.concatenate([np.zeros((L, HEAD_DIM), np.float32), sin64], axis=0)
    rep = LANES // HEAD_DIM
    return jnp.asarray(np.tile(cos_t, (1, rep)), F32), jnp.asarray(np.tile(sin_t, (1, rep)), F32)


def _na_bias_table(rel_bias):
    rpt = TM // GRID_W
    half = NA_WIN_ROWS // 2
    assert half <= rpt and rpt - 1 - half + NA_WIN_ROWS <= 2 * rpt and rpt + half + 1 >= NA_WIN_ROWS
    col = np.arange(GRID_W)
    col_start = np.clip(col - NA_WIN_COLS // 2, 0, GRID_W - NA_WIN_COLS)
    kc = np.arange(GRID_W)
    inside = (kc[None, :] >= col_start[:, None]) & (kc[None, :] < col_start[:, None] + NA_WIN_COLS)
    relp = jnp.pad(rel_bias, ((0, 0), (0, 0), (0, 0), (GRID_W, GRID_W)))
    off = GRID_W + NA_WIN_COLS - 1
    cmat = jnp.stack([relp[..., off - qc:off - qc + GRID_W] for qc in range(GRID_W)], axis=3)
    cmat = jnp.where(jnp.asarray(inside), cmat * LOG2E, MASK_VALUE)
    masked = jnp.full(cmat.shape[:2] + (GRID_W, GRID_W), MASK_VALUE, F32)

    cases = []
    for case in range(3):
        rows = []
        for a in range(rpt):
            first = (rpt + max(a - half, 0),
                     rpt + a - half,
                     min(rpt + a - half, 2 * rpt - NA_WIN_ROWS))[case]
            blocks = [cmat[:, :, t - (rpt + a) + NA_WIN_ROWS - 1] if first <= t < first + NA_WIN_ROWS else masked
                      for t in range(3 * rpt)]
            rows.append(jnp.concatenate(blocks, axis=-1))
        cases.append(jnp.concatenate(rows, axis=-2))
    return jnp.stack(cases, axis=1)


def _block_diag_ones(n, blk):
    idx = np.arange(n) // blk
    return jnp.asarray((idx[:, None] == idx[None, :]).astype(np.float32), BF16)


def _pool_block_diag(pool_w):
    depth, g, c, d = pool_w.shape
    eye = jnp.asarray(np.eye(g, dtype=np.float32))
    out = pool_w[:, :, :, None, :] * eye[None, :, None, :, None]
    return out.reshape(depth, g * c, g * d).astype(BF16)


def kernel(x, c, ctx, c_ctx, w_mod, b_mod, norm1_g, norm2_g, w_in, q_norm_g, k_norm_g, na_rel_bias, pool_w,
           pool_scale, w_branch_a, w_branch_b, w_branch_c, w_out, ffn_w1, ffn_w3, ffn_w2, router_w, router_b,
           moe_w1, moe_w3, moe_w2, final_g):
    B, S, D = x.shape
    L = ctx.shape[1]
    T = L + S
    depth = w_mod.shape[0]
    assert D == D_MODEL and L % TM == 0 and S % TM == 0 and TM % GRID_W == 0 and B + 1 <= SUBLANES
    assert S // GRID_W >= NA_WIN_ROWS and len(POOL_WINDOWS) == C_GROUPS
    assert list(POOL_WINDOWS) == sorted(POOL_WINDOWS) and POOL_WINDOWS[-1] - POOL_WINDOWS[-1] // 2 <= POOL_HALO
    nct = L // TM

    cvec = jnp.zeros((SUBLANES, D), F32).at[:B].set(c).at[B].set(c_ctx)
    mod = _mod_vectors(cvec, w_mod, b_mod)
    cos_t, sin_t = _rope_tables(L, S)
    bd = _block_diag_ones(QK_W, HEAD_DIM)
    na_bias = _na_bias_table(na_rel_bias)
    bf = lambda a: a.astype(BF16)
    w_qkv, w_gate = bf(w_in[:, :, :OFF_G]), bf(w_in[:, :, OFF_G:])
    wpa, wpb, wpc, wo, pbd = bf(w_branch_a), bf(w_branch_b), bf(w_branch_c), bf(w_out), _pool_block_diag(pool_w)
    fw1, fw3, fw2 = bf(ffn_w1), bf(ffn_w3), bf(ffn_w2)
    mw1, mw3, mw2 = bf(moe_w1), bf(moe_w3), bf(moe_w2)

    xs = jnp.concatenate([ctx, x], axis=1)
    for layer in range(depth):
        modr = mod[layer].reshape(SUBLANES, 6, D)
        g1 = norm1_g[layer].reshape(1, D)
        gqk = jnp.concatenate([jnp.tile(q_norm_g[layer], A_HEADS), jnp.tile(k_norm_g[layer], A_KV_HEADS)])
        q, kt, v, bq, bk, bv, cu = _inproj(xs, modr, g1, w_qkv, layer, bd, gqk.reshape(1, QK_W), cos_t, sin_t, nct)
        ya = _attn_a(q, kt, v, nct)
        yb = _attn_b(bq, bk, bv, na_bias, layer, nct, L)
        is_moe = layer % 2 == 1
        i = layer // 2
        router = None
        if is_moe:
            rw = jnp.zeros((D, LANES), F32).at[:, :N_EXPERTS].set(router_w[i])
            rw_hi = rw.astype(BF16)
            rw_lo = (rw - rw_hi.astype(F32)).astype(BF16)
            rb = jnp.zeros((1, LANES), F32).at[0, :N_EXPERTS].set(router_b[i])
            router = (jnp.concatenate([rw_hi, rw_lo], axis=1), rb)
        outs = _merge(ya, yb, cu, xs, modr, g1, w_gate, wpa, wpb, wpc, pbd, pool_scale[layer].reshape(1, C_WIDTH),
                      wo, norm2_g[layer].reshape(1, D), layer, router, nct, L)
        if is_moe:
            xs1, h2, cmb, sel = outs
            xs = _moe(h2, xs1, cmb, sel, modr[:, 5, :], mw1, mw3, mw2, i, L)
        else:
            xs1, h2 = outs
            xs = _ffn_dense(h2, xs1, modr[:, 5:6, :], fw1, fw3, fw2, i, nct)
    return _final_norm(xs, final_g.reshape(1, D), nct, S)
```

```python
import functools

import numpy as np
import jax
import jax.numpy as jnp
from jax import lax
from jax.experimental import pallas as pl
from jax.experimental.pallas import tpu as pltpu

F32 = jnp.float32
BF16 = jnp.bfloat16

D_MODEL = 1024
DEPTH = 4
GRID_W = 64
HEAD_DIM = 64
A_HEADS = 8
A_KV_HEADS = 2
A_GROUP = A_HEADS // A_KV_HEADS
B_HEADS = 4
C_GROUPS = 4
C_GROUP_DIM = 64
C_WIDTH = C_GROUPS * C_GROUP_DIM
POOL_WINDOWS = (2, 4, 8, 16)
NA_WIN_ROWS = 8
NA_WIN_COLS = 16
ROPE_THETA = 10000.0
AXIS_ROPE_DIM = HEAD_DIM // 2
N_EXPERTS = 8
EPS = 1e-6

A_Q = A_HEADS * HEAD_DIM
A_KV = A_KV_HEADS * HEAD_DIM
B_W = B_HEADS * HEAD_DIM
QK_W = A_Q + A_KV
OFF_AV = QK_W
OFF_BQ = OFF_AV + A_KV
OFF_BK = OFF_BQ + B_W
OFF_BV = OFF_BK + B_W
OFF_CU = OFF_BV + B_W
OFF_G = OFF_CU + C_WIDTH

LANES = 128
SUBLANES = 8
TM = 256
TX = 512
FC = 512
A_KBLK = 8
LOG2E = 1.4426950408889634
POOL_HALO = 8
MASK_VALUE = -1e30
VMEM_LIMIT = 56 * 1024 * 1024


def _cparams(sem):
    return pltpu.CompilerParams(dimension_semantics=sem, vmem_limit_bytes=VMEM_LIMIT)


def _dot(a, b):
    return jnp.dot(a, b, preferred_element_type=F32)


def _dot_nt(a, b):
    return lax.dot_general(a, b, (((1,), (1,)), ((), ())), preferred_element_type=F32)


def _split(a):
    hi = a.astype(BF16)
    lo = (a - hi.astype(F32)).astype(BF16)
    return hi, lo


def _dot3(a, b_hi, b_lo):
    a_hi, a_lo = _split(a)
    return _dot(a_hi, b_hi) + (_dot(a_lo, b_hi) + _dot(a_hi, b_lo))


def _const_spec(shape):
    n = len(shape)
    return pl.BlockSpec(shape, lambda *_: (0,) * n, pipeline_mode=pl.Buffered(1))


def _layer_spec(shape, layer):
    n = len(shape)
    return pl.BlockSpec((1,) + tuple(shape), lambda *_: (layer,) + (0,) * n, pipeline_mode=pl.Buffered(1))


def _rms_mod(x, g, shift, scale):
    y = x * lax.rsqrt(jnp.mean(x * x, axis=-1, keepdims=True) + EPS) * g
    return y * (1.0 + scale) + shift


def _mod_kernel(c_ref, w_ref, b_ref, o_ref):
    c = c_ref[...]
    s = c * jax.nn.sigmoid(c)
    w_hi, w_lo = _split(w_ref[0])
    o_ref[0] = _dot3(s, w_hi, w_lo) + b_ref[0]


def _mod_vectors(cvec, w_mod, b_mod):
    depth, d, n = w_mod.shape
    tn = 1536
    return pl.pallas_call(
        _mod_kernel,
        grid=(depth, n // tn),
        in_specs=[pl.BlockSpec((SUBLANES, d), lambda l, j: (0, 0)),
                  pl.BlockSpec((1, d, tn), lambda l, j: (l, 0, j)),
                  pl.BlockSpec((1, 1, tn), lambda l, j: (l, 0, j))],
        out_specs=pl.BlockSpec((1, SUBLANES, tn), lambda l, j: (l, 0, j)),
        out_shape=jax.ShapeDtypeStruct((depth, SUBLANES, n), F32),
        compiler_params=_cparams(("arbitrary", "arbitrary")),
        name="mod_vectors",
    )(cvec, w_mod, b_mod.reshape(depth, 1, n))


def _rotate_half(x):
    half = AXIS_ROPE_DIM // 2
    lane = lax.broadcasted_iota(jnp.int32, (1, LANES), 1)
    first = (lane % AXIS_ROPE_DIM) < half
    return jnp.where(first, pltpu.roll(x, LANES - half, 1), pltpu.roll(x, half, 1))


def _inproj_kernel(x_ref, mod_ref, g_ref, w_ref, bd_ref, gqk_ref, cos_ref, sin_ref,
                   q_ref, kt_ref, v_ref, bq_ref, bk_ref, bv_ref, cu_ref):
    m = mod_ref[0]
    h = _rms_mod(x_ref[0], g_ref[...], m[0:1], m[1:2])
    px = _dot(h.astype(BF16), w_ref[0])

    qk = px[:, :QK_W]
    sq_hi, sq_lo = _split(qk * qk)
    ss = _dot(sq_hi, bd_ref[...]) + _dot(sq_lo, bd_ref[...])
    qn = qk * lax.rsqrt(ss * (1.0 / HEAD_DIM) + EPS) * gqk_ref[...]
    cos = cos_ref[...]
    sin = sin_ref[...]
    chunks = []
    for j in range(QK_W // LANES):
        c = qn[:, j * LANES:(j + 1) * LANES]
        chunks.append(c * cos + _rotate_half(c) * sin)
    scale = HEAD_DIM ** -0.5
    for j in range(A_Q // LANES):
        q_ref[0, :, j * LANES:(j + 1) * LANES] = (chunks[j] * (scale * LOG2E)).astype(BF16)
    kt_ref[0, 0] = chunks[A_Q // LANES].T.astype(BF16)

    ones = jnp.ones((TM, LANES - HEAD_DIM), BF16)
    for j in range(A_KV_HEADS):
        vj = px[:, OFF_AV + j * HEAD_DIM:OFF_AV + (j + 1) * HEAD_DIM].astype(BF16)
        v_ref[0, :, j * LANES:(j + 1) * LANES] = jnp.concatenate([vj, ones], axis=1)
    bq_ref[0] = (px[:, OFF_BQ:OFF_BK] * (scale * LOG2E)).astype(BF16)
    bk_ref[0] = px[:, OFF_BK:OFF_BV].astype(BF16)
    for h in range(B_HEADS):
        vh = px[:, OFF_BV + h * HEAD_DIM:OFF_BV + (h + 1) * HEAD_DIM].astype(BF16)
        bv_ref[0, :, h * LANES:(h + 1) * LANES] = jnp.concatenate([vh, ones], axis=1)
    cu_ref[0] = px[:, OFF_CU:OFF_G]


def _inproj(xs, modr, g1, w_qkv, layer, bd, gqk, cos_t, sin_t, nct):
    B, T, D = xs.shape
    nt = T // TM
    tok = lambda w: pl.BlockSpec((1, TM, w), lambda b, i: (b, i, 0))
    out_shape = (
        jax.ShapeDtypeStruct((B, T, A_Q), BF16),
        jax.ShapeDtypeStruct((B, nt, A_KV, TM), BF16),
        jax.ShapeDtypeStruct((B, T, A_KV_HEADS * LANES), BF16),
        jax.ShapeDtypeStruct((B, T, B_W), BF16),
        jax.ShapeDtypeStruct((B, T, B_W), BF16),
        jax.ShapeDtypeStruct((B, T, B_HEADS * LANES), BF16),
        jax.ShapeDtypeStruct((B, T, C_WIDTH), F32),
    )
    return pl.pallas_call(
        _inproj_kernel,
        grid=(B, nt),
        in_specs=[tok(D),
                  pl.BlockSpec((1, 6, D), lambda b, i: (jnp.where(i < nct, B, b), 0, 0)),
                  _const_spec((1, D)),
                  _layer_spec((D, OFF_G), layer),
                  _const_spec((QK_W, QK_W)),
                  _const_spec((1, QK_W)),
                  pl.BlockSpec((TM, LANES), lambda b, i: (i, 0)),
                  pl.BlockSpec((TM, LANES), lambda b, i: (i, 0))],
        out_specs=(tok(A_Q),
                   pl.BlockSpec((1, 1, A_KV, TM), lambda b, i: (b, i, 0, 0)),
                   tok(A_KV_HEADS * LANES), tok(B_W), tok(B_W), tok(B_HEADS * LANES), tok(C_WIDTH)),
        out_shape=out_shape,
        compiler_params=_cparams(("arbitrary", "arbitrary")),
        name="inproj",
    )(xs, modr, g1, w_qkv, bd, gqk, cos_t, sin_t)


def _attn_a_kernel(q_ref, kt_ref, v_ref, o_ref, *, nct, n_steps):
    i = pl.program_id(1)
    rows = A_GROUP * TM

    def step(carry, q4, j, blk0, nblk):
        m, acc = carry
        s_list = [_dot(q4, kt_ref[0, blk0 + c, j * HEAD_DIM:(j + 1) * HEAD_DIM, :]) for c in range(nblk)]
        smax = s_list[0]
        for s in s_list[1:]:
            smax = jnp.maximum(smax, s)
        m_new = jnp.maximum(m, jnp.max(smax, axis=-1, keepdims=True))
        alpha = jnp.exp2(m - m_new)
        p = jnp.concatenate([jnp.exp2((s - m_new).astype(BF16)) for s in s_list], axis=1)
        vv = v_ref[0, pl.ds(pl.multiple_of(blk0 * TM, TM), nblk * TM), j * LANES:(j + 1) * LANES]
        return m_new, alpha * acc + _dot(p, vv)

    def run(n_main):
        for j in range(A_KV_HEADS):
            base = j * A_GROUP * HEAD_DIM
            q4 = jnp.concatenate(
                [q_ref[0, :, base + g * HEAD_DIM: base + (g + 1) * HEAD_DIM] for g in range(A_GROUP)], axis=0)
            carry = (jnp.full((rows, 1), MASK_VALUE, F32), jnp.zeros((rows, LANES), F32))
            if n_main:
                carry = step(carry, q4, j, 0, nct + A_KBLK)
                carry = lax.fori_loop(
                    1, n_main, lambda k, cr: step(cr, q4, j, nct + k * A_KBLK, A_KBLK), carry)
            else:
                carry = step(carry, q4, j, 0, nct)
            acc = carry[1]
            o = acc[:, 0:HEAD_DIM] / acc[:, HEAD_DIM:HEAD_DIM + 1]
            for g in range(A_GROUP):
                o_ref[0, :, base + g * HEAD_DIM: base + (g + 1) * HEAD_DIM] = o[g * TM:(g + 1) * TM].astype(BF16)

    @pl.when(i < nct)
    def _():
        run(0)

    @pl.when(i >= nct)
    def _():
        run(n_steps)


def _attn_a(q, kt, v, nct):
    B, T, _ = q.shape
    nt = T // TM
    assert (nt - nct) % A_KBLK == 0
    return pl.pallas_call(
        functools.partial(_attn_a_kernel, nct=nct, n_steps=(nt - nct) // A_KBLK),
        grid=(B, nt),
        in_specs=[pl.BlockSpec((1, TM, A_Q), lambda b, i: (b, i, 0)),
                  pl.BlockSpec((1, nt, A_KV, TM), lambda b, i: (b, 0, 0, 0)),
                  pl.BlockSpec((1, T, A_KV_HEADS * LANES), lambda b, i: (b, 0, 0))],
        out_specs=pl.BlockSpec((1, TM, A_Q), lambda b, i: (b, i, 0)),
        out_shape=jax.ShapeDtypeStruct((B, T, A_Q), BF16),
        compiler_params=_cparams(("arbitrary", "arbitrary")),
        name="attn_a",
    )(q, kt, v)


def _attn_b_kernel(q_ref, kp_ref, kc_ref, kn_ref, vp_ref, vc_ref, vn_ref, kx_ref, vx_ref, bias_ref,
                   o_ref, kbuf, vbuf, *, nct, L):
    i = pl.program_id(1)

    def normalised(o_ext):
        return (o_ext[:, 0:HEAD_DIM] / o_ext[:, HEAD_DIM:HEAD_DIM + 1]).astype(BF16)

    @pl.when(i < nct)
    def _():
        for h in range(B_HEADS):
            sl = slice(h * HEAD_DIM, (h + 1) * HEAD_DIM)
            vl = slice(h * LANES, (h + 1) * LANES)
            s = _dot_nt(q_ref[0, :, sl], kx_ref[0, :, sl])
            m = jnp.max(s, axis=-1, keepdims=True)
            p = jnp.exp2((s - m).astype(BF16))
            o_ref[0, :, sl] = normalised(_dot(p, vx_ref[0, :, vl]))

    @pl.when(i >= nct)
    def _():
        for buf, cx, pv, cu, nx in ((kbuf, kx_ref, kp_ref, kc_ref, kn_ref), (vbuf, vx_ref, vp_ref, vc_ref, vn_ref)):
            buf[0:L] = cx[0]
            buf[L:L + TM] = pv[0]
            buf[L + TM:L + 2 * TM] = cu[0]
            buf[L + 2 * TM:L + 3 * TM] = nx[0]
        for h in range(B_HEADS):
            sl = slice(h * HEAD_DIM, (h + 1) * HEAD_DIM)
            s = _dot_nt(q_ref[0, :, sl], kbuf[:, sl])
            s_c = s[:, 0:L]
            s_w = s[:, L:] + bias_ref[0, 0, h]
            m = jnp.maximum(jnp.max(s_w, axis=-1, keepdims=True), jnp.max(s_c, axis=-1, keepdims=True))
            p = jnp.concatenate([jnp.exp2((s_c - m).astype(BF16)), jnp.exp2((s_w - m).astype(BF16))], axis=1)
            o_ref[0, :, sl] = normalised(_dot(p, vbuf[:, h * LANES:(h + 1) * LANES]))


def _attn_b(bq, bk, bv, bias_t, layer, nct, L):
    B, T, _ = bq.shape
    nt = T // TM
    assert L % LANES == 0 and nt - nct >= 2
    VW = B_HEADS * LANES
    case = lambda i: jnp.where(i <= nct, 0, jnp.where(i == nt - 1, 2, 1))
    cur = lambda b, i: (b, i, 0)
    prev = lambda b, i: (b, jnp.maximum(i - 1, nct), 0)
    nxt = lambda b, i: (b, jnp.minimum(i + 1, nt - 1), 0)
    blk = lambda f: pl.BlockSpec((1, TM, B_W), f)
    vblk = lambda f: pl.BlockSpec((1, TM, VW), f)
    ctx = lambda w: pl.BlockSpec((1, L, w), lambda b, i: (b, 0, 0))
    return pl.pallas_call(
        functools.partial(_attn_b_kernel, nct=nct, L=L),
        grid=(B, nt),
        in_specs=[blk(cur), blk(prev), blk(cur), blk(nxt), vblk(prev), vblk(cur), vblk(nxt), ctx(B_W), ctx(VW),
                  pl.BlockSpec((1, 1) + bias_t.shape[2:], lambda b, i: (layer, case(i), 0, 0, 0))],
        out_specs=blk(cur),
        out_shape=jax.ShapeDtypeStruct((B, T, B_W), BF16),
        scratch_shapes=[pltpu.VMEM((L + 3 * TM, B_W), BF16), pltpu.VMEM((L + 3 * TM, VW), BF16)],
        compiler_params=_cparams(("arbitrary", "arbitrary")),
        name="attn_b",
    )(bq, bk, bk, bk, bv, bv, bv, bk, bv, bias_t)


def _merge_kernel(*refs, nct, nt, L, S, with_router):
    if with_router:
        (ya_ref, yb_ref, cup_ref, cu_ref, cun_ref, x_ref, mod_ref, g1_ref, wg_ref, wpa_ref, wpb_ref, wpc_ref,
         pbd_ref, psc_ref, wo_ref, g2_ref, rw_ref, rb_ref,
         xo_ref, h_ref, cmb_ref, sel_ref, cnt_ref, e_scr) = refs
    else:
        (ya_ref, yb_ref, cup_ref, cu_ref, cun_ref, x_ref, mod_ref, g1_ref, wg_ref, wpa_ref, wpb_ref, wpc_ref,
         pbd_ref, psc_ref, wo_ref, g2_ref, xo_ref, h_ref, e_scr) = refs
    i = pl.program_id(1)
    D = D_MODEL
    m = mod_ref[0]
    x_in = x_ref[0]

    at_start = (i == 0) | (i == nct)
    at_end = (i == nct - 1) | (i == nt - 1)
    u = cu_ref[0]
    e_scr[0:POOL_HALO] = jnp.where(at_start, 0.0, cup_ref[0])
    e_scr[POOL_HALO:POOL_HALO + TM] = u
    e_scr[POOL_HALO + TM:] = jnp.where(at_end, 0.0, cun_ref[0])
    sh = lambda k: e_scr[POOL_HALO + k:POOL_HALO + k + TM]
    grp = lax.broadcasted_iota(jnp.int32, (1, C_WIDTH), 1) // C_GROUP_DIM
    t_loc = lax.broadcasted_iota(jnp.int32, (TM, 1), 0)
    t_seq = jnp.where(i < nct, i * TM, (i - nct) * TM) + t_loc
    n_seq = jnp.where(i < nct, L, S)
    wsum, running, lo, hi = None, None, 0, 0
    lo_v = jnp.zeros((1, C_WIDTH), jnp.int32)
    hi_v = jnp.zeros((1, C_WIDTH), jnp.int32)
    for g, w in enumerate(POOL_WINDOWS):
        for k in list(range(-(w // 2), lo)) + list(range(hi, w - w // 2)):
            running = sh(k) if running is None else running + sh(k)
        lo, hi = -(w // 2), w - w // 2
        wsum = running if wsum is None else jnp.where(grp == g, running, wsum)
        lo_v = jnp.where(grp == g, lo, lo_v)
        hi_v = jnp.where(grp == g, hi, hi_v)
    cnt = jnp.minimum(t_seq + hi_v, n_seq) - jnp.maximum(t_seq + lo_v, 0)
    dlt = wsum / cnt.astype(F32) - u
    yc = _dot(dlt.astype(BF16), pbd_ref[0]) * psc_ref[...]

    h1 = _rms_mod(x_in, g1_ref[...], m[0:1], m[1:2]).astype(BF16)
    branches = (ya_ref[0], yb_ref[0], yc.astype(BF16))
    weights = (wpa_ref, wpb_ref, wpc_ref)
    mrg = None
    for k in range(3):
        gate = jax.nn.sigmoid(_dot(h1, wg_ref[0, :, k * D:(k + 1) * D]))
        term = gate * _dot(branches[k], weights[k][0])
        mrg = term if mrg is None else mrg + term
    x = x_in + m[2:3] * _dot(mrg.astype(BF16), wo_ref[0])
    xo_ref[0] = x
    h = _rms_mod(x, g2_ref[...], m[3:4], m[4:5])
    h_ref[0] = h.astype(BF16)

    if with_router:
        lane = lax.broadcasted_iota(jnp.int32, (1, LANES), 1).astype(F32)
        h_hi, h_lo = _split(h)
        d_hi = _dot(h_hi, rw_ref[...])
        d_lo = _dot(h_lo, rw_ref[...])
        lg = d_hi[:, :LANES] + (d_hi[:, LANES:] + d_lo[:, :LANES]) + rb_ref[...]
        lg = jnp.where(lane < N_EXPERTS, lg, -jnp.inf)
        m1 = jnp.max(lg, axis=-1, keepdims=True)
        i1 = jnp.min(jnp.where(lg == m1, lane, float(LANES)), axis=-1, keepdims=True)
        mask1 = lane == i1
        lg2 = jnp.where(mask1, -jnp.inf, lg)
        m2 = jnp.max(lg2, axis=-1, keepdims=True)
        i2 = jnp.min(jnp.where(lg2 == m2, lane, float(LANES)), axis=-1, keepdims=True)
        mask2 = lane == i2
        e2 = jnp.exp(m2 - m1)
        den = 1.0 + e2
        cmb_ref[0] = jnp.where(mask1, 1.0 / den, 0.0) + jnp.where(mask2, e2 / den, 0.0)
        sel = jnp.where(mask1 | mask2, 1.0, 0.0)
        sel_ref[0] = sel
        tri = lax.broadcasted_iota(jnp.int32, (TM, TM), 0) >= lax.broadcasted_iota(jnp.int32, (TM, TM), 1)
        cnt_ref[0] = _dot(jnp.where(tri, 1.0, 0.0).astype(BF16), sel.astype(BF16))


def _merge(ya, yb, cu, xs, modr, g1, w_gate, wpa, wpb, wpc, pbd, psc, wo, g2, layer, router, nct, L):
    B, T, D = xs.shape
    nt = T // TM
    S = T - L
    hb = TM // POOL_HALO
    tok = lambda w: pl.BlockSpec((1, TM, w), lambda b, i: (b, i, 0))
    in_specs = [tok(A_Q), tok(B_W),
                pl.BlockSpec((1, POOL_HALO, C_WIDTH), lambda b, i: (b, jnp.maximum(i * hb - 1, 0), 0)),
                tok(C_WIDTH),
                pl.BlockSpec((1, POOL_HALO, C_WIDTH), lambda b, i: (b, jnp.minimum((i + 1) * hb, nt * hb - 1), 0)),
                tok(D),
                pl.BlockSpec((1, 6, D), lambda b, i: (jnp.where(i < nct, B, b), 0, 0)),
                _const_spec((1, D)),
                _layer_spec((D, 3 * D), layer),
                _layer_spec((A_Q, D), layer), _layer_spec((B_W, D), layer), _layer_spec((C_WIDTH, D), layer),
                _layer_spec((C_WIDTH, C_WIDTH), layer), _const_spec((1, C_WIDTH)), _layer_spec((D, D), layer),
                _const_spec((1, D))]
    args = [ya, yb, cu, cu, cu, xs, modr, g1, w_gate, wpa, wpb, wpc, pbd, psc, wo, g2]
    out_specs = [tok(D), tok(D)]
    out_shape = [jax.ShapeDtypeStruct((B, T, D), F32), jax.ShapeDtypeStruct((B, T, D), BF16)]
    if router is not None:
        in_specs += [_const_spec((D, 2 * LANES)), _const_spec((1, LANES))]
        args += list(router)
        out_specs += [tok(LANES)] * 3
        out_shape += [jax.ShapeDtypeStruct((B, T, LANES), F32)] * 3
    return pl.pallas_call(
        functools.partial(_merge_kernel, nct=nct, nt=nt, L=L, S=S, with_router=router is not None),
        grid=(B, nt),
        in_specs=in_specs,
        out_specs=tuple(out_specs),
        out_shape=tuple(out_shape),
        scratch_shapes=[pltpu.VMEM((TM + 2 * POOL_HALO, C_WIDTH), F32)],
        compiler_params=_cparams(("arbitrary", "arbitrary")),
        name="merge_router" if router is not None else "merge",
    )(*args)


def _swiglu_acc(h, w1_ref, w3_ref, w2_ref, lead):
    d_ff = w1_ref.shape[-1]
    acc = None
    for f0 in range(0, d_ff, FC):
        f1 = min(f0 + FC, d_ff)
        a = _dot(h, w1_ref[lead + (slice(None), slice(f0, f1))])
        b = _dot(h, w3_ref[lead + (slice(None), slice(f0, f1))])
        t = (a * jax.nn.sigmoid(a) * b).astype(BF16)
        part = _dot(t, w2_ref[lead + (slice(f0, f1), slice(None))])
        acc = part if acc is None else acc + part
    return acc


def _ffn_kernel(h_ref, x_ref, g_ref, w1_ref, w3_ref, w2_ref, o_ref):
    f = _swiglu_acc(h_ref[0], w1_ref, w3_ref, w2_ref, (0,))
    o_ref[0] = x_ref[0] + g_ref[0] * f


def _ffn_dense(h2, xs, gate2, w1, w3, w2, idx, nct):
    B, T, D = xs.shape
    F = w1.shape[-1]
    nt = T // TM
    tok = lambda: pl.BlockSpec((1, TM, D), lambda b, i: (b, i, 0))
    return pl.pallas_call(
        _ffn_kernel,
        grid=(B, nt),
        in_specs=[tok(), tok(),
                  pl.BlockSpec((1, 1, D), lambda b, i: (jnp.where(i < nct, B, b), 0, 0)),
                  _layer_spec((D, F), idx), _layer_spec((D, F), idx), _layer_spec((F, D), idx)],
        out_specs=tok(),
        out_shape=jax.ShapeDtypeStruct((B, T, D), F32),
        compiler_params=_cparams(("arbitrary", "arbitrary")),
        name="ffn_dense",
    )(h2, xs, gate2, w1, w3, w2)


def _gather_kernel(wj_ref, ws_ref, wf_ref, wl_ref, wv_ref, te_ref, pos_ref, cmb_ref, h_ref,
                   o_ref, pw_ref, acc_ref, accw_ref):
    w = pl.program_id(0)

    @pl.when(wf_ref[w] == 1)
    def _():
        acc_ref[...] = jnp.zeros_like(acc_ref)
        accw_ref[...] = jnp.zeros_like(accw_ref)

    @pl.when(wv_ref[w] == 1)
    def _():
        row = lax.broadcasted_iota(jnp.int32, (TX, 1), 0) + wj_ref[w] * TX
        hit = pos_ref[0, 0] == row
        acc_ref[...] += _dot(jnp.where(hit, 1.0, 0.0).astype(BF16), h_ref[...])
        accw_ref[...] += jnp.sum(jnp.where(hit, cmb_ref[0, 0], 0.0), axis=-1, keepdims=True)

    @pl.when(wl_ref[w] == 1)
    def _():
        o_ref[...] = acc_ref[...].astype(BF16)
        pw_ref[...] = accw_ref[...]


def _moe_gather(work, tile_expert, pos_t, cmb_t, h2f, n_tiles):
    wj, ws, wf, wl, wv = work
    n, D = h2f.shape
    row_spec = pl.BlockSpec((1, 1, 1, TX), lambda w, wj, ws, wf, wl, wv, te: (te[wj[w]], ws[w], 0, 0))
    return pl.pallas_call(
        _gather_kernel,
        grid_spec=pltpu.PrefetchScalarGridSpec(
            num_scalar_prefetch=6,
            grid=(wj.shape[0],),
            in_specs=[row_spec, row_spec,
                      pl.BlockSpec((TX, D), lambda w, wj, ws, wf, wl, wv, te: (ws[w], 0))],
            out_specs=(pl.BlockSpec((TX, D), lambda w, wj, ws, wf, wl, wv, te: (wj[w], 0)),
                       pl.BlockSpec((TX, 1), lambda w, wj, ws, wf, wl, wv, te: (wj[w], 0))),
            scratch_shapes=[pltpu.VMEM((TX, D), F32), pltpu.VMEM((TX, 1), F32)]),
        out_shape=(jax.ShapeDtypeStruct((n_tiles * TX, D), BF16),
                   jax.ShapeDtypeStruct((n_tiles * TX, 1), F32)),
        compiler_params=_cparams(("arbitrary",)),
        name="moe_gather",
    )(wj, ws, wf, wl, wv, tile_expert, pos_t, cmb_t, h2f)


def _gffn_kernel(te_ref, tv_ref, x_ref, pw_ref, w1_ref, w3_ref, w2_ref, o_ref):
    j = pl.program_id(0)

    @pl.when(tv_ref[j] == 1)
    def _():
        f = _swiglu_acc(x_ref[...], w1_ref, w3_ref, w2_ref, (0, 0))
        o_ref[...] = (f * pw_ref[...]).astype(BF16)

    @pl.when(tv_ref[j] == 0)
    def _():
        o_ref[...] = jnp.zeros_like(o_ref)


def _moe_gffn(tile_expert, tile_valid, xg, pw, w1, w3, w2, idx):
    P, D = xg.shape
    F = w1.shape[-1]
    wspec = lambda shape: pl.BlockSpec((1, 1) + shape, lambda j, te, tv: (idx, te[j], 0, 0))
    return pl.pallas_call(
        _gffn_kernel,
        grid_spec=pltpu.PrefetchScalarGridSpec(
            num_scalar_prefetch=2,
            grid=(P // TX,),
            in_specs=[pl.BlockSpec((TX, D), lambda j, te, tv: (j, 0)),
                      pl.BlockSpec((TX, 1), lambda j, te, tv: (j, 0)),
                      wspec((D, F)), wspec((D, F)), wspec((F, D))],
            out_specs=pl.BlockSpec((TX, D), lambda j, te, tv: (j, 0))),
        out_shape=jax.ShapeDtypeStruct((P, D), BF16),
        compiler_params=_cparams(("arbitrary",)),
        name="moe_gffn",
    )(tile_expert, tile_valid, xg, pw, w1, w3, w2)


def _combine_kernel(wt_ref, ws_ref, wf_ref, wl_ref, wv_ref, pos_ref, y_ref, x_ref, g_ref, o_ref, acc_ref,
                    *, n_batch, T, L):
    w = pl.program_id(0)

    @pl.when(wf_ref[w] == 1)
    def _():
        acc_ref[...] = jnp.zeros_like(acc_ref)

    @pl.when(wv_ref[w] == 1)
    def _():
        col = lax.broadcasted_iota(jnp.int32, (1, TX), 1) + ws_ref[w] * TX
        pos = pos_ref[...]
        hit = (pos[:, 0:1] == col) | (pos[:, 1:2] == col)
        acc_ref[...] += _dot(jnp.where(hit, 1.0, 0.0).astype(BF16), y_ref[...])

    @pl.when(wl_ref[w] == 1)
    def _():
        n = lax.broadcasted_iota(jnp.int32, (TX, 1), 0) + wt_ref[w] * TX
        b = jnp.zeros((TX, 1), jnp.int32)
        for k in range(1, n_batch):
            b = b + (n >= k * T).astype(jnp.int32)
        rowid = jnp.where(n - b * T < L, n_batch, b)
        gate = jnp.zeros((TX, D_MODEL), F32)
        for r in range(n_batch + 1):
            gate = jnp.where(rowid == r, g_ref[r:r + 1, :], gate)
        o_ref[...] = x_ref[...] + gate * acc_ref[...]


def _moe_combine(work, pos2, yw, xf, gate2, n_batch, T, L):
    wt, ws, wf, wl, wv = work
    n, D = xf.shape
    return pl.pallas_call(
        functools.partial(_combine_kernel, n_batch=n_batch, T=T, L=L),
        grid_spec=pltpu.PrefetchScalarGridSpec(
            num_scalar_prefetch=5,
            grid=(wt.shape[0],),
            in_specs=[pl.BlockSpec((TX, 2), lambda w, wt, ws, wf, wl, wv: (wt[w], 0)),
                      pl.BlockSpec((TX, D), lambda w, wt, ws, wf, wl, wv: (ws[w], 0)),
                      pl.BlockSpec((TX, D), lambda w, wt, ws, wf, wl, wv: (wt[w], 0)),
                      pl.BlockSpec((SUBLANES, D), lambda w, wt, ws, wf, wl, wv: (0, 0))],
            out_specs=pl.BlockSpec((TX, D), lambda w, wt, ws, wf, wl, wv: (wt[w], 0)),
            scratch_shapes=[pltpu.VMEM((TX, D), F32)]),
        out_shape=jax.ShapeDtypeStruct((n, D), F32),
        compiler_params=_cparams(("arbitrary",)),
        name="moe_combine",
    )(wt, ws, wf, wl, wv, pos2, yw, xf, gate2)


def _count_below(sorted_vals, x):
    return jnp.sum((sorted_vals[None, :] < x[:, None]).astype(jnp.int32), axis=1)


def _with_flags(g, it, compute, n_work, total):
    live = jnp.arange(n_work, dtype=jnp.int32) < total
    g_prev = jnp.concatenate([jnp.full((1,), -1, jnp.int32), g[:-1]])
    g_next = jnp.concatenate([g[1:], jnp.full((1,), -1, jnp.int32)])
    live_next = jnp.concatenate([live[1:], jnp.zeros((1,), bool)])
    first = live & (g != g_prev)
    last = live & ((g != g_next) | ~live_next)
    i32 = lambda a: a.astype(jnp.int32)
    return i32(g), i32(it), i32(first), i32(last), i32(live & compute)


def _moe_plan(sel, cnt_in_tile, n_tok):
    E = N_EXPERTS
    sel = sel.astype(jnp.int32)
    tile_tot = cnt_in_tile[TM - 1::TM]
    before = jnp.cumsum(tile_tot, axis=0) - tile_tot
    cnt = jnp.sum(tile_tot, axis=0)
    rank = (cnt_in_tile.reshape(-1, TM, E) + before[:, None, :]).reshape(n_tok, E) - 1
    gsz = ((cnt + TX - 1) // TX) * TX
    gend = jnp.cumsum(gsz)
    goff = gend - gsz
    n_tiles = (2 * n_tok + E * (TX - 1) + TX - 1) // TX
    P = n_tiles * TX
    pos = jnp.where(sel == 1, goff[None, :] + rank, -1)
    tile_start = jnp.arange(n_tiles, dtype=jnp.int32) * TX
    tile_valid = tile_start < gend[-1]
    tile_expert = jnp.minimum(_count_below(gend, tile_start + 1), E - 1)
    pmax_tok = jnp.max(pos, axis=1)
    psec_tok = jnp.max(jnp.where(pos == pmax_tok[:, None], -1, pos), axis=1)
    pos2 = jnp.stack([psec_tok, pmax_tok], axis=1)

    n_src_tiles = n_tok // TX
    cin = rank[TX - 1::TX] + 1
    ra = tile_start - goff[tile_expert]
    rb = ra + jnp.clip(cnt[tile_expert] - ra, 0, TX) - 1
    cin_t = cin.T[tile_expert]
    lo = jnp.sum((cin_t <= ra[:, None]).astype(jnp.int32), axis=1)
    hi = jnp.sum((cin_t <= rb[:, None]).astype(jnp.int32), axis=1)
    span = jnp.where(tile_valid, hi - lo + 1, 1)
    n_gw = E * n_src_tiles + n_tiles
    cs = jnp.cumsum(span)
    total = cs[-1]
    w = jnp.minimum(jnp.arange(n_gw, dtype=jnp.int32), total - 1)
    jw = _count_below(cs, w + 1)
    sw = jnp.where(tile_valid[jw], lo[jw] + (w - (cs[jw] - span[jw])), 0)
    gwork = _with_flags(jw, sw, tile_valid[jw], n_gw, total)

    pt = pos.reshape(n_src_tiles, TX, E)
    pmax = jnp.max(pt, axis=1)
    pmin = jnp.min(jnp.where(pt >= 0, pt, P), axis=1)
    ta = pmin // TX
    tb = pmax // TX
    cand_item = jnp.stack([ta, tb], axis=-1).reshape(-1)
    cand_valid = jnp.stack([pmax >= 0, (pmax >= 0) & (tb != ta)], axis=-1).reshape(-1).astype(jnp.int32)
    cand_group = jnp.repeat(jnp.arange(n_src_tiles, dtype=jnp.int32), 2 * E)
    n_cw = E * n_src_tiles + n_tiles
    ccs = jnp.cumsum(cand_valid)
    ctotal = ccs[-1]
    cw = jnp.minimum(jnp.arange(n_cw, dtype=jnp.int32), ctotal - 1)
    cidx = _count_below(ccs, cw + 1)
    cwork = _with_flags(cand_group[cidx], cand_item[cidx], jnp.ones((n_cw,), bool), n_cw, ctotal)
    return pos, pos2, tile_expert, tile_valid.astype(jnp.int32), gwork, cwork, n_tiles


def _moe(h2, xs1, cmb, sel, cnt_in_tile, gate2, w1, w3, w2, idx, L):
    B, T, D = xs1.shape
    n_tok = B * T
    E = N_EXPERTS
    assert n_tok % TX == 0
    selm = sel.reshape(n_tok, LANES)[:, :E] > 0.5
    cmbm = cmb.reshape(n_tok, LANES)[:, :E]
    cntm = cnt_in_tile.reshape(n_tok, LANES)[:, :E].astype(jnp.int32)
    pos, pos2, tile_expert, tile_valid, gwork, cwork, n_tiles = _moe_plan(selm, cntm, n_tok)
    pos_t = pos.T.reshape(E, n_tok // TX, 1, TX)
    cmb_t = cmbm.T.reshape(E, n_tok // TX, 1, TX)

    xg, pw = _moe_gather(gwork, tile_expert, pos_t, cmb_t, h2.reshape(n_tok, D), n_tiles)
    yw = _moe_gffn(tile_expert, tile_valid, xg, pw, w1, w3, w2, idx)
    out = _moe_combine(cwork, pos2, yw, xs1.reshape(n_tok, D), gate2, B, T, L)
    return out.reshape(B, T, D)


def _final_kernel(x_ref, g_ref, o_ref):
    x = x_ref[0]
    o_ref[0] = x * lax.rsqrt(jnp.mean(x * x, axis=-1, keepdims=True) + EPS) * g_ref[...]


def _final_norm(xs, g, nct, S):
    B, T, D = xs.shape
    return pl.pallas_call(
        _final_kernel,
        grid=(B, S // TM),
        in_specs=[pl.BlockSpec((1, TM, D), lambda b, i: (b, i + nct, 0)), _const_spec((1, D))],
        out_specs=pl.BlockSpec((1, TM, D), lambda b, i: (b, i, 0)),
        out_shape=jax.ShapeDtypeStruct((B, S, D), F32),
        compiler_params=_cparams(("arbitrary", "arbitrary")),
        name="final_norm",
    )(xs, g)


def _rope_tables(L, S):
    t = np.arange(S)
    pos = np.stack([t // GRID_W, t % GRID_W], axis=-1).astype(np.float32)
    inv_freq = (ROPE_THETA ** (-np.arange(0, AXIS_ROPE_DIM, 2, dtype=np.float32) / AXIS_ROPE_DIM)).astype(np.float32)
    ang = pos[:, :, None] * inv_freq[None, None, :]
    cos, sin = np.cos(ang), np.sin(ang)
    cos64 = np.concatenate([cos[:, 0], cos[:, 0], cos[:, 1], cos[:, 1]], axis=-1)
    sin64 = np.concatenate([-sin[:, 0], sin[:, 0], -sin[:, 1], sin[:, 1]], axis=-1)
    cos_t = np.concatenate([np.ones((L, HEAD_DIM), np.float32), cos64], axis=0)
    sin_t = np.concatenate([np.zeros((L, HEAD_DIM), np.float32), sin64], axis=0)
    rep = LANES // HEAD_DIM
    return jnp.asarray(np.tile(cos_t, (1, rep)), F32), jnp.asarray(np.tile(sin_t, (1, rep)), F32)


def _na_bias_table(rel_bias):
    rpt = TM // GRID_W
    half = NA_WIN_ROWS // 2
    assert half <= rpt and rpt - 1 - half + NA_WIN_ROWS <= 2 * rpt and rpt + half + 1 >= NA_WIN_ROWS
    col = np.arange(GRID_W)
    col_start = np.clip(col - NA_WIN_COLS // 2, 0, GRID_W - NA_WIN_COLS)
    kc = np.arange(GRID_W)
    inside = (kc[None, :] >= col_start[:, None]) & (kc[None, :] < col_start[:, None] + NA_WIN_COLS)
    relp = jnp.pad(rel_bias, ((0, 0), (0, 0), (0, 0), (GRID_W, GRID_W)))
    off = GRID_W + NA_WIN_COLS - 1
    cmat = jnp.stack([relp[..., off - qc:off - qc + GRID_W] for qc in range(GRID_W)], axis=3)
    cmat = jnp.where(jnp.asarray(inside), cmat * LOG2E, MASK_VALUE)
    masked = jnp.full(cmat.shape[:2] + (GRID_W, GRID_W), MASK_VALUE, F32)

    cases = []
    for case in range(3):
        rows = []
        for a in range(rpt):
            first = (rpt + max(a - half, 0),
                     rpt + a - half,
                     min(rpt + a - half, 2 * rpt - NA_WIN_ROWS))[case]
            blocks = [cmat[:, :, t - (rpt + a) + NA_WIN_ROWS - 1] if first <= t < first + NA_WIN_ROWS else masked
                      for t in range(3 * rpt)]
            rows.append(jnp.concatenate(blocks, axis=-1))
        cases.append(jnp.concatenate(rows, axis=-2))
    return jnp.stack(cases, axis=1)


def _block_diag_ones(n, blk):
    idx = np.arange(n) // blk
    return jnp.asarray((idx[:, None] == idx[None, :]).astype(np.float32), BF16)


def _pool_block_diag(pool_w):
    depth, g, c, d = pool_w.shape
    eye = jnp.asarray(np.eye(g, dtype=np.float32))
    out = pool_w[:, :, :, None, :] * eye[None, :, None, :, None]
    return out.reshape(depth, g * c, g * d).astype(BF16)


def kernel(x, c, ctx, c_ctx, w_mod, b_mod, norm1_g, norm2_g, w_in, q_norm_g, k_norm_g, na_rel_bias, pool_w,
           pool_scale, w_branch_a, w_branch_b, w_branch_c, w_out, ffn_w1, ffn_w3, ffn_w2, router_w, router_b,
           moe_w1, moe_w3, moe_w2, final_g):
    B, S, D = x.shape
    L = ctx.shape[1]
    T = L + S
    depth = w_mod.shape[0]
    assert D == D_MODEL and L % TM == 0 and S % TM == 0 and TM % GRID_W == 0 and B + 1 <= SUBLANES
    assert S // GRID_W >= NA_WIN_ROWS and len(POOL_WINDOWS) == C_GROUPS
    assert list(POOL_WINDOWS) == sorted(POOL_WINDOWS) and POOL_WINDOWS[-1] - POOL_WINDOWS[-1] // 2 <= POOL_HALO
    nct = L // TM

    cvec = jnp.zeros((SUBLANES, D), F32).at[:B].set(c).at[B].set(c_ctx)
    mod = _mod_vectors(cvec, w_mod, b_mod)
    cos_t, sin_t = _rope_tables(L, S)
    bd = _block_diag_ones(QK_W, HEAD_DIM)
    na_bias = _na_bias_table(na_rel_bias)
    bf = lambda a: a.astype(BF16)
    w_qkv, w_gate = bf(w_in[:, :, :OFF_G]), bf(w_in[:, :, OFF_G:])
    wpa, wpb, wpc, wo, pbd = bf(w_branch_a), bf(w_branch_b), bf(w_branch_c), bf(w_out), _pool_block_diag(pool_w)
    fw1, fw3, fw2 = bf(ffn_w1), bf(ffn_w3), bf(ffn_w2)
    mw1, mw3, mw2 = bf(moe_w1), bf(moe_w3), bf(moe_w2)

    xs = jnp.concatenate([ctx, x], axis=1)
    for layer in range(depth):
        modr = mod[layer].reshape(SUBLANES, 6, D)
        g1 = norm1_g[layer].reshape(1, D)
        gqk = jnp.concatenate([jnp.tile(q_norm_g[layer], A_HEADS), jnp.tile(k_norm_g[layer], A_KV_HEADS)])
        q, kt, v, bq, bk, bv, cu = _inproj(xs, modr, g1, w_qkv, layer, bd, gqk.reshape(1, QK_W), cos_t, sin_t, nct)
        ya = _attn_a(q, kt, v, nct)
        yb = _attn_b(bq, bk, bv, na_bias, layer, nct, L)
        is_moe = layer % 2 == 1
        i = layer // 2
        router = None
        if is_moe:
            rw = jnp.zeros((D, LANES), F32).at[:, :N_EXPERTS].set(router_w[i])
            rw_hi = rw.astype(BF16)
            rw_lo = (rw - rw_hi.astype(F32)).astype(BF16)
            rb = jnp.zeros((1, LANES), F32).at[0, :N_EXPERTS].set(router_b[i])
            router = (jnp.concatenate([rw_hi, rw_lo], axis=1), rb)
        outs = _merge(ya, yb, cu, xs, modr, g1, w_gate, wpa, wpb, wpc, pbd, pool_scale[layer].reshape(1, C_WIDTH),
                      wo, norm2_g[layer].reshape(1, D), layer, router, nct, L)
        if is_moe:
            xs1, h2, cmb, sel, cnt_in_tile = outs
            xs = _moe(h2, xs1, cmb, sel, cnt_in_tile, modr[:, 5, :], mw1, mw3, mw2, i, L)
        else:
            xs1, h2 = outs
            xs = _ffn_dense(h2, xs1, modr[:, 5:6, :], fw1, fw3, fw2, i, nct)
    return _final_norm(xs, final_g.reshape(1, D), nct, S)
```

```python
import functools

import numpy as np
import jax
import jax.numpy as jnp
from jax import lax
from jax.experimental import pallas as pl
from jax.experimental.pallas import tpu as pltpu

F32 = jnp.float32
BF16 = jnp.bfloat16

D_MODEL = 1024
DEPTH = 4
GRID_W = 64
HEAD_DIM = 64
A_HEADS = 8
A_KV_HEADS = 2
A_GROUP = A_HEADS // A_KV_HEADS
B_HEADS = 4
C_GROUPS = 4
C_GROUP_DIM = 64
C_WIDTH = C_GROUPS * C_GROUP_DIM
POOL_WINDOWS = (2, 4, 8, 16)
NA_WIN_ROWS = 8
NA_WIN_COLS = 16
ROPE_THETA = 10000.0
AXIS_ROPE_DIM = HEAD_DIM // 2
N_EXPERTS = 8
EPS = 1e-6

A_Q = A_HEADS * HEAD_DIM
A_KV = A_KV_HEADS * HEAD_DIM
B_W = B_HEADS * HEAD_DIM
QK_W = A_Q + A_KV
OFF_AV = QK_W
OFF_BQ = OFF_AV + A_KV
OFF_BK = OFF_BQ + B_W
OFF_BV = OFF_BK + B_W
OFF_CU = OFF_BV + B_W
OFF_G = OFF_CU + C_WIDTH

LANES = 128
SUBLANES = 8
TM = 256
TX = 512
FC = 512
A_KBLK = 8
LOG2E = 1.4426950408889634
POOL_HALO = 8
MASK_VALUE = -1e30
VMEM_LIMIT = 56 * 1024 * 1024


def _cparams(sem):
    return pltpu.CompilerParams(dimension_semantics=sem, vmem_limit_bytes=VMEM_LIMIT)


def _dot(a, b):
    return jnp.dot(a, b, preferred_element_type=F32)


def _dot_nt(a, b):
    return lax.dot_general(a, b, (((1,), (1,)), ((), ())), preferred_element_type=F32)


def _split(a):
    hi = a.astype(BF16)
    lo = (a - hi.astype(F32)).astype(BF16)
    return hi, lo


def _dot3(a, b_hi, b_lo):
    a_hi, a_lo = _split(a)
    return _dot(a_hi, b_hi) + (_dot(a_lo, b_hi) + _dot(a_hi, b_lo))


def _const_spec(shape):
    n = len(shape)
    return pl.BlockSpec(shape, lambda *_: (0,) * n, pipeline_mode=pl.Buffered(1))


def _layer_spec(shape, layer):
    n = len(shape)
    return pl.BlockSpec((1,) + tuple(shape), lambda *_: (layer,) + (0,) * n, pipeline_mode=pl.Buffered(1))


def _rms_mod(x, g, shift, scale):
    y = x * lax.rsqrt(jnp.mean(x * x, axis=-1, keepdims=True) + EPS) * g
    return y * (1.0 + scale) + shift


def _mod_kernel(c_ref, w_ref, b_ref, o_ref):
    c = c_ref[...]
    s = c * jax.nn.sigmoid(c)
    w_hi, w_lo = _split(w_ref[0])
    o_ref[0] = _dot3(s, w_hi, w_lo) + b_ref[0]


def _mod_vectors(cvec, w_mod, b_mod):
    depth, d, n = w_mod.shape
    tn = 1536
    return pl.pallas_call(
        _mod_kernel,
        grid=(depth, n // tn),
        in_specs=[pl.BlockSpec((SUBLANES, d), lambda l, j: (0, 0)),
                  pl.BlockSpec((1, d, tn), lambda l, j: (l, 0, j)),
                  pl.BlockSpec((1, 1, tn), lambda l, j: (l, 0, j))],
        out_specs=pl.BlockSpec((1, SUBLANES, tn), lambda l, j: (l, 0, j)),
        out_shape=jax.ShapeDtypeStruct((depth, SUBLANES, n), F32),
        compiler_params=_cparams(("arbitrary", "arbitrary")),
        name="mod_vectors",
    )(cvec, w_mod, b_mod.reshape(depth, 1, n))


def _rotate_half(x):
    half = AXIS_ROPE_DIM // 2
    lane = lax.broadcasted_iota(jnp.int32, (1, LANES), 1)
    first = (lane % AXIS_ROPE_DIM) < half
    return jnp.where(first, pltpu.roll(x, LANES - half, 1), pltpu.roll(x, half, 1))


def _inproj_kernel(x_ref, mod_ref, g_ref, w_ref, bd_ref, gqk_ref, cos_ref, sin_ref,
                   q_ref, kt_ref, v_ref, bq_ref, bk_ref, bv_ref, cu_ref):
    m = mod_ref[0]
    h = _rms_mod(x_ref[0], g_ref[...], m[0:1], m[1:2])
    px = _dot(h.astype(BF16), w_ref[0])

    qk = px[:, :QK_W]
    ss = _dot((qk * qk).astype(BF16), bd_ref[...])
    qn = qk * lax.rsqrt(ss * (1.0 / HEAD_DIM) + EPS) * gqk_ref[...]
    cos = cos_ref[...]
    sin = sin_ref[...]
    chunks = []
    for j in range(QK_W // LANES):
        c = qn[:, j * LANES:(j + 1) * LANES]
        chunks.append(c * cos + _rotate_half(c) * sin)
    scale = HEAD_DIM ** -0.5
    for j in range(A_Q // LANES):
        q_ref[0, :, j * LANES:(j + 1) * LANES] = (chunks[j] * (scale * LOG2E)).astype(BF16)
    kt_ref[0, 0] = chunks[A_Q // LANES].T.astype(BF16)

    ones = jnp.ones((TM, LANES - HEAD_DIM), BF16)
    for j in range(A_KV_HEADS):
        vj = px[:, OFF_AV + j * HEAD_DIM:OFF_AV + (j + 1) * HEAD_DIM].astype(BF16)
        v_ref[0, :, j * LANES:(j + 1) * LANES] = jnp.concatenate([vj, ones], axis=1)
    bq_ref[0] = (px[:, OFF_BQ:OFF_BK] * (scale * LOG2E)).astype(BF16)
    bk_ref[0] = px[:, OFF_BK:OFF_BV].astype(BF16)
    for h in range(B_HEADS):
        vh = px[:, OFF_BV + h * HEAD_DIM:OFF_BV + (h + 1) * HEAD_DIM].astype(BF16)
        bv_ref[0, :, h * LANES:(h + 1) * LANES] = jnp.concatenate([vh, ones], axis=1)
    cu_ref[0] = px[:, OFF_CU:OFF_G]


def _inproj(xs, modr, g1, w_qkv, layer, bd, gqk, cos_t, sin_t, nct):
    B, T, D = xs.shape
    nt = T // TM
    tok = lambda w: pl.BlockSpec((1, TM, w), lambda b, i: (b, i, 0))
    out_shape = (
        jax.ShapeDtypeStruct((B, T, A_Q), BF16),
        jax.ShapeDtypeStruct((B, nt, A_KV, TM), BF16),
        jax.ShapeDtypeStruct((B, T, A_KV_HEADS * LANES), BF16),
        jax.ShapeDtypeStruct((B, T, B_W), BF16),
        jax.ShapeDtypeStruct((B, T, B_W), BF16),
        jax.ShapeDtypeStruct((B, T, B_HEADS * LANES), BF16),
        jax.ShapeDtypeStruct((B, T, C_WIDTH), F32),
    )
    return pl.pallas_call(
        _inproj_kernel,
        grid=(B, nt),
        in_specs=[tok(D),
                  pl.BlockSpec((1, 6, D), lambda b, i: (jnp.where(i < nct, B, b), 0, 0)),
                  _const_spec((1, D)),
                  _layer_spec((D, OFF_G), layer),
                  _const_spec((QK_W, QK_W)),
                  _const_spec((1, QK_W)),
                  pl.BlockSpec((TM, LANES), lambda b, i: (i, 0)),
                  pl.BlockSpec((TM, LANES), lambda b, i: (i, 0))],
        out_specs=(tok(A_Q),
                   pl.BlockSpec((1, 1, A_KV, TM), lambda b, i: (b, i, 0, 0)),
                   tok(A_KV_HEADS * LANES), tok(B_W), tok(B_W), tok(B_HEADS * LANES), tok(C_WIDTH)),
        out_shape=out_shape,
        compiler_params=_cparams(("arbitrary", "arbitrary")),
        name="inproj",
    )(xs, modr, g1, w_qkv, bd, gqk, cos_t, sin_t)


def _attn_a_kernel(q_ref, kt_ref, v_ref, o_ref, *, nct, n_steps):
    i = pl.program_id(1)
    rows = A_GROUP * TM

    def step(carry, q4, j, blk0, nblk):
        m, acc = carry
        s_list = [_dot(q4, kt_ref[0, blk0 + c, j * HEAD_DIM:(j + 1) * HEAD_DIM, :]) for c in range(nblk)]
        smax = s_list[0]
        for s in s_list[1:]:
            smax = jnp.maximum(smax, s)
        m_new = jnp.maximum(m, jnp.max(smax, axis=-1, keepdims=True))
        alpha = jnp.exp2(m - m_new)
        p = jnp.concatenate([jnp.exp2((s - m_new).astype(BF16)) for s in s_list], axis=1)
        vv = v_ref[0, pl.ds(pl.multiple_of(blk0 * TM, TM), nblk * TM), j * LANES:(j + 1) * LANES]
        return m_new, alpha * acc + _dot(p, vv)

    def run(n_main):
        for j in range(A_KV_HEADS):
            base = j * A_GROUP * HEAD_DIM
            q4 = jnp.concatenate(
                [q_ref[0, :, base + g * HEAD_DIM: base + (g + 1) * HEAD_DIM] for g in range(A_GROUP)], axis=0)
            carry = (jnp.full((rows, 1), MASK_VALUE, F32), jnp.zeros((rows, LANES), F32))
            if n_main:
                carry = step(carry, q4, j, 0, nct + A_KBLK)
                carry = lax.fori_loop(
                    1, n_main, lambda k, cr: step(cr, q4, j, nct + k * A_KBLK, A_KBLK), carry)
            else:
                carry = step(carry, q4, j, 0, nct)
            acc = carry[1]
            o = acc[:, 0:HEAD_DIM] / acc[:, HEAD_DIM:HEAD_DIM + 1]
            for g in range(A_GROUP):
                o_ref[0, :, base + g * HEAD_DIM: base + (g + 1) * HEAD_DIM] = o[g * TM:(g + 1) * TM].astype(BF16)

    @pl.when(i < nct)
    def _():
        run(0)

    @pl.when(i >= nct)
    def _():
        run(n_steps)


def _attn_a(q, kt, v, nct):
    B, T, _ = q.shape
    nt = T // TM
    assert (nt - nct) % A_KBLK == 0
    return pl.pallas_call(
        functools.partial(_attn_a_kernel, nct=nct, n_steps=(nt - nct) // A_KBLK),
        grid=(B, nt),
        in_specs=[pl.BlockSpec((1, TM, A_Q), lambda b, i: (b, i, 0)),
                  pl.BlockSpec((1, nt, A_KV, TM), lambda b, i: (b, 0, 0, 0)),
                  pl.BlockSpec((1, T, A_KV_HEADS * LANES), lambda b, i: (b, 0, 0))],
        out_specs=pl.BlockSpec((1, TM, A_Q), lambda b, i: (b, i, 0)),
        out_shape=jax.ShapeDtypeStruct((B, T, A_Q), BF16),
        compiler_params=_cparams(("arbitrary", "arbitrary")),
        name="attn_a",
    )(q, kt, v)


def _attn_b_kernel(q_ref, kp_ref, kc_ref, kn_ref, vp_ref, vc_ref, vn_ref, kx_ref, vx_ref, bias_ref,
                   o_ref, kbuf, vbuf, *, nct, L):
    i = pl.program_id(1)

    def normalised(o_ext):
        return (o_ext[:, 0:HEAD_DIM] / o_ext[:, HEAD_DIM:HEAD_DIM + 1]).astype(BF16)

    @pl.when(i < nct)
    def _():
        for h in range(B_HEADS):
            sl = slice(h * HEAD_DIM, (h + 1) * HEAD_DIM)
            vl = slice(h * LANES, (h + 1) * LANES)
            s = _dot_nt(q_ref[0, :, sl], kx_ref[0, :, sl])
            m = jnp.max(s, axis=-1, keepdims=True)
            p = jnp.exp2((s - m).astype(BF16))
            o_ref[0, :, sl] = normalised(_dot(p, vx_ref[0, :, vl]))

    @pl.when(i >= nct)
    def _():
        for buf, cx, pv, cu, nx in ((kbuf, kx_ref, kp_ref, kc_ref, kn_ref), (vbuf, vx_ref, vp_ref, vc_ref, vn_ref)):
            buf[0:L] = cx[0]
            buf[L:L + TM] = pv[0]
            buf[L + TM:L + 2 * TM] = cu[0]
            buf[L + 2 * TM:L + 3 * TM] = nx[0]
        for h in range(B_HEADS):
            sl = slice(h * HEAD_DIM, (h + 1) * HEAD_DIM)
            s = _dot_nt(q_ref[0, :, sl], kbuf[:, sl])
            s_c = s[:, 0:L]
            s_w = s[:, L:] + bias_ref[0, 0, h]
            m = jnp.maximum(jnp.max(s_w, axis=-1, keepdims=True), jnp.max(s_c, axis=-1, keepdims=True))
            p = jnp.concatenate([jnp.exp2((s_c - m).astype(BF16)), jnp.exp2((s_w - m).astype(BF16))], axis=1)
            o_ref[0, :, sl] = normalised(_dot(p, vbuf[:, h * LANES:(h + 1) * LANES]))


def _attn_b(bq, bk, bv, bias_t, layer, nct, L):
    B, T, _ = bq.shape
    nt = T // TM
    assert L % LANES == 0 and nt - nct >= 2
    VW = B_HEADS * LANES
    case = lambda i: jnp.where(i <= nct, 0, jnp.where(i == nt - 1, 2, 1))
    cur = lambda b, i: (b, i, 0)
    prev = lambda b, i: (b, jnp.maximum(i - 1, nct), 0)
    nxt = lambda b, i: (b, jnp.minimum(i + 1, nt - 1), 0)
    blk = lambda f: pl.BlockSpec((1, TM, B_W), f)
    vblk = lambda f: pl.BlockSpec((1, TM, VW), f)
    ctx = lambda w: pl.BlockSpec((1, L, w), lambda b, i: (b, 0, 0))
    return pl.pallas_call(
        functools.partial(_attn_b_kernel, nct=nct, L=L),
        grid=(B, nt),
        in_specs=[blk(cur), blk(prev), blk(cur), blk(nxt), vblk(prev), vblk(cur), vblk(nxt), ctx(B_W), ctx(VW),
                  pl.BlockSpec((1, 1) + bias_t.shape[2:], lambda b, i: (layer, case(i), 0, 0, 0))],
        out_specs=blk(cur),
        out_shape=jax.ShapeDtypeStruct((B, T, B_W), BF16),
        scratch_shapes=[pltpu.VMEM((L + 3 * TM, B_W), BF16), pltpu.VMEM((L + 3 * TM, VW), BF16)],
        compiler_params=_cparams(("arbitrary", "arbitrary")),
        name="attn_b",
    )(bq, bk, bk, bk, bv, bv, bv, bk, bv, bias_t)


def _merge_kernel(*refs, nct, nt, L, S, with_router):
    if with_router:
        (ya_ref, yb_ref, cup_ref, cu_ref, cun_ref, x_ref, mod_ref, g1_ref, wg_ref, wpa_ref, wpb_ref, wpc_ref,
         pbd_ref, psc_ref, wo_ref, g2_ref, rw_ref, rb_ref,
         xo_ref, h_ref, cmb_ref, sel_ref, e_scr) = refs
    else:
        (ya_ref, yb_ref, cup_ref, cu_ref, cun_ref, x_ref, mod_ref, g1_ref, wg_ref, wpa_ref, wpb_ref, wpc_ref,
         pbd_ref, psc_ref, wo_ref, g2_ref, xo_ref, h_ref, e_scr) = refs
    i = pl.program_id(1)
    D = D_MODEL
    m = mod_ref[0]
    x_in = x_ref[0]

    at_start = (i == 0) | (i == nct)
    at_end = (i == nct - 1) | (i == nt - 1)
    u = cu_ref[0]
    e_scr[0:POOL_HALO] = jnp.where(at_start, 0.0, cup_ref[0])
    e_scr[POOL_HALO:POOL_HALO + TM] = u
    e_scr[POOL_HALO + TM:] = jnp.where(at_end, 0.0, cun_ref[0])
    sh = lambda k: e_scr[POOL_HALO + k:POOL_HALO + k + TM]
    grp = lax.broadcasted_iota(jnp.int32, (1, C_WIDTH), 1) // C_GROUP_DIM
    t_loc = lax.broadcasted_iota(jnp.int32, (TM, 1), 0)
    t_seq = jnp.where(i < nct, i * TM, (i - nct) * TM) + t_loc
    n_seq = jnp.where(i < nct, L, S)
    wsum, running, lo, hi = None, None, 0, 0
    lo_v = jnp.zeros((1, C_WIDTH), jnp.int32)
    hi_v = jnp.zeros((1, C_WIDTH), jnp.int32)
    for g, w in enumerate(POOL_WINDOWS):
        for k in list(range(-(w // 2), lo)) + list(range(hi, w - w // 2)):
            running = sh(k) if running is None else running + sh(k)
        lo, hi = -(w // 2), w - w // 2
        wsum = running if wsum is None else jnp.where(grp == g, running, wsum)
        lo_v = jnp.where(grp == g, lo, lo_v)
        hi_v = jnp.where(grp == g, hi, hi_v)
    cnt = jnp.minimum(t_seq + hi_v, n_seq) - jnp.maximum(t_seq + lo_v, 0)
    dlt = wsum / cnt.astype(F32) - u
    yc = _dot(dlt.astype(BF16), pbd_ref[0]) * psc_ref[...]

    h1 = _rms_mod(x_in, g1_ref[...], m[0:1], m[1:2]).astype(BF16)
    branches = (ya_ref[0], yb_ref[0], yc.astype(BF16))
    weights = (wpa_ref, wpb_ref, wpc_ref)
    mrg = None
    for k in range(3):
        gate = jax.nn.sigmoid(_dot(h1, wg_ref[0, :, k * D:(k + 1) * D]))
        term = gate * _dot(branches[k], weights[k][0])
        mrg = term if mrg is None else mrg + term
    x = x_in + m[2:3] * _dot(mrg.astype(BF16), wo_ref[0])
    xo_ref[0] = x
    h = _rms_mod(x, g2_ref[...], m[3:4], m[4:5])
    h_ref[0] = h.astype(BF16)

    if with_router:
        lane = lax.broadcasted_iota(jnp.int32, (1, LANES), 1).astype(F32)
        h_hi, h_lo = _split(h)
        d_hi = _dot(h_hi, rw_ref[...])
        d_lo = _dot(h_lo, rw_ref[...])
        lg = d_hi[:, :LANES] + (d_hi[:, LANES:] + d_lo[:, :LANES]) + rb_ref[...]
        lg = jnp.where(lane < N_EXPERTS, lg, -jnp.inf)
        m1 = jnp.max(lg, axis=-1, keepdims=True)
        i1 = jnp.min(jnp.where(lg == m1, lane, float(LANES)), axis=-1, keepdims=True)
        mask1 = lane == i1
        lg2 = jnp.where(mask1, -jnp.inf, lg)
        m2 = jnp.max(lg2, axis=-1, keepdims=True)
        i2 = jnp.min(jnp.where(lg2 == m2, lane, float(LANES)), axis=-1, keepdims=True)
        mask2 = lane == i2
        e2 = jnp.exp(m2 - m1)
        den = 1.0 + e2
        cmb_ref[0] = jnp.where(mask1, 1.0 / den, 0.0) + jnp.where(mask2, e2 / den, 0.0)
        sel_ref[0] = jnp.where(mask1 | mask2, 1.0, 0.0)


def _merge(ya, yb, cu, xs, modr, g1, w_gate, wpa, wpb, wpc, pbd, psc, wo, g2, layer, router, nct, L):
    B, T, D = xs.shape
    nt = T // TM
    S = T - L
    hb = TM // POOL_HALO
    tok = lambda w: pl.BlockSpec((1, TM, w), lambda b, i: (b, i, 0))
    in_specs = [tok(A_Q), tok(B_W),
                pl.BlockSpec((1, POOL_HALO, C_WIDTH), lambda b, i: (b, jnp.maximum(i * hb - 1, 0), 0)),
                tok(C_WIDTH),
                pl.BlockSpec((1, POOL_HALO, C_WIDTH), lambda b, i: (b, jnp.minimum((i + 1) * hb, nt * hb - 1), 0)),
                tok(D),
                pl.BlockSpec((1, 6, D), lambda b, i: (jnp.where(i < nct, B, b), 0, 0)),
                _const_spec((1, D)),
                _layer_spec((D, 3 * D), layer),
                _layer_spec((A_Q, D), layer), _layer_spec((B_W, D), layer), _layer_spec((C_WIDTH, D), layer),
                _layer_spec((C_WIDTH, C_WIDTH), layer), _const_spec((1, C_WIDTH)), _layer_spec((D, D), layer),
                _const_spec((1, D))]
    args = [ya, yb, cu, cu, cu, xs, modr, g1, w_gate, wpa, wpb, wpc, pbd, psc, wo, g2]
    out_specs = [tok(D), tok(D)]
    out_shape = [jax.ShapeDtypeStruct((B, T, D), F32), jax.ShapeDtypeStruct((B, T, D), BF16)]
    if router is not None:
        in_specs += [_const_spec((D, 2 * LANES)), _const_spec((1, LANES))]
        args += list(router)
        out_specs += [tok(LANES), tok(LANES)]
        out_shape += [jax.ShapeDtypeStruct((B, T, LANES), F32)] * 2
    return pl.pallas_call(
        functools.partial(_merge_kernel, nct=nct, nt=nt, L=L, S=S, with_router=router is not None),
        grid=(B, nt),
        in_specs=in_specs,
        out_specs=tuple(out_specs),
        out_shape=tuple(out_shape),
        scratch_shapes=[pltpu.VMEM((TM + 2 * POOL_HALO, C_WIDTH), F32)],
        compiler_params=_cparams(("arbitrary", "arbitrary")),
        name="merge_router" if router is not None else "merge",
    )(*args)


def _swiglu_acc(h, w1_ref, w3_ref, w2_ref, lead):
    d_ff = w1_ref.shape[-1]
    acc = None
    for f0 in range(0, d_ff, FC):
        f1 = min(f0 + FC, d_ff)
        a = _dot(h, w1_ref[lead + (slice(None), slice(f0, f1))])
        b = _dot(h, w3_ref[lead + (slice(None), slice(f0, f1))])
        t = (a * jax.nn.sigmoid(a) * b).astype(BF16)
        part = _dot(t, w2_ref[lead + (slice(f0, f1), slice(None))])
        acc = part if acc is None else acc + part
    return acc


def _ffn_kernel(h_ref, x_ref, g_ref, w1_ref, w3_ref, w2_ref, o_ref):
    f = _swiglu_acc(h_ref[0], w1_ref, w3_ref, w2_ref, (0,))
    o_ref[0] = x_ref[0] + g_ref[0] * f


def _ffn_dense(h2, xs, gate2, w1, w3, w2, idx, nct):
    B, T, D = xs.shape
    F = w1.shape[-1]
    nt = T // TM
    tok = lambda: pl.BlockSpec((1, TM, D), lambda b, i: (b, i, 0))
    return pl.pallas_call(
        _ffn_kernel,
        grid=(B, nt),
        in_specs=[tok(), tok(),
                  pl.BlockSpec((1, 1, D), lambda b, i: (jnp.where(i < nct, B, b), 0, 0)),
                  _layer_spec((D, F), idx), _layer_spec((D, F), idx), _layer_spec((F, D), idx)],
        out_specs=tok(),
        out_shape=jax.ShapeDtypeStruct((B, T, D), F32),
        compiler_params=_cparams(("arbitrary", "arbitrary")),
        name="ffn_dense",
    )(h2, xs, gate2, w1, w3, w2)


def _gather_kernel(wj_ref, ws_ref, wf_ref, wl_ref, wv_ref, te_ref, pos_ref, cmb_ref, h_ref,
                   o_ref, pw_ref, acc_ref, accw_ref):
    w = pl.program_id(0)

    @pl.when(wf_ref[w] == 1)
    def _():
        acc_ref[...] = jnp.zeros_like(acc_ref)
        accw_ref[...] = jnp.zeros_like(accw_ref)

    @pl.when(wv_ref[w] == 1)
    def _():
        row = lax.broadcasted_iota(jnp.int32, (TX, 1), 0) + wj_ref[w] * TX
        hit = pos_ref[0, 0] == row
        acc_ref[...] += _dot(jnp.where(hit, 1.0, 0.0).astype(BF16), h_ref[...])
        accw_ref[...] += jnp.sum(jnp.where(hit, cmb_ref[0, 0], 0.0), axis=-1, keepdims=True)

    @pl.when(wl_ref[w] == 1)
    def _():
        o_ref[...] = acc_ref[...].astype(BF16)
        pw_ref[...] = accw_ref[...]


def _moe_gather(work, tile_expert, pos_t, cmb_t, h2f, n_tiles):
    wj, ws, wf, wl, wv = work
    n, D = h2f.shape
    row_spec = pl.BlockSpec((1, 1, 1, TX), lambda w, wj, ws, wf, wl, wv, te: (te[wj[w]], ws[w], 0, 0))
    return pl.pallas_call(
        _gather_kernel,
        grid_spec=pltpu.PrefetchScalarGridSpec(
            num_scalar_prefetch=6,
            grid=(wj.shape[0],),
            in_specs=[row_spec, row_spec,
                      pl.BlockSpec((TX, D), lambda w, wj, ws, wf, wl, wv, te: (ws[w], 0))],
            out_specs=(pl.BlockSpec((TX, D), lambda w, wj, ws, wf, wl, wv, te: (wj[w], 0)),
                       pl.BlockSpec((TX, 1), lambda w, wj, ws, wf, wl, wv, te: (wj[w], 0))),
            scratch_shapes=[pltpu.VMEM((TX, D), F32), pltpu.VMEM((TX, 1), F32)]),
        out_shape=(jax.ShapeDtypeStruct((n_tiles * TX, D), BF16),
                   jax.ShapeDtypeStruct((n_tiles * TX, 1), F32)),
        compiler_params=_cparams(("arbitrary",)),
        name="moe_gather",
    )(wj, ws, wf, wl, wv, tile_expert, pos_t, cmb_t, h2f)


def _gffn_kernel(te_ref, tv_ref, x_ref, pw_ref, w1_ref, w3_ref, w2_ref, o_ref):
    j = pl.program_id(0)

    @pl.when(tv_ref[j] == 1)
    def _():
        f = _swiglu_acc(x_ref[...], w1_ref, w3_ref, w2_ref, (0, 0))
        o_ref[...] = (f * pw_ref[...]).astype(BF16)

    @pl.when(tv_ref[j] == 0)
    def _():
        o_ref[...] = jnp.zeros_like(o_ref)


def _moe_gffn(tile_expert, tile_valid, xg, pw, w1, w3, w2, idx):
    P, D = xg.shape
    F = w1.shape[-1]
    wspec = lambda shape: pl.BlockSpec((1, 1) + shape, lambda j, te, tv: (idx, te[j], 0, 0))
    return pl.pallas_call(
        _gffn_kernel,
        grid_spec=pltpu.PrefetchScalarGridSpec(
            num_scalar_prefetch=2,
            grid=(P // TX,),
            in_specs=[pl.BlockSpec((TX, D), lambda j, te, tv: (j, 0)),
                      pl.BlockSpec((TX, 1), lambda j, te, tv: (j, 0)),
                      wspec((D, F)), wspec((D, F)), wspec((F, D))],
            out_specs=pl.BlockSpec((TX, D), lambda j, te, tv: (j, 0))),
        out_shape=jax.ShapeDtypeStruct((P, D), BF16),
        compiler_params=_cparams(("arbitrary",)),
        name="moe_gffn",
    )(tile_expert, tile_valid, xg, pw, w1, w3, w2)


def _combine_kernel(wt_ref, ws_ref, wf_ref, wl_ref, wv_ref, pos_ref, y_ref, x_ref, g_ref, o_ref, acc_ref,
                    *, n_batch, T, L):
    w = pl.program_id(0)

    @pl.when(wf_ref[w] == 1)
    def _():
        acc_ref[...] = jnp.zeros_like(acc_ref)

    @pl.when(wv_ref[w] == 1)
    def _():
        col = lax.broadcasted_iota(jnp.int32, (1, TX), 1) + ws_ref[w] * TX
        pos = pos_ref[...]
        hit = (pos[:, 0:1] == col) | (pos[:, 1:2] == col)
        acc_ref[...] += _dot(jnp.where(hit, 1.0, 0.0).astype(BF16), y_ref[...])

    @pl.when(wl_ref[w] == 1)
    def _():
        n = lax.broadcasted_iota(jnp.int32, (TX, 1), 0) + wt_ref[w] * TX
        b = jnp.zeros((TX, 1), jnp.int32)
        for k in range(1, n_batch):
            b = b + (n >= k * T).astype(jnp.int32)
        rowid = jnp.where(n - b * T < L, n_batch, b)
        gate = jnp.zeros((TX, D_MODEL), F32)
        for r in range(n_batch + 1):
            gate = jnp.where(rowid == r, g_ref[r:r + 1, :], gate)
        o_ref[...] = x_ref[...] + gate * acc_ref[...]


def _moe_combine(work, pos2, yw, xf, gate2, n_batch, T, L):
    wt, ws, wf, wl, wv = work
    n, D = xf.shape
    return pl.pallas_call(
        functools.partial(_combine_kernel, n_batch=n_batch, T=T, L=L),
        grid_spec=pltpu.PrefetchScalarGridSpec(
            num_scalar_prefetch=5,
            grid=(wt.shape[0],),
            in_specs=[pl.BlockSpec((TX, 2), lambda w, wt, ws, wf, wl, wv: (wt[w], 0)),
                      pl.BlockSpec((TX, D), lambda w, wt, ws, wf, wl, wv: (ws[w], 0)),
                      pl.BlockSpec((TX, D), lambda w, wt, ws, wf, wl, wv: (wt[w], 0)),
                      pl.BlockSpec((SUBLANES, D), lambda w, wt, ws, wf, wl, wv: (0, 0))],
            out_specs=pl.BlockSpec((TX, D), lambda w, wt, ws, wf, wl, wv: (wt[w], 0)),
            scratch_shapes=[pltpu.VMEM((TX, D), F32)]),
        out_shape=jax.ShapeDtypeStruct((n, D), F32),
        compiler_params=_cparams(("arbitrary",)),
        name="moe_combine",
    )(wt, ws, wf, wl, wv, pos2, yw, xf, gate2)


def _count_below(sorted_vals, x):
    return jnp.sum((sorted_vals[None, :] < x[:, None]).astype(jnp.int32), axis=1)


def _with_flags(g, it, compute, n_work, total):
    live = jnp.arange(n_work, dtype=jnp.int32) < total
    g_prev = jnp.concatenate([jnp.full((1,), -1, jnp.int32), g[:-1]])
    g_next = jnp.concatenate([g[1:], jnp.full((1,), -1, jnp.int32)])
    live_next = jnp.concatenate([live[1:], jnp.zeros((1,), bool)])
    first = live & (g != g_prev)
    last = live & ((g != g_next) | ~live_next)
    i32 = lambda a: a.astype(jnp.int32)
    return i32(g), i32(it), i32(first), i32(last), i32(live & compute)


def _moe_plan(sel, n_tok):
    E = N_EXPERTS
    sel = sel.astype(jnp.int32)
    cnt = jnp.sum(sel, axis=0)
    rank = jnp.cumsum(sel, axis=0) - 1
    gsz = ((cnt + TX - 1) // TX) * TX
    gend = jnp.cumsum(gsz)
    goff = gend - gsz
    n_tiles = (2 * n_tok + E * (TX - 1) + TX - 1) // TX
    P = n_tiles * TX
    pos = jnp.where(sel == 1, goff[None, :] + rank, -1)
    tile_start = jnp.arange(n_tiles, dtype=jnp.int32) * TX
    tile_valid = tile_start < gend[-1]
    tile_expert = jnp.minimum(_count_below(gend, tile_start + 1), E - 1)
    pmax_tok = jnp.max(pos, axis=1)
    psec_tok = jnp.max(jnp.where(pos == pmax_tok[:, None], -1, pos), axis=1)
    pos2 = jnp.stack([psec_tok, pmax_tok], axis=1)

    n_src_tiles = n_tok // TX
    cin = rank[TX - 1::TX] + 1
    ra = tile_start - goff[tile_expert]
    rb = ra + jnp.clip(cnt[tile_expert] - ra, 0, TX) - 1
    cin_t = cin.T[tile_expert]
    lo = jnp.sum((cin_t <= ra[:, None]).astype(jnp.int32), axis=1)
    hi = jnp.sum((cin_t <= rb[:, None]).astype(jnp.int32), axis=1)
    span = jnp.where(tile_valid, hi - lo + 1, 1)
    n_gw = E * n_src_tiles + n_tiles
    cs = jnp.cumsum(span)
    total = cs[-1]
    w = jnp.minimum(jnp.arange(n_gw, dtype=jnp.int32), total - 1)
    jw = _count_below(cs, w + 1)
    sw = jnp.where(tile_valid[jw], lo[jw] + (w - (cs[jw] - span[jw])), 0)
    gwork = _with_flags(jw, sw, tile_valid[jw], n_gw, total)

    pt = pos.reshape(n_src_tiles, TX, E)
    pmax = jnp.max(pt, axis=1)
    pmin = jnp.min(jnp.where(pt >= 0, pt, P), axis=1)
    ta = pmin // TX
    tb = pmax // TX
    cand_item = jnp.stack([ta, tb], axis=-1).reshape(-1)
    cand_valid = jnp.stack([pmax >= 0, (pmax >= 0) & (tb != ta)], axis=-1).reshape(-1).astype(jnp.int32)
    cand_group = jnp.repeat(jnp.arange(n_src_tiles, dtype=jnp.int32), 2 * E)
    n_cw = E * n_src_tiles + n_tiles
    ccs = jnp.cumsum(cand_valid)
    ctotal = ccs[-1]
    cw = jnp.minimum(jnp.arange(n_cw, dtype=jnp.int32), ctotal - 1)
    cidx = _count_below(ccs, cw + 1)
    cwork = _with_flags(cand_group[cidx], cand_item[cidx], jnp.ones((n_cw,), bool), n_cw, ctotal)
    return pos, pos2, tile_expert, tile_valid.astype(jnp.int32), gwork, cwork, n_tiles


def _moe(h2, xs1, cmb, sel, gate2, w1, w3, w2, idx, L):
    B, T, D = xs1.shape
    n_tok = B * T
    E = N_EXPERTS
    assert n_tok % TX == 0
    selm = sel.reshape(n_tok, LANES)[:, :E] > 0.5
    cmbm = cmb.reshape(n_tok, LANES)[:, :E]
    pos, pos2, tile_expert, tile_valid, gwork, cwork, n_tiles = _moe_plan(selm, n_tok)
    pos_t = pos.T.reshape(E, n_tok // TX, 1, TX)
    cmb_t = cmbm.T.reshape(E, n_tok // TX, 1, TX)

    xg, pw = _moe_gather(gwork, tile_expert, pos_t, cmb_t, h2.reshape(n_tok, D), n_tiles)
    yw = _moe_gffn(tile_expert, tile_valid, xg, pw, w1, w3, w2, idx)
    out = _moe_combine(cwork, pos2, yw, xs1.reshape(n_tok, D), gate2, B, T, L)
    return out.reshape(B, T, D)


def _final_kernel(x_ref, g_ref, o_ref):
    x = x_ref[0]
    o_ref[0] = x * lax.rsqrt(jnp.mean(x * x, axis=-1, keepdims=True) + EPS) * g_ref[...]


def _final_norm(xs, g, nct, S):
    B, T, D = xs.shape
    return pl.pallas_call(
        _final_kernel,
        grid=(B, S // TM),
        in_specs=[pl.BlockSpec((1, TM, D), lambda b, i: (b, i + nct, 0)), _const_spec((1, D))],
        out_specs=pl.BlockSpec((1, TM, D), lambda b, i: (b, i, 0)),
        out_shape=jax.ShapeDtypeStruct((B, S, D), F32),
        compiler_params=_cparams(("arbitrary", "arbitrary")),
        name="final_norm",
    )(xs, g)


def _rope_tables(L, S):
    t = np.arange(S)
    pos = np.stack([t // GRID_W, t % GRID_W], axis=-1).astype(np.float32)
    inv_freq = (ROPE_THETA ** (-np.arange(0, AXIS_ROPE_DIM, 2, dtype=np.float32) / AXIS_ROPE_DIM)).astype(np.float32)
    ang = pos[:, :, None] * inv_freq[None, None, :]
    cos, sin = np.cos(ang), np.sin(ang)
    cos64 = np.concatenate([cos[:, 0], cos[:, 0], cos[:, 1], cos[:, 1]], axis=-1)
    sin64 = np.concatenate([-sin[:, 0], sin[:, 0], -sin[:, 1], sin[:, 1]], axis=-1)
    cos_t = np.concatenate([np.ones((L, HEAD_DIM), np.float32), cos64], axis=0)
    sin_t = np.concatenate([np.zeros((L, HEAD_DIM), np.float32), sin64], axis=0)
    rep = LANES // HEAD_DIM
    return jnp.asarray(np.tile(cos_t, (1, rep)), F32), jnp.asarray(np.tile(sin_t, (1, rep)), F32)


def _na_bias_table(rel_bias):
    rpt = TM // GRID_W
    half = NA_WIN_ROWS // 2
    assert half <= rpt and rpt - 1 - half + NA_WIN_ROWS <= 2 * rpt and rpt + half + 1 >= NA_WIN_ROWS
    col = np.arange(GRID_W)
    col_start = np.clip(col - NA_WIN_COLS // 2, 0, GRID_W - NA_WIN_COLS)
    kc = np.arange(GRID_W)
    inside = (kc[None, :] >= col_start[:, None]) & (kc[None, :] < col_start[:, None] + NA_WIN_COLS)
    relp = jnp.pad(rel_bias, ((0, 0), (0, 0), (0, 0), (GRID_W, GRID_W)))
    off = GRID_W + NA_WIN_COLS - 1
    cmat = jnp.stack([relp[..., off - qc:off - qc + GRID_W] for qc in range(GRID_W)], axis=3)
    cmat = jnp.where(jnp.asarray(inside), cmat * LOG2E, MASK_VALUE)
    masked = jnp.full(cmat.shape[:2] + (GRID_W, GRID_W), MASK_VALUE, F32)

    cases = []
    for case in range(3):
        rows = []
        for a in range(rpt):
            first = (rpt + max(a - half, 0),
                     rpt + a - half,
                     min(rpt + a - half, 2 * rpt - NA_WIN_ROWS))[case]
            blocks = [cmat[:, :, t - (rpt + a) + NA_WIN_ROWS - 1] if first <= t < first + NA_WIN_ROWS else masked
                      for t in range(3 * rpt)]
            rows.append(jnp.concatenate(blocks, axis=-1))
        cases.append(jnp.concatenate(rows, axis=-2))
    return jnp.stack(cases, axis=1)


def _block_diag_ones(n, blk):
    idx = np.arange(n) // blk
    return jnp.asarray((idx[:, None] == idx[None, :]).astype(np.float32), BF16)


def _pool_block_diag(pool_w):
    depth, g, c, d = pool_w.shape
    eye = jnp.asarray(np.eye(g, dtype=np.float32))
    out = pool_w[:, :, :, None, :] * eye[None, :, None, :, None]
    return out.reshape(depth, g * c, g * d).astype(BF16)


def kernel(x, c, ctx, c_ctx, w_mod, b_mod, norm1_g, norm2_g, w_in, q_norm_g, k_norm_g, na_rel_bias, pool_w,
           pool_scale, w_branch_a, w_branch_b, w_branch_c, w_out, ffn_w1, ffn_w3, ffn_w2, router_w, router_b,
           moe_w1, moe_w3, moe_w2, final_g):
    B, S, D = x.shape
    L = ctx.shape[1]
    T = L + S
    depth = w_mod.shape[0]
    assert D == D_MODEL and L % TM == 0 and S % TM == 0 and TM % GRID_W == 0 and B + 1 <= SUBLANES
    assert S // GRID_W >= NA_WIN_ROWS and len(POOL_WINDOWS) == C_GROUPS
    assert list(POOL_WINDOWS) == sorted(POOL_WINDOWS) and POOL_WINDOWS[-1] - POOL_WINDOWS[-1] // 2 <= POOL_HALO
    nct = L // TM

    cvec = jnp.zeros((SUBLANES, D), F32).at[:B].set(c).at[B].set(c_ctx)
    mod = _mod_vectors(cvec, w_mod, b_mod)
    cos_t, sin_t = _rope_tables(L, S)
    bd = _block_diag_ones(QK_W, HEAD_DIM)
    na_bias = _na_bias_table(na_rel_bias)
    bf = lambda a: a.astype(BF16)
    w_qkv, w_gate = bf(w_in[:, :, :OFF_G]), bf(w_in[:, :, OFF_G:])
    wpa, wpb, wpc, wo, pbd = bf(w_branch_a), bf(w_branch_b), bf(w_branch_c), bf(w_out), _pool_block_diag(pool_w)
    fw1, fw3, fw2 = bf(ffn_w1), bf(ffn_w3), bf(ffn_w2)
    mw1, mw3, mw2 = bf(moe_w1), bf(moe_w3), bf(moe_w2)

    xs = jnp.concatenate([ctx, x], axis=1)
    for layer in range(depth):
        modr = mod[layer].reshape(SUBLANES, 6, D)
        g1 = norm1_g[layer].reshape(1, D)
        gqk = jnp.concatenate([jnp.tile(q_norm_g[layer], A_HEADS), jnp.tile(k_norm_g[layer], A_KV_HEADS)])
        q, kt, v, bq, bk, bv, cu = _inproj(xs, modr, g1, w_qkv, layer, bd, gqk.reshape(1, QK_W), cos_t, sin_t, nct)
        ya = _attn_a(q, kt, v, nct)
        yb = _attn_b(bq, bk, bv, na_bias, layer, nct, L)
        is_moe = layer % 2 == 1
        i = layer // 2
        router = None
        if is_moe:
            rw = jnp.zeros((D, LANES), F32).at[:, :N_EXPERTS].set(router_w[i])
            rw_hi = rw.astype(BF16)
            rw_lo = (rw - rw_hi.astype(F32)).astype(BF16)
            rb = jnp.zeros((1, LANES), F32).at[0, :N_EXPERTS].set(router_b[i])
            router = (jnp.concatenate([rw_hi, rw_lo], axis=1), rb)
        outs = _merge(ya, yb, cu, xs, modr, g1, w_gate, wpa, wpb, wpc, pbd, pool_scale[layer].reshape(1, C_WIDTH),
                      wo, norm2_g[layer].reshape(1, D), layer, router, nct, L)
        if is_moe:
            xs1, h2, cmb, sel = outs
            xs = _moe(h2, xs1, cmb, sel, modr[:, 5, :], mw1, mw3, mw2, i, L)
        else:
            xs1, h2 = outs
            xs = _ffn_dense(h2, xs1, modr[:, 5:6, :], fw1, fw3, fw2, i, nct)
    return _final_norm(xs, final_g.reshape(1, D), nct, S)
```

```python
import functools

import numpy as np
import jax
import jax.numpy as jnp
from jax import lax
from jax.experimental import pallas as pl
from jax.experimental.pallas import tpu as pltpu

F32 = jnp.float32
BF16 = jnp.bfloat16

D_MODEL = 1024
DEPTH = 4
GRID_W = 64
HEAD_DIM = 64
A_HEADS = 8
A_KV_HEADS = 2
A_GROUP = A_HEADS // A_KV_HEADS
B_HEADS = 4
C_GROUPS = 4
C_GROUP_DIM = 64
C_WIDTH = C_GROUPS * C_GROUP_DIM
POOL_WINDOWS = (2, 4, 8, 16)
NA_WIN_ROWS = 8
NA_WIN_COLS = 16
ROPE_THETA = 10000.0
AXIS_ROPE_DIM = HEAD_DIM // 2
N_EXPERTS = 8
EPS = 1e-6

A_Q = A_HEADS * HEAD_DIM
A_KV = A_KV_HEADS * HEAD_DIM
B_W = B_HEADS * HEAD_DIM
QK_W = A_Q + A_KV
OFF_AV = QK_W
OFF_BQ = OFF_AV + A_KV
OFF_BK = OFF_BQ + B_W
OFF_BV = OFF_BK + B_W
OFF_CU = OFF_BV + B_W
OFF_G = OFF_CU + C_WIDTH

LANES = 128
SUBLANES = 8
TM = 256
TX = 512
FC = 512
A_KBLK = 8
LOG2E = 1.4426950408889634
POOL_HALO = 8
MASK_VALUE = -1e30
VMEM_LIMIT = 56 * 1024 * 1024


def _cparams(sem):
    return pltpu.CompilerParams(dimension_semantics=sem, vmem_limit_bytes=VMEM_LIMIT)


def _dot(a, b):
    return jnp.dot(a, b, preferred_element_type=F32)


def _dot_nt(a, b):
    return lax.dot_general(a, b, (((1,), (1,)), ((), ())), preferred_element_type=F32)


def _split(a):
    hi = a.astype(BF16)
    lo = (a - hi.astype(F32)).astype(BF16)
    return hi, lo


def _dot3(a, b_hi, b_lo):
    a_hi, a_lo = _split(a)
    return _dot(a_hi, b_hi) + (_dot(a_lo, b_hi) + _dot(a_hi, b_lo))


def _const_spec(shape):
    n = len(shape)
    return pl.BlockSpec(shape, lambda *_: (0,) * n, pipeline_mode=pl.Buffered(1))


def _layer_spec(shape, layer):
    n = len(shape)
    return pl.BlockSpec((1,) + tuple(shape), lambda *_: (layer,) + (0,) * n, pipeline_mode=pl.Buffered(1))


def _rms_mod(x, g, shift, scale):
    y = x * lax.rsqrt(jnp.mean(x * x, axis=-1, keepdims=True) + EPS) * g
    return y * (1.0 + scale) + shift


def _mod_kernel(c_ref, w_ref, b_ref, o_ref):
    c = c_ref[...]
    s = c * jax.nn.sigmoid(c)
    w_hi, w_lo = _split(w_ref[0])
    o_ref[0] = _dot3(s, w_hi, w_lo) + b_ref[0]


def _mod_vectors(cvec, w_mod, b_mod):
    depth, d, n = w_mod.shape
    tn = 1536
    return pl.pallas_call(
        _mod_kernel,
        grid=(depth, n // tn),
        in_specs=[pl.BlockSpec((SUBLANES, d), lambda l, j: (0, 0)),
                  pl.BlockSpec((1, d, tn), lambda l, j: (l, 0, j)),
                  pl.BlockSpec((1, 1, tn), lambda l, j: (l, 0, j))],
        out_specs=pl.BlockSpec((1, SUBLANES, tn), lambda l, j: (l, 0, j)),
        out_shape=jax.ShapeDtypeStruct((depth, SUBLANES, n), F32),
        compiler_params=_cparams(("arbitrary", "arbitrary")),
        name="mod_vectors",
    )(cvec, w_mod, b_mod.reshape(depth, 1, n))


def _rotate_half(x):
    half = AXIS_ROPE_DIM // 2
    lane = lax.broadcasted_iota(jnp.int32, (1, LANES), 1)
    first = (lane % AXIS_ROPE_DIM) < half
    return jnp.where(first, pltpu.roll(x, LANES - half, 1), pltpu.roll(x, half, 1))


def _inproj_kernel(x_ref, mod_ref, g_ref, w_ref, bd_ref, gqk_ref, cos_ref, sin_ref,
                   q_ref, kt_ref, v_ref, bq_ref, bk_ref, bv_ref, cu_ref):
    m = mod_ref[0]
    h = _rms_mod(x_ref[0], g_ref[...], m[0:1], m[1:2])
    px = _dot(h.astype(BF16), w_ref[0])

    qk = px[:, :QK_W]
    ss = _dot((qk * qk).astype(BF16), bd_ref[...])
    qn = qk * lax.rsqrt(ss * (1.0 / HEAD_DIM) + EPS) * gqk_ref[...]
    cos = cos_ref[...]
    sin = sin_ref[...]
    chunks = []
    for j in range(QK_W // LANES):
        c = qn[:, j * LANES:(j + 1) * LANES]
        chunks.append(c * cos + _rotate_half(c) * sin)
    scale = HEAD_DIM ** -0.5
    for j in range(A_Q // LANES):
        q_ref[0, :, j * LANES:(j + 1) * LANES] = (chunks[j] * (scale * LOG2E)).astype(BF16)
    kt_ref[0, 0] = chunks[A_Q // LANES].T.astype(BF16)

    ones = jnp.ones((TM, LANES - HEAD_DIM), BF16)
    for j in range(A_KV_HEADS):
        vj = px[:, OFF_AV + j * HEAD_DIM:OFF_AV + (j + 1) * HEAD_DIM].astype(BF16)
        v_ref[0, :, j * LANES:(j + 1) * LANES] = jnp.concatenate([vj, ones], axis=1)
    bq_ref[0] = (px[:, OFF_BQ:OFF_BK] * (scale * LOG2E)).astype(BF16)
    bk_ref[0] = px[:, OFF_BK:OFF_BV].astype(BF16)
    for h in range(B_HEADS):
        vh = px[:, OFF_BV + h * HEAD_DIM:OFF_BV + (h + 1) * HEAD_DIM].astype(BF16)
        bv_ref[0, :, h * LANES:(h + 1) * LANES] = jnp.concatenate([vh, ones], axis=1)
    cu_ref[0] = px[:, OFF_CU:OFF_G]


def _inproj(xs, modr, g1, w_qkv, layer, bd, gqk, cos_t, sin_t, nct):
    B, T, D = xs.shape
    nt = T // TM
    tok = lambda w: pl.BlockSpec((1, TM, w), lambda b, i: (b, i, 0))
    out_shape = (
        jax.ShapeDtypeStruct((B, T, A_Q), BF16),
        jax.ShapeDtypeStruct((B, nt, A_KV, TM), BF16),
        jax.ShapeDtypeStruct((B, T, A_KV_HEADS * LANES), BF16),
        jax.ShapeDtypeStruct((B, T, B_W), BF16),
        jax.ShapeDtypeStruct((B, T, B_W), BF16),
        jax.ShapeDtypeStruct((B, T, B_HEADS * LANES), BF16),
        jax.ShapeDtypeStruct((B, T, C_WIDTH), F32),
    )
    return pl.pallas_call(
        _inproj_kernel,
        grid=(B, nt),
        in_specs=[tok(D),
                  pl.BlockSpec((1, 6, D), lambda b, i: (jnp.where(i < nct, B, b), 0, 0)),
                  _const_spec((1, D)),
                  _layer_spec((D, OFF_G), layer),
                  _const_spec((QK_W, QK_W)),
                  _const_spec((1, QK_W)),
                  pl.BlockSpec((TM, LANES), lambda b, i: (i, 0)),
                  pl.BlockSpec((TM, LANES), lambda b, i: (i, 0))],
        out_specs=(tok(A_Q),
                   pl.BlockSpec((1, 1, A_KV, TM), lambda b, i: (b, i, 0, 0)),
                   tok(A_KV_HEADS * LANES), tok(B_W), tok(B_W), tok(B_HEADS * LANES), tok(C_WIDTH)),
        out_shape=out_shape,
        compiler_params=_cparams(("arbitrary", "arbitrary")),
        name="inproj",
    )(xs, modr, g1, w_qkv, bd, gqk, cos_t, sin_t)


def _attn_a_kernel(q_ref, kt_ref, v_ref, o_ref, *, nct, n_steps):
    i = pl.program_id(1)
    rows = A_GROUP * TM

    def step(carry, q4, j, blk0, nblk):
        m, acc = carry
        s_list = [_dot(q4, kt_ref[0, blk0 + c, j * HEAD_DIM:(j + 1) * HEAD_DIM, :]) for c in range(nblk)]
        smax = s_list[0]
        for s in s_list[1:]:
            smax = jnp.maximum(smax, s)
        m_new = jnp.maximum(m, jnp.max(smax, axis=-1, keepdims=True))
        alpha = jnp.exp2(m - m_new)
        p = jnp.concatenate([jnp.exp2(s - m_new).astype(BF16) for s in s_list], axis=1)
        vv = v_ref[0, pl.ds(pl.multiple_of(blk0 * TM, TM), nblk * TM), j * LANES:(j + 1) * LANES]
        return m_new, alpha * acc + _dot(p, vv)

    def run(n_main):
        for j in range(A_KV_HEADS):
            base = j * A_GROUP * HEAD_DIM
            q4 = jnp.concatenate(
                [q_ref[0, :, base + g * HEAD_DIM: base + (g + 1) * HEAD_DIM] for g in range(A_GROUP)], axis=0)
            carry = (jnp.full((rows, 1), MASK_VALUE, F32), jnp.zeros((rows, LANES), F32))
            if n_main:
                carry = step(carry, q4, j, 0, nct + A_KBLK)
                carry = lax.fori_loop(
                    1, n_main, lambda k, cr: step(cr, q4, j, nct + k * A_KBLK, A_KBLK), carry)
            else:
                carry = step(carry, q4, j, 0, nct)
            acc = carry[1]
            o = acc[:, 0:HEAD_DIM] / acc[:, HEAD_DIM:HEAD_DIM + 1]
            for g in range(A_GROUP):
                o_ref[0, :, base + g * HEAD_DIM: base + (g + 1) * HEAD_DIM] = o[g * TM:(g + 1) * TM].astype(BF16)

    @pl.when(i < nct)
    def _():
        run(0)

    @pl.when(i >= nct)
    def _():
        run(n_steps)


def _attn_a(q, kt, v, nct):
    B, T, _ = q.shape
    nt = T // TM
    assert (nt - nct) % A_KBLK == 0
    return pl.pallas_call(
        functools.partial(_attn_a_kernel, nct=nct, n_steps=(nt - nct) // A_KBLK),
        grid=(B, nt),
        in_specs=[pl.BlockSpec((1, TM, A_Q), lambda b, i: (b, i, 0)),
                  pl.BlockSpec((1, nt, A_KV, TM), lambda b, i: (b, 0, 0, 0)),
                  pl.BlockSpec((1, T, A_KV_HEADS * LANES), lambda b, i: (b, 0, 0))],
        out_specs=pl.BlockSpec((1, TM, A_Q), lambda b, i: (b, i, 0)),
        out_shape=jax.ShapeDtypeStruct((B, T, A_Q), BF16),
        compiler_params=_cparams(("arbitrary", "arbitrary")),
        name="attn_a",
    )(q, kt, v)


def _attn_b_kernel(q_ref, kp_ref, kc_ref, kn_ref, vp_ref, vc_ref, vn_ref, kx_ref, vx_ref, bias_ref,
                   o_ref, kbuf, vbuf, *, nct, L):
    i = pl.program_id(1)

    def normalised(o_ext):
        return (o_ext[:, 0:HEAD_DIM] / o_ext[:, HEAD_DIM:HEAD_DIM + 1]).astype(BF16)

    @pl.when(i < nct)
    def _():
        for h in range(B_HEADS):
            sl = slice(h * HEAD_DIM, (h + 1) * HEAD_DIM)
            vl = slice(h * LANES, (h + 1) * LANES)
            s = _dot_nt(q_ref[0, :, sl], kx_ref[0, :, sl])
            m = jnp.max(s, axis=-1, keepdims=True)
            p = jnp.exp2((s - m).astype(BF16))
            o_ref[0, :, sl] = normalised(_dot(p, vx_ref[0, :, vl]))

    @pl.when(i >= nct)
    def _():
        for buf, cx, pv, cu, nx in ((kbuf, kx_ref, kp_ref, kc_ref, kn_ref), (vbuf, vx_ref, vp_ref, vc_ref, vn_ref)):
            buf[0:L] = cx[0]
            buf[L:L + TM] = pv[0]
            buf[L + TM:L + 2 * TM] = cu[0]
            buf[L + 2 * TM:L + 3 * TM] = nx[0]
        for h in range(B_HEADS):
            sl = slice(h * HEAD_DIM, (h + 1) * HEAD_DIM)
            s = _dot_nt(q_ref[0, :, sl], kbuf[:, sl])
            s_c = s[:, 0:L]
            s_w = s[:, L:] + bias_ref[0, 0, h]
            m = jnp.maximum(jnp.max(s_w, axis=-1, keepdims=True), jnp.max(s_c, axis=-1, keepdims=True))
            p = jnp.concatenate([jnp.exp2((s_c - m).astype(BF16)), jnp.exp2((s_w - m).astype(BF16))], axis=1)
            o_ref[0, :, sl] = normalised(_dot(p, vbuf[:, h * LANES:(h + 1) * LANES]))


def _attn_b(bq, bk, bv, bias_t, layer, nct, L):
    B, T, _ = bq.shape
    nt = T // TM
    assert L % LANES == 0 and nt - nct >= 2
    VW = B_HEADS * LANES
    case = lambda i: jnp.where(i <= nct, 0, jnp.where(i == nt - 1, 2, 1))
    cur = lambda b, i: (b, i, 0)
    prev = lambda b, i: (b, jnp.maximum(i - 1, nct), 0)
    nxt = lambda b, i: (b, jnp.minimum(i + 1, nt - 1), 0)
    blk = lambda f: pl.BlockSpec((1, TM, B_W), f)
    vblk = lambda f: pl.BlockSpec((1, TM, VW), f)
    ctx = lambda w: pl.BlockSpec((1, L, w), lambda b, i: (b, 0, 0))
    return pl.pallas_call(
        functools.partial(_attn_b_kernel, nct=nct, L=L),
        grid=(B, nt),
        in_specs=[blk(cur), blk(prev), blk(cur), blk(nxt), vblk(prev), vblk(cur), vblk(nxt), ctx(B_W), ctx(VW),
                  pl.BlockSpec((1, 1) + bias_t.shape[2:], lambda b, i: (layer, case(i), 0, 0, 0))],
        out_specs=blk(cur),
        out_shape=jax.ShapeDtypeStruct((B, T, B_W), BF16),
        scratch_shapes=[pltpu.VMEM((L + 3 * TM, B_W), BF16), pltpu.VMEM((L + 3 * TM, VW), BF16)],
        compiler_params=_cparams(("arbitrary", "arbitrary")),
        name="attn_b",
    )(bq, bk, bk, bk, bv, bv, bv, bk, bv, bias_t)


def _merge_kernel(*refs, nct, nt, L, S, with_router):
    if with_router:
        (ya_ref, yb_ref, cup_ref, cu_ref, cun_ref, x_ref, mod_ref, g1_ref, wg_ref, wpa_ref, wpb_ref, wpc_ref,
         pbd_ref, psc_ref, wo_ref, g2_ref, rw_ref, rb_ref,
         xo_ref, h_ref, cmb_ref, sel_ref, e_scr) = refs
    else:
        (ya_ref, yb_ref, cup_ref, cu_ref, cun_ref, x_ref, mod_ref, g1_ref, wg_ref, wpa_ref, wpb_ref, wpc_ref,
         pbd_ref, psc_ref, wo_ref, g2_ref, xo_ref, h_ref, e_scr) = refs
    i = pl.program_id(1)
    D = D_MODEL
    m = mod_ref[0]
    x_in = x_ref[0]

    at_start = (i == 0) | (i == nct)
    at_end = (i == nct - 1) | (i == nt - 1)
    u = cu_ref[0]
    e_scr[0:POOL_HALO] = jnp.where(at_start, 0.0, cup_ref[0])
    e_scr[POOL_HALO:POOL_HALO + TM] = u
    e_scr[POOL_HALO + TM:] = jnp.where(at_end, 0.0, cun_ref[0])
    sh = lambda k: e_scr[POOL_HALO + k:POOL_HALO + k + TM]
    grp = lax.broadcasted_iota(jnp.int32, (1, C_WIDTH), 1) // C_GROUP_DIM
    t_loc = lax.broadcasted_iota(jnp.int32, (TM, 1), 0)
    t_seq = jnp.where(i < nct, i * TM, (i - nct) * TM) + t_loc
    n_seq = jnp.where(i < nct, L, S)
    wsum, running, lo, hi = None, None, 0, 0
    lo_v = jnp.zeros((1, C_WIDTH), jnp.int32)
    hi_v = jnp.zeros((1, C_WIDTH), jnp.int32)
    for g, w in enumerate(POOL_WINDOWS):
        for k in list(range(-(w // 2), lo)) + list(range(hi, w - w // 2)):
            running = sh(k) if running is None else running + sh(k)
        lo, hi = -(w // 2), w - w // 2
        wsum = running if wsum is None else jnp.where(grp == g, running, wsum)
        lo_v = jnp.where(grp == g, lo, lo_v)
        hi_v = jnp.where(grp == g, hi, hi_v)
    cnt = jnp.minimum(t_seq + hi_v, n_seq) - jnp.maximum(t_seq + lo_v, 0)
    dlt = wsum / cnt.astype(F32) - u
    yc = _dot(dlt.astype(BF16), pbd_ref[0]) * psc_ref[...]

    h1 = _rms_mod(x_in, g1_ref[...], m[0:1], m[1:2]).astype(BF16)
    branches = (ya_ref[0], yb_ref[0], yc.astype(BF16))
    weights = (wpa_ref, wpb_ref, wpc_ref)
    mrg = None
    for k in range(3):
        gate = jax.nn.sigmoid(_dot(h1, wg_ref[0, :, k * D:(k + 1) * D]))
        term = gate * _dot(branches[k], weights[k][0])
        mrg = term if mrg is None else mrg + term
    x = x_in + m[2:3] * _dot(mrg.astype(BF16), wo_ref[0])
    xo_ref[0] = x
    h = _rms_mod(x, g2_ref[...], m[3:4], m[4:5])
    h_ref[0] = h.astype(BF16)

    if with_router:
        lane = lax.broadcasted_iota(jnp.int32, (1, LANES), 1).astype(F32)
        h_hi, h_lo = _split(h)
        d_hi = _dot(h_hi, rw_ref[...])
        d_lo = _dot(h_lo, rw_ref[...])
        lg = d_hi[:, :LANES] + (d_hi[:, LANES:] + d_lo[:, :LANES]) + rb_ref[...]
        lg = jnp.where(lane < N_EXPERTS, lg, -jnp.inf)
        m1 = jnp.max(lg, axis=-1, keepdims=True)
        i1 = jnp.min(jnp.where(lg == m1, lane, float(LANES)), axis=-1, keepdims=True)
        mask1 = lane == i1
        lg2 = jnp.where(mask1, -jnp.inf, lg)
        m2 = jnp.max(lg2, axis=-1, keepdims=True)
        i2 = jnp.min(jnp.where(lg2 == m2, lane, float(LANES)), axis=-1, keepdims=True)
        mask2 = lane == i2
        e2 = jnp.exp(m2 - m1)
        den = 1.0 + e2
        cmb_ref[0] = jnp.where(mask1, 1.0 / den, 0.0) + jnp.where(mask2, e2 / den, 0.0)
        sel_ref[0] = jnp.where(mask1 | mask2, 1.0, 0.0)


def _merge(ya, yb, cu, xs, modr, g1, w_gate, wpa, wpb, wpc, pbd, psc, wo, g2, layer, router, nct, L):
    B, T, D = xs.shape
    nt = T // TM
    S = T - L
    hb = TM // POOL_HALO
    tok = lambda w: pl.BlockSpec((1, TM, w), lambda b, i: (b, i, 0))
    in_specs = [tok(A_Q), tok(B_W),
                pl.BlockSpec((1, POOL_HALO, C_WIDTH), lambda b, i: (b, jnp.maximum(i * hb - 1, 0), 0)),
                tok(C_WIDTH),
                pl.BlockSpec((1, POOL_HALO, C_WIDTH), lambda b, i: (b, jnp.minimum((i + 1) * hb, nt * hb - 1), 0)),
                tok(D),
                pl.BlockSpec((1, 6, D), lambda b, i: (jnp.where(i < nct, B, b), 0, 0)),
                _const_spec((1, D)),
                _layer_spec((D, 3 * D), layer),
                _layer_spec((A_Q, D), layer), _layer_spec((B_W, D), layer), _layer_spec((C_WIDTH, D), layer),
                _layer_spec((C_WIDTH, C_WIDTH), layer), _const_spec((1, C_WIDTH)), _layer_spec((D, D), layer),
                _const_spec((1, D))]
    args = [ya, yb, cu, cu, cu, xs, modr, g1, w_gate, wpa, wpb, wpc, pbd, psc, wo, g2]
    out_specs = [tok(D), tok(D)]
    out_shape = [jax.ShapeDtypeStruct((B, T, D), F32), jax.ShapeDtypeStruct((B, T, D), BF16)]
    if router is not None:
        in_specs += [_const_spec((D, 2 * LANES)), _const_spec((1, LANES))]
        args += list(router)
        out_specs += [tok(LANES), tok(LANES)]
        out_shape += [jax.ShapeDtypeStruct((B, T, LANES), F32)] * 2
    return pl.pallas_call(
        functools.partial(_merge_kernel, nct=nct, nt=nt, L=L, S=S, with_router=router is not None),
        grid=(B, nt),
        in_specs=in_specs,
        out_specs=tuple(out_specs),
        out_shape=tuple(out_shape),
        scratch_shapes=[pltpu.VMEM((TM + 2 * POOL_HALO, C_WIDTH), F32)],
        compiler_params=_cparams(("arbitrary", "arbitrary")),
        name="merge_router" if router is not None else "merge",
    )(*args)


def _swiglu_acc(h, w1_ref, w3_ref, w2_ref, lead):
    d_ff = w1_ref.shape[-1]
    acc = None
    for f0 in range(0, d_ff, FC):
        f1 = min(f0 + FC, d_ff)
        a = _dot(h, w1_ref[lead + (slice(None), slice(f0, f1))])
        b = _dot(h, w3_ref[lead + (slice(None), slice(f0, f1))])
        t = (a * jax.nn.sigmoid(a) * b).astype(BF16)
        part = _dot(t, w2_ref[lead + (slice(f0, f1), slice(None))])
        acc = part if acc is None else acc + part
    return acc


def _ffn_kernel(h_ref, x_ref, g_ref, w1_ref, w3_ref, w2_ref, o_ref):
    f = _swiglu_acc(h_ref[0], w1_ref, w3_ref, w2_ref, (0,))
    o_ref[0] = x_ref[0] + g_ref[0] * f


def _ffn_dense(h2, xs, gate2, w1, w3, w2, idx, nct):
    B, T, D = xs.shape
    F = w1.shape[-1]
    nt = T // TM
    tok = lambda: pl.BlockSpec((1, TM, D), lambda b, i: (b, i, 0))
    return pl.pallas_call(
        _ffn_kernel,
        grid=(B, nt),
        in_specs=[tok(), tok(),
                  pl.BlockSpec((1, 1, D), lambda b, i: (jnp.where(i < nct, B, b), 0, 0)),
                  _layer_spec((D, F), idx), _layer_spec((D, F), idx), _layer_spec((F, D), idx)],
        out_specs=tok(),
        out_shape=jax.ShapeDtypeStruct((B, T, D), F32),
        compiler_params=_cparams(("arbitrary", "arbitrary")),
        name="ffn_dense",
    )(h2, xs, gate2, w1, w3, w2)


def _gather_kernel(wj_ref, ws_ref, wf_ref, wl_ref, wv_ref, te_ref, pos_ref, cmb_ref, h_ref,
                   o_ref, pw_ref, acc_ref, accw_ref):
    w = pl.program_id(0)

    @pl.when(wf_ref[w] == 1)
    def _():
        acc_ref[...] = jnp.zeros_like(acc_ref)
        accw_ref[...] = jnp.zeros_like(accw_ref)

    @pl.when(wv_ref[w] == 1)
    def _():
        row = lax.broadcasted_iota(jnp.int32, (TX, 1), 0) + wj_ref[w] * TX
        hit = pos_ref[0, 0] == row
        acc_ref[...] += _dot(jnp.where(hit, 1.0, 0.0).astype(BF16), h_ref[...])
        accw_ref[...] += jnp.sum(jnp.where(hit, cmb_ref[0, 0], 0.0), axis=-1, keepdims=True)

    @pl.when(wl_ref[w] == 1)
    def _():
        o_ref[...] = acc_ref[...].astype(BF16)
        pw_ref[...] = accw_ref[...]


def _moe_gather(work, tile_expert, pos_t, cmb_t, h2f, n_tiles):
    wj, ws, wf, wl, wv = work
    n, D = h2f.shape
    row_spec = pl.BlockSpec((1, 1, 1, TX), lambda w, wj, ws, wf, wl, wv, te: (te[wj[w]], ws[w], 0, 0))
    return pl.pallas_call(
        _gather_kernel,
        grid_spec=pltpu.PrefetchScalarGridSpec(
            num_scalar_prefetch=6,
            grid=(wj.shape[0],),
            in_specs=[row_spec, row_spec,
                      pl.BlockSpec((TX, D), lambda w, wj, ws, wf, wl, wv, te: (ws[w], 0))],
            out_specs=(pl.BlockSpec((TX, D), lambda w, wj, ws, wf, wl, wv, te: (wj[w], 0)),
                       pl.BlockSpec((TX, 1), lambda w, wj, ws, wf, wl, wv, te: (wj[w], 0))),
            scratch_shapes=[pltpu.VMEM((TX, D), F32), pltpu.VMEM((TX, 1), F32)]),
        out_shape=(jax.ShapeDtypeStruct((n_tiles * TX, D), BF16),
                   jax.ShapeDtypeStruct((n_tiles * TX, 1), F32)),
        compiler_params=_cparams(("arbitrary",)),
        name="moe_gather",
    )(wj, ws, wf, wl, wv, tile_expert, pos_t, cmb_t, h2f)


def _gffn_kernel(te_ref, tv_ref, x_ref, pw_ref, w1_ref, w3_ref, w2_ref, o_ref):
    j = pl.program_id(0)

    @pl.when(tv_ref[j] == 1)
    def _():
        f = _swiglu_acc(x_ref[...], w1_ref, w3_ref, w2_ref, (0, 0))
        o_ref[...] = (f * pw_ref[...]).astype(BF16)

    @pl.when(tv_ref[j] == 0)
    def _():
        o_ref[...] = jnp.zeros_like(o_ref)


def _moe_gffn(tile_expert, tile_valid, xg, pw, w1, w3, w2, idx):
    P, D = xg.shape
    F = w1.shape[-1]
    wspec = lambda shape: pl.BlockSpec((1, 1) + shape, lambda j, te, tv: (idx, te[j], 0, 0))
    return pl.pallas_call(
        _gffn_kernel,
        grid_spec=pltpu.PrefetchScalarGridSpec(
            num_scalar_prefetch=2,
            grid=(P // TX,),
            in_specs=[pl.BlockSpec((TX, D), lambda j, te, tv: (j, 0)),
                      pl.BlockSpec((TX, 1), lambda j, te, tv: (j, 0)),
                      wspec((D, F)), wspec((D, F)), wspec((F, D))],
            out_specs=pl.BlockSpec((TX, D), lambda j, te, tv: (j, 0))),
        out_shape=jax.ShapeDtypeStruct((P, D), BF16),
        compiler_params=_cparams(("arbitrary",)),
        name="moe_gffn",
    )(tile_expert, tile_valid, xg, pw, w1, w3, w2)


def _combine_kernel(wt_ref, ws_ref, wf_ref, wl_ref, wv_ref, pos_ref, y_ref, x_ref, g_ref, o_ref, acc_ref,
                    *, n_batch, T, L):
    w = pl.program_id(0)

    @pl.when(wf_ref[w] == 1)
    def _():
        acc_ref[...] = jnp.zeros_like(acc_ref)

    @pl.when(wv_ref[w] == 1)
    def _():
        col = lax.broadcasted_iota(jnp.int32, (1, TX), 1) + ws_ref[w] * TX
        pos = pos_ref[...]
        hit = (pos[:, 0:1] == col) | (pos[:, 1:2] == col)
        acc_ref[...] += _dot(jnp.where(hit, 1.0, 0.0).astype(BF16), y_ref[...])

    @pl.when(wl_ref[w] == 1)
    def _():
        n = lax.broadcasted_iota(jnp.int32, (TX, 1), 0) + wt_ref[w] * TX
        b = jnp.zeros((TX, 1), jnp.int32)
        for k in range(1, n_batch):
            b = b + (n >= k * T).astype(jnp.int32)
        rowid = jnp.where(n - b * T < L, n_batch, b)
        gate = jnp.zeros((TX, D_MODEL), F32)
        for r in range(n_batch + 1):
            gate = jnp.where(rowid == r, g_ref[r:r + 1, :], gate)
        o_ref[...] = x_ref[...] + gate * acc_ref[...]


def _moe_combine(work, pos2, yw, xf, gate2, n_batch, T, L):
    wt, ws, wf, wl, wv = work
    n, D = xf.shape
    return pl.pallas_call(
        functools.partial(_combine_kernel, n_batch=n_batch, T=T, L=L),
        grid_spec=pltpu.PrefetchScalarGridSpec(
            num_scalar_prefetch=5,
            grid=(wt.shape[0],),
            in_specs=[pl.BlockSpec((TX, 2), lambda w, wt, ws, wf, wl, wv: (wt[w], 0)),
                      pl.BlockSpec((TX, D), lambda w, wt, ws, wf, wl, wv: (ws[w], 0)),
                      pl.BlockSpec((TX, D), lambda w, wt, ws, wf, wl, wv: (wt[w], 0)),
                      pl.BlockSpec((SUBLANES, D), lambda w, wt, ws, wf, wl, wv: (0, 0))],
            out_specs=pl.BlockSpec((TX, D), lambda w, wt, ws, wf, wl, wv: (wt[w], 0)),
            scratch_shapes=[pltpu.VMEM((TX, D), F32)]),
        out_shape=jax.ShapeDtypeStruct((n, D), F32),
        compiler_params=_cparams(("arbitrary",)),
        name="moe_combine",
    )(wt, ws, wf, wl, wv, pos2, yw, xf, gate2)


def _count_below(sorted_vals, x):
    return jnp.sum((sorted_vals[None, :] < x[:, None]).astype(jnp.int32), axis=1)


def _with_flags(g, it, compute, n_work, total):
    live = jnp.arange(n_work, dtype=jnp.int32) < total
    g_prev = jnp.concatenate([jnp.full((1,), -1, jnp.int32), g[:-1]])
    g_next = jnp.concatenate([g[1:], jnp.full((1,), -1, jnp.int32)])
    live_next = jnp.concatenate([live[1:], jnp.zeros((1,), bool)])
    first = live & (g != g_prev)
    last = live & ((g != g_next) | ~live_next)
    i32 = lambda a: a.astype(jnp.int32)
    return i32(g), i32(it), i32(first), i32(last), i32(live & compute)


def _moe_plan(sel, n_tok):
    E = N_EXPERTS
    sel = sel.astype(jnp.int32)
    cnt = jnp.sum(sel, axis=0)
    rank = jnp.cumsum(sel, axis=0) - 1
    gsz = ((cnt + TX - 1) // TX) * TX
    gend = jnp.cumsum(gsz)
    goff = gend - gsz
    n_tiles = (2 * n_tok + E * (TX - 1) + TX - 1) // TX
    P = n_tiles * TX
    pos = jnp.where(sel == 1, goff[None, :] + rank, -1)
    tile_start = jnp.arange(n_tiles, dtype=jnp.int32) * TX
    tile_valid = tile_start < gend[-1]
    tile_expert = jnp.minimum(_count_below(gend, tile_start + 1), E - 1)
    pmax_tok = jnp.max(pos, axis=1)
    psec_tok = jnp.max(jnp.where(pos == pmax_tok[:, None], -1, pos), axis=1)
    pos2 = jnp.stack([psec_tok, pmax_tok], axis=1)

    n_src_tiles = n_tok // TX
    cin = rank[TX - 1::TX] + 1
    ra = tile_start - goff[tile_expert]
    rb = ra + jnp.clip(cnt[tile_expert] - ra, 0, TX) - 1
    cin_t = cin.T[tile_expert]
    lo = jnp.sum((cin_t <= ra[:, None]).astype(jnp.int32), axis=1)
    hi = jnp.sum((cin_t <= rb[:, None]).astype(jnp.int32), axis=1)
    span = jnp.where(tile_valid, hi - lo + 1, 1)
    n_gw = E * n_src_tiles + n_tiles
    cs = jnp.cumsum(span)
    total = cs[-1]
    w = jnp.minimum(jnp.arange(n_gw, dtype=jnp.int32), total - 1)
    jw = _count_below(cs, w + 1)
    sw = jnp.where(tile_valid[jw], lo[jw] + (w - (cs[jw] - span[jw])), 0)
    gwork = _with_flags(jw, sw, tile_valid[jw], n_gw, total)

    pt = pos.reshape(n_src_tiles, TX, E)
    pmax = jnp.max(pt, axis=1)
    pmin = jnp.min(jnp.where(pt >= 0, pt, P), axis=1)
    ta = pmin // TX
    tb = pmax // TX
    cand_item = jnp.stack([ta, tb], axis=-1).reshape(-1)
    cand_valid = jnp.stack([pmax >= 0, (pmax >= 0) & (tb != ta)], axis=-1).reshape(-1).astype(jnp.int32)
    cand_group = jnp.repeat(jnp.arange(n_src_tiles, dtype=jnp.int32), 2 * E)
    n_cw = E * n_src_tiles + n_tiles
    ccs = jnp.cumsum(cand_valid)
    ctotal = ccs[-1]
    cw = jnp.minimum(jnp.arange(n_cw, dtype=jnp.int32), ctotal - 1)
    cidx = _count_below(ccs, cw + 1)
    cwork = _with_flags(cand_group[cidx], cand_item[cidx], jnp.ones((n_cw,), bool), n_cw, ctotal)
    return pos, pos2, tile_expert, tile_valid.astype(jnp.int32), gwork, cwork, n_tiles


def _moe(h2, xs1, cmb, sel, gate2, w1, w3, w2, idx, L):
    B, T, D = xs1.shape
    n_tok = B * T
    E = N_EXPERTS
    assert n_tok % TX == 0
    selm = sel.reshape(n_tok, LANES)[:, :E] > 0.5
    cmbm = cmb.reshape(n_tok, LANES)[:, :E]
    pos, pos2, tile_expert, tile_valid, gwork, cwork, n_tiles = _moe_plan(selm, n_tok)
    pos_t = pos.T.reshape(E, n_tok // TX, 1, TX)
    cmb_t = cmbm.T.reshape(E, n_tok // TX, 1, TX)

    xg, pw = _moe_gather(gwork, tile_expert, pos_t, cmb_t, h2.reshape(n_tok, D), n_tiles)
    yw = _moe_gffn(tile_expert, tile_valid, xg, pw, w1, w3, w2, idx)
    out = _moe_combine(cwork, pos2, yw, xs1.reshape(n_tok, D), gate2, B, T, L)
    return out.reshape(B, T, D)


def _final_kernel(x_ref, g_ref, o_ref):
    x = x_ref[0]
    o_ref[0] = x * lax.rsqrt(jnp.mean(x * x, axis=-1, keepdims=True) + EPS) * g_ref[...]


def _final_norm(xs, g, nct, S):
    B, T, D = xs.shape
    return pl.pallas_call(
        _final_kernel,
        grid=(B, S // TM),
        in_specs=[pl.BlockSpec((1, TM, D), lambda b, i: (b, i + nct, 0)), _const_spec((1, D))],
        out_specs=pl.BlockSpec((1, TM, D), lambda b, i: (b, i, 0)),
        out_shape=jax.ShapeDtypeStruct((B, S, D), F32),
        compiler_params=_cparams(("arbitrary", "arbitrary")),
        name="final_norm",
    )(xs, g)


def _rope_tables(L, S):
    t = np.arange(S)
    pos = np.stack([t // GRID_W, t % GRID_W], axis=-1).astype(np.float32)
    inv_freq = (ROPE_THETA ** (-np.arange(0, AXIS_ROPE_DIM, 2, dtype=np.float32) / AXIS_ROPE_DIM)).astype(np.float32)
    ang = pos[:, :, None] * inv_freq[None, None, :]
    cos, sin = np.cos(ang), np.sin(ang)
    cos64 = np.concatenate([cos[:, 0], cos[:, 0], cos[:, 1], cos[:, 1]], axis=-1)
    sin64 = np.concatenate([-sin[:, 0], sin[:, 0], -sin[:, 1], sin[:, 1]], axis=-1)
    cos_t = np.concatenate([np.ones((L, HEAD_DIM), np.float32), cos64], axis=0)
    sin_t = np.concatenate([np.zeros((L, HEAD_DIM), np.float32), sin64], axis=0)
    rep = LANES // HEAD_DIM
    return jnp.asarray(np.tile(cos_t, (1, rep)), F32), jnp.asarray(np.tile(sin_t, (1, rep)), F32)


def _na_bias_table(rel_bias):
    rpt = TM // GRID_W
    half = NA_WIN_ROWS // 2
    assert half <= rpt and rpt - 1 - half + NA_WIN_ROWS <= 2 * rpt and rpt + half + 1 >= NA_WIN_ROWS
    col = np.arange(GRID_W)
    col_start = np.clip(col - NA_WIN_COLS // 2, 0, GRID_W - NA_WIN_COLS)
    kc = np.arange(GRID_W)
    inside = (kc[None, :] >= col_start[:, None]) & (kc[None, :] < col_start[:, None] + NA_WIN_COLS)
    relp = jnp.pad(rel_bias, ((0, 0), (0, 0), (0, 0), (GRID_W, GRID_W)))
    off = GRID_W + NA_WIN_COLS - 1
    cmat = jnp.stack([relp[..., off - qc:off - qc + GRID_W] for qc in range(GRID_W)], axis=3)
    cmat = jnp.where(jnp.asarray(inside), cmat * LOG2E, MASK_VALUE)
    masked = jnp.full(cmat.shape[:2] + (GRID_W, GRID_W), MASK_VALUE, F32)

    cases = []
    for case in range(3):
        rows = []
        for a in range(rpt):
            first = (rpt + max(a - half, 0),
                     rpt + a - half,
                     min(rpt + a - half, 2 * rpt - NA_WIN_ROWS))[case]
            blocks = [cmat[:, :, t - (rpt + a) + NA_WIN_ROWS - 1] if first <= t < first + NA_WIN_ROWS else masked
                      for t in range(3 * rpt)]
            rows.append(jnp.concatenate(blocks, axis=-1))
        cases.append(jnp.concatenate(rows, axis=-2))
    return jnp.stack(cases, axis=1)


def _block_diag_ones(n, blk):
    idx = np.arange(n) // blk
    return jnp.asarray((idx[:, None] == idx[None, :]).astype(np.float32), BF16)


def _pool_block_diag(pool_w):
    depth, g, c, d = pool_w.shape
    eye = jnp.asarray(np.eye(g, dtype=np.float32))
    out = pool_w[:, :, :, None, :] * eye[None, :, None, :, None]
    return out.reshape(depth, g * c, g * d).astype(BF16)


def kernel(x, c, ctx, c_ctx, w_mod, b_mod, norm1_g, norm2_g, w_in, q_norm_g, k_norm_g, na_rel_bias, pool_w,
           pool_scale, w_branch_a, w_branch_b, w_branch_c, w_out, ffn_w1, ffn_w3, ffn_w2, router_w, router_b,
           moe_w1, moe_w3, moe_w2, final_g):
    B, S, D = x.shape
    L = ctx.shape[1]
    T = L + S
    depth = w_mod.shape[0]
    assert D == D_MODEL and L % TM == 0 and S % TM == 0 and TM % GRID_W == 0 and B + 1 <= SUBLANES
    assert S // GRID_W >= NA_WIN_ROWS and len(POOL_WINDOWS) == C_GROUPS
    assert list(POOL_WINDOWS) == sorted(POOL_WINDOWS) and POOL_WINDOWS[-1] - POOL_WINDOWS[-1] // 2 <= POOL_HALO
    nct = L // TM

    cvec = jnp.zeros((SUBLANES, D), F32).at[:B].set(c).at[B].set(c_ctx)
    mod = _mod_vectors(cvec, w_mod, b_mod)
    cos_t, sin_t = _rope_tables(L, S)
    bd = _block_diag_ones(QK_W, HEAD_DIM)
    na_bias = _na_bias_table(na_rel_bias)
    bf = lambda a: a.astype(BF16)
    w_qkv, w_gate = bf(w_in[:, :, :OFF_G]), bf(w_in[:, :, OFF_G:])
    wpa, wpb, wpc, wo, pbd = bf(w_branch_a), bf(w_branch_b), bf(w_branch_c), bf(w_out), _pool_block_diag(pool_w)
    fw1, fw3, fw2 = bf(ffn_w1), bf(ffn_w3), bf(ffn_w2)
    mw1, mw3, mw2 = bf(moe_w1), bf(moe_w3), bf(moe_w2)

    xs = jnp.concatenate([ctx, x], axis=1)
    for layer in range(depth):
        modr = mod[layer].reshape(SUBLANES, 6, D)
        g1 = norm1_g[layer].reshape(1, D)
        gqk = jnp.concatenate([jnp.tile(q_norm_g[layer], A_HEADS), jnp.tile(k_norm_g[layer], A_KV_HEADS)])
        q, kt, v, bq, bk, bv, cu = _inproj(xs, modr, g1, w_qkv, layer, bd, gqk.reshape(1, QK_W), cos_t, sin_t, nct)
        ya = _attn_a(q, kt, v, nct)
        yb = _attn_b(bq, bk, bv, na_bias, layer, nct, L)
        is_moe = layer % 2 == 1
        i = layer // 2
        router = None
        if is_moe:
            rw = jnp.zeros((D, LANES), F32).at[:, :N_EXPERTS].set(router_w[i])
            rw_hi = rw.astype(BF16)
            rw_lo = (rw - rw_hi.astype(F32)).astype(BF16)
            rb = jnp.zeros((1, LANES), F32).at[0, :N_EXPERTS].set(router_b[i])
            router = (jnp.concatenate([rw_hi, rw_lo], axis=1), rb)
        outs = _merge(ya, yb, cu, xs, modr, g1, w_gate, wpa, wpb, wpc, pbd, pool_scale[layer].reshape(1, C_WIDTH),
                      wo, norm2_g[layer].reshape(1, D), layer, router, nct, L)
        if is_moe:
            xs1, h2, cmb, sel = outs
            xs = _moe(h2, xs1, cmb, sel, modr[:, 5, :], mw1, mw3, mw2, i, L)
        else:
            xs1, h2 = outs
            xs = _ffn_dense(h2, xs1, modr[:, 5:6, :], fw1, fw3, fw2, i, nct)
    return _final_norm(xs, final_g.reshape(1, D), nct, S)
```
